```python
import math
import jax, jax.numpy as jnp
from jax import lax
import numpy as np

D_MODEL = 4096
BATCH = 1
SEQ = 16384
DEPTH = 1

MIX_WIDTH = D_MODEL
NSA_WIDTH = D_MODEL // 2
GM_WIDTH = D_MODEL // 4
MEM_WIDTH = D_MODEL // 4
DH = 128
N_HEADS_NSA = NSA_WIDTH // DH
HEADS_PER_GROUP = 4
N_KV = N_HEADS_NSA // HEADS_PER_GROUP
KV_WIDTH = N_KV * DH
N_BRANCH = 3
CMP_LEN = 32
CMP_STRIDE = 16
SLC_BLOCK = 64
SLC_RATIO = SLC_BLOCK // CMP_STRIDE
TOP_N = 16
WINDOW = 512
Q_BLOCK = 128
CHUNK = 128
GM_GROUPS = 4
GM_DG = GM_WIDTH // GM_GROUPS
MEM_LEN = 256
MEM_HEADS = 4
MEM_DH = MEM_WIDTH // MEM_HEADS
LN_EPS = 1e-5
SPLIT_SIZES = (NSA_WIDTH,
               KV_WIDTH, KV_WIDTH,
               KV_WIDTH, KV_WIDTH,
               KV_WIDTH, KV_WIDTH,
               N_HEADS_NSA * N_BRANCH,
               NSA_WIDTH,
               GM_WIDTH, GM_WIDTH, GM_WIDTH,
               MEM_WIDTH, MEM_WIDTH)
IN_WIDTH = 2 * NSA_WIDTH + 6 * KV_WIDTH + N_HEADS_NSA * N_BRANCH + 3 * GM_WIDTH + 2 * MEM_WIDTH

kernel_name = "hybrid_nsa_gmlp_memxattn_deepnorm"


def _layernorm(x, g, b):
    xf = x.astype(jnp.float32)
    mu = jnp.mean(xf, axis=-1, keepdims=True)
    var = jnp.mean(jnp.square(xf - mu), axis=-1, keepdims=True)
    return ((xf - mu) * lax.rsqrt(var + LN_EPS) * g.astype(jnp.float32) + b.astype(jnp.float32)).astype(x.dtype)


def _masked_softmax(s, mask):
    s = jnp.where(mask, s.astype(jnp.float32), -jnp.inf)
    m = jnp.max(s, axis=-1, keepdims=True)
    m = jnp.where(jnp.isfinite(m), m, 0.0)
    p = jnp.where(mask, jnp.exp(s - m), 0.0)
    return p / jnp.maximum(jnp.sum(p, axis=-1, keepdims=True), 1e-30)


def _alibi_slopes(n):
    return jnp.exp2(-8.0 * jnp.arange(1, n + 1, dtype=jnp.float32) / n)


def _compress(a, pe, w1, w2):
    B, G, T, _ = a.shape
    ch = a.reshape(B, G, T // CMP_STRIDE, CMP_STRIDE, DH)
    blk = jnp.concatenate([ch[:, :, :-1], ch[:, :, 1:]], axis=3)
    nc = blk.shape[2]
    blk = (blk + pe).reshape(B, G, nc, CMP_LEN * DH)
    return jax.nn.gelu(blk @ w1) @ w2


def setup_inputs(seed: int = 0) -> dict:
    key = jax.random.key(seed)
    ks = jax.random.split(key, 20)
    f32 = jnp.float32

    def nrm(k, shape, scale):
        return jax.random.normal(k, shape, f32) * scale

    beta = (8.0 * DEPTH) ** -0.25
    return {
        "x": nrm(ks[0], (BATCH, SEQ, D_MODEL), 1.0),
        "mem": nrm(ks[1], (BATCH, MEM_LEN, D_MODEL), 1.0),
        "w_in": nrm(ks[2], (D_MODEL, IN_WIDTH), D_MODEL ** -0.5),
        "w_cmp_k1": nrm(ks[3], (CMP_LEN * DH, DH), (CMP_LEN * DH) ** -0.5),
        "w_cmp_k2": nrm(ks[4], (DH, DH), DH ** -0.5),
        "w_cmp_v1": nrm(ks[5], (CMP_LEN * DH, DH), (CMP_LEN * DH) ** -0.5),
        "w_cmp_v2": nrm(ks[6], (DH, DH), DH ** -0.5),
        "pe_cmp_k": nrm(ks[7], (CMP_LEN, DH), 0.02),
        "pe_cmp_v": nrm(ks[8], (CMP_LEN, DH), 0.02),
        "gm_ln_g": 1.0 + nrm(ks[9], (GM_WIDTH,), 0.01),
        "gm_ln_b": nrm(ks[10], (GM_WIDTH,), 0.01),
        "w_spatial": nrm(ks[11], (GM_GROUPS, CHUNK, CHUNK), 0.1 * CHUNK ** -0.5),
        "b_spatial": 1.0 + nrm(ks[12], (GM_GROUPS, CHUNK), 0.01),
        "w_mem_kv": nrm(ks[13], (D_MODEL, 2 * MEM_WIDTH), D_MODEL ** -0.5),
        "w_out": nrm(ks[14], (MIX_WIDTH, D_MODEL), beta * MIX_WIDTH ** -0.5),
        "ln_g": 1.0 + nrm(ks[15], (D_MODEL,), 0.01),
        "ln_b": nrm(ks[16], (D_MODEL,), 0.01),
    }


def _nsa(q, k_c, v_c, k_s, v_s, k_w, v_w, gates, w_cmp_k1, w_cmp_k2, w_cmp_v1, w_cmp_v2,
         pe_cmp_k, pe_cmp_v):
    B, T, _ = q.shape
    G, HG = N_KV, HEADS_PER_GROUP
    scale = DH ** -0.5
    qh = q.reshape(B, T, G, HG, DH).transpose(0, 2, 3, 1, 4)

    def kvh(a):
        return a.reshape(B, T, G, DH).transpose(0, 2, 1, 3)

    kc = _compress(kvh(k_c), pe_cmp_k, w_cmp_k1, w_cmp_k2)
    vc = _compress(kvh(v_c), pe_cmp_v, w_cmp_v1, w_cmp_v2)
    n_cmp = kc.shape[2]
    cmp_end = jnp.arange(n_cmp) * CMP_STRIDE + (CMP_LEN - 1)

    n_slc = T // SLC_BLOCK
    n_top = min(TOP_N, n_slc)
    ks_blocks = kvh(k_s).reshape(B, G, n_slc, SLC_BLOCK, DH)
    vs_blocks = kvh(v_s).reshape(B, G, n_slc, SLC_BLOCK, DH)

    kw_pad = jnp.pad(kvh(k_w), ((0, 0), (0, 0), (WINDOW, 0), (0, 0)))
    vw_pad = jnp.pad(kvh(v_w), ((0, 0), (0, 0), (WINDOW, 0), (0, 0)))

    g_all = jax.nn.sigmoid(gates.reshape(B, T, G, HG, N_BRANCH).transpose(0, 2, 3, 1, 4))
    slope = _alibi_slopes(N_HEADS_NSA).reshape(G, HG)[None, :, :, None, None]
    bi = jnp.arange(B)[:, None, None, None]
    gi = jnp.arange(G)[None, :, None, None]
    jblk = jnp.arange(n_slc)

    def block(i):
        q0 = i * Q_BLOCK
        qb = lax.dynamic_slice_in_dim(qh, q0, Q_BLOCK, axis=3)
        t = q0 + jnp.arange(Q_BLOCK)

        dist_c = t[:, None] - cmp_end[None, :]
        s_c = jnp.einsum('bghqd,bgkd->bghqk', qb, kc) * scale - slope * dist_c
        p_c = _masked_softmax(s_c, dist_c >= 0)
        o_c = jnp.einsum('bghqk,bgkd->bghqd', p_c, vc.astype(jnp.float32))

        imp = jnp.sum(p_c, axis=2)
        imp_p = jnp.pad(imp, ((0, 0), (0, 0), (0, 0), (1, 1)))
        imp_s = (imp_p[..., :SLC_RATIO * n_slc].reshape(B, G, Q_BLOCK, n_slc, SLC_RATIO).sum(-1)
                 + imp_p[..., SLC_RATIO::SLC_RATIO])
        cur = (t // SLC_BLOCK)[:, None]
        forced = (jblk[None] == 0) | (jblk[None] == cur) | (jblk[None] == cur - 1)
        sel_score = jnp.where(jblk[None] > cur, -jnp.inf, jnp.where(forced, jnp.inf, imp_s))
        _, idx = lax.top_k(sel_score, n_top)

        ks_g = ks_blocks[bi, gi, idx].reshape(B, G, Q_BLOCK, n_top * SLC_BLOCK, DH)
        vs_g = vs_blocks[bi, gi, idx].reshape(B, G, Q_BLOCK, n_top * SLC_BLOCK, DH)
        pos = (idx[..., None] * SLC_BLOCK + jnp.arange(SLC_BLOCK)).reshape(B, G, Q_BLOCK, n_top * SLC_BLOCK)
        dist_s = (t[None, None, :, None] - pos)[:, :, None]
        s_s = jnp.einsum('bghqd,bgqkd->bghqk', qb, ks_g) * scale - slope * dist_s
        p_s = _masked_softmax(s_s, dist_s >= 0)
        o_s = jnp.einsum('bghqk,bgqkd->bghqd', p_s, vs_g.astype(jnp.float32))

        kw_b = lax.dynamic_slice_in_dim(kw_pad, q0, Q_BLOCK + WINDOW, axis=2)
        vw_b = lax.dynamic_slice_in_dim(vw_pad, q0, Q_BLOCK + WINDOW, axis=2)
        spos = q0 - WINDOW + jnp.arange(Q_BLOCK + WINDOW)
        dist_w = t[:, None] - spos[None, :]
        mask_w = (dist_w >= 0) & (dist_w < WINDOW) & (spos[None, :] >= 0)
        s_w = jnp.einsum('bghqd,bgkd->bghqk', qb, kw_b) * scale - slope * dist_w
        p_w = _masked_softmax(s_w, mask_w)
        o_w = jnp.einsum('bghqk,bgkd->bghqd', p_w, vw_b.astype(jnp.float32))

        gb = lax.dynamic_slice_in_dim(g_all, q0, Q_BLOCK, axis=3).astype(jnp.float32)
        return gb[..., 0:1] * o_c + gb[..., 1:2] * o_s + gb[..., 2:3] * o_w

    out = lax.map(block, jnp.arange(T // Q_BLOCK))
    out = out.transpose(1, 0, 4, 2, 3, 5).reshape(B, T, NSA_WIDTH)
    return out.astype(q.dtype)


def _gmlp(u, v, gm_ln_g, gm_ln_b, w_spatial, b_spatial):
    B, T, _ = u.shape
    v = _layernorm(v, gm_ln_g, gm_ln_b)
    vc = v.reshape(B, T // CHUNK, CHUNK, GM_GROUPS, GM_DG)
    causal = jnp.tril(jnp.ones((CHUNK, CHUNK), dtype=bool))
    ws = jnp.where(causal[None], w_spatial, 0.0)
    sv = jnp.einsum('gts,bcsgd->bctgd', ws, vc) + b_spatial.T[None, None, :, :, None]
    return (u.reshape(B, T // CHUNK, CHUNK, GM_GROUPS, GM_DG) * sv).reshape(B, T, GM_WIDTH).astype(u.dtype)


def _mem_xattn(q_m, mem, w_mem_kv):
    B, T, _ = q_m.shape
    kv = jnp.einsum('bmd,dn->bmn', mem, w_mem_kv)
    k_m, v_m = jnp.split(kv, 2, axis=-1)
    k_m = k_m.reshape(B, MEM_LEN, MEM_HEADS, MEM_DH)
    v_m = v_m.reshape(B, MEM_LEN, MEM_HEADS, MEM_DH)
    qh = q_m.reshape(B, T, MEM_HEADS, MEM_DH)
    s = jnp.einsum('bthd,bmhd->bhtm', qh, k_m) * (MEM_DH ** -0.5)
    p = jax.nn.softmax(s.astype(jnp.float32), axis=-1)
    o = jnp.einsum('bhtm,bmhd->bthd', p, v_m.astype(jnp.float32))
    return o.reshape(B, T, MEM_WIDTH).astype(q_m.dtype)


def reference(x, mem, w_in, w_cmp_k1, w_cmp_k2, w_cmp_v1, w_cmp_v2, pe_cmp_k, pe_cmp_v,
              gm_ln_g, gm_ln_b, w_spatial, b_spatial, w_mem_kv, w_out, ln_g, ln_b):
    alpha = (2.0 * DEPTH) ** 0.25
    offs = np.cumsum(np.array(SPLIT_SIZES))[:-1].tolist()
    for _layer in range(DEPTH):
        proj = jnp.einsum('btd,dn->btn', x, w_in)
        (q, k_c, v_c, k_s, v_s, k_w, v_w, gates, z_nsa,
         u, v, z_gm, q_m, z_m) = jnp.split(proj, offs, axis=-1)
        y_nsa = _nsa(q, k_c, v_c, k_s, v_s, k_w, v_w, gates, w_cmp_k1, w_cmp_k2,
                     w_cmp_v1, w_cmp_v2, pe_cmp_k, pe_cmp_v) * jax.nn.silu(z_nsa)
        y_gm = _gmlp(u, v, gm_ln_g, gm_ln_b, w_spatial, b_spatial) * jax.nn.silu(z_gm)
        y_mem = _mem_xattn(q_m, mem, w_mem_kv) * jax.nn.silu(z_m)
        h = jnp.einsum('btn,nd->btd', jnp.concatenate([y_nsa, y_gm, y_mem], axis=-1), w_out)
        x = _layernorm(alpha * x + h, ln_g, ln_b)
    return x
```

```python
import functools

import numpy as np
import jax
import jax.numpy as jnp
from jax import lax
from jax.experimental import pallas as pl
from jax.experimental.pallas import tpu as pltpu

D_MODEL = 4096
DH = 128
N_HEADS = 16
HG = 4
N_KV = 4
NSA_W = N_HEADS * DH
KV_W = N_KV * DH
CMP_LEN = 32
CMP_STRIDE = 16
SLC_BLOCK = 64
TOP_N = 16
WINDOW = 512
QB = 128
CHUNK = 128
GM_W = 1024
GM_GROUPS = 4
GM_DG = GM_W // GM_GROUPS
MEM_W = 1024
MEM_LEN = 256
MEM_HEADS = 4
MEM_DH = MEM_W // MEM_HEADS
LN_EPS = 1e-5
N_GATES = N_HEADS * 3

LANES = 128
KT = 512
BLK_PER_KT = KT // SLC_BLOCK
OH_PERIOD = LANES // BLK_PER_KT
NEG = -2.0 ** 100
VMEM_LIMIT = 48 * 1024 * 1024

F32 = jnp.float32
BF16 = jnp.bfloat16
_NT = (((1,), (1,)), ((), ()))


def _params(n_axes):
    return pltpu.CompilerParams(dimension_semantics=("arbitrary",) * n_axes,
                                vmem_limit_bytes=VMEM_LIMIT)


def _mm_kernel(a_ref, b_ref, o_ref):
    o_ref[...] = jnp.dot(a_ref[...], b_ref[...], preferred_element_type=F32).astype(o_ref.dtype)


def _matmul(a, b, out_dtype, tm, tn, name):
    m, k = a.shape
    n = b.shape[1]
    tm = min(tm, m)
    return pl.pallas_call(
        _mm_kernel,
        grid=(n // tn, m // tm),
        in_specs=[pl.BlockSpec((tm, k), lambda j, i: (i, 0)),
                  pl.BlockSpec((k, tn), lambda j, i: (0, j))],
        out_specs=pl.BlockSpec((tm, tn), lambda j, i: (i, j)),
        out_shape=jax.ShapeDtypeStruct((m, n), out_dtype),
        compiler_params=_params(2),
        name=name,
    )(a, b)


def _gelu_tanh(x):
    c = np.float32(np.sqrt(2.0 / np.pi))
    return x * (0.5 * (1.0 + jnp.tanh(c * (x + 0.044715 * (x * x * x)))))


def _compress_kernel(ch_ref, pe_lo_ref, pe_hi_ref, w1_ref, w2_ref, o_ref):
    ch = ch_ref[0, 0]
    nch = ch.shape[0]
    half = CMP_STRIDE * DH
    lo = (ch + pe_lo_ref[0]).astype(BF16)
    hi = (ch + pe_hi_ref[0]).astype(BF16)
    a = jnp.dot(lo, w1_ref[0, :half, :], preferred_element_type=F32)
    b = jnp.dot(hi, w1_ref[0, half:, :], preferred_element_type=F32)
    pre = a + pltpu.roll(b, nch - 1, axis=0)
    h = _gelu_tanh(pre).astype(BF16)
    o_ref[0, 0] = jnp.dot(h, w2_ref[0], preferred_element_type=F32).astype(o_ref.dtype)


def _compress(ch, pe_lo, pe_hi, w1, w2):
    _, g, nch, cw = ch.shape
    return pl.pallas_call(
        _compress_kernel,
        grid=(2, g),
        in_specs=[pl.BlockSpec((1, 1, nch, cw), lambda s, j: (s, j, 0, 0)),
                  pl.BlockSpec((1, 1, cw), lambda s, j: (s, 0, 0)),
                  pl.BlockSpec((1, 1, cw), lambda s, j: (s, 0, 0)),
                  pl.BlockSpec((1, 2 * cw, DH), lambda s, j: (s, 0, 0)),
                  pl.BlockSpec((1, DH, DH), lambda s, j: (s, 0, 0))],
        out_specs=pl.BlockSpec((1, 1, nch, DH), lambda s, j: (s, j, 0, 0)),
        out_shape=jax.ShapeDtypeStruct((2, g, nch, DH), BF16),
        compiler_params=_params(2),
        name="compress_kv",
    )(ch, pe_lo, pe_hi, w1, w2)


def _stack_heads(q):
    return jnp.concatenate([q[:, h * DH:(h + 1) * DH] for h in range(HG)], axis=0)


def _store_heads(o_ref, o):
    for h in range(HG):
        o_ref[:, h * DH:(h + 1) * DH] = o[h * QB:(h + 1) * QB, :]


def _cmp_kernel(slope_ref, q_ref, kc_ref, vc_ref, pool_ref, oc_ref, mb_ref):
    g = pl.program_id(0)
    i = pl.program_id(1)
    q0 = i * QB
    nch = kc_ref.shape[1]
    nsbp = pool_ref.shape[0]
    scale = np.float32(DH ** -0.5)

    qs = _stack_heads(q_ref[...]).astype(BF16)
    s = lax.dot_general(qs, kc_ref[0], _NT, preferred_element_type=F32)

    row = lax.broadcasted_iota(jnp.int32, (QB, nch), 0)
    col = lax.broadcasted_iota(jnp.int32, (QB, nch), 1)
    dist = (q0 + row) - (col * CMP_STRIDE + (CMP_LEN - 1))
    valid = dist >= 0
    distf = dist.astype(F32)

    imp = jnp.zeros((QB, nch), F32)
    for h in range(HG):
        sh = s[h * QB:(h + 1) * QB] * scale - slope_ref[g, h] * distf
        sh = jnp.where(valid, sh, NEG)
        m = jnp.max(sh, axis=1, keepdims=True)
        p = jnp.where(valid, jnp.exp(sh - m), 0.0)
        l = jnp.sum(p, axis=1, keepdims=True)
        pn = p / jnp.maximum(l, 1e-30)
        imp = imp + pn
        oc_ref[:, h * DH:(h + 1) * DH] = jnp.dot(pn.astype(BF16), vc_ref[0],
                                                 preferred_element_type=F32)

    pool = pool_ref[...]
    p1 = imp.astype(BF16)
    r1 = imp - p1.astype(F32)
    p2 = r1.astype(BF16)
    p3 = (r1 - p2.astype(F32)).astype(BF16)
    imps = (lax.dot_general(pool, p1, _NT, preferred_element_type=F32)
            + lax.dot_general(pool, p2, _NT, preferred_element_type=F32)
            + lax.dot_general(pool, p3, _NT, preferred_element_type=F32))

    j = lax.broadcasted_iota(jnp.int32, (nsbp, QB), 0)
    tq = q0 + lax.broadcasted_iota(jnp.int32, (nsbp, QB), 1)
    cur = lax.shift_right_arithmetic(tq, 6)
    forced = jnp.where(j == 0, 1, jnp.where(j == cur, 1, jnp.where(j == cur - 1, 1, 0)))
    sc = jnp.where(j > cur, -1.0, jnp.where(forced == 1, np.float32(3e38), imps))
    sel = jnp.zeros((nsbp, QB), F32)
    for _ in range(min(TOP_N, nsbp)):
        mx = jnp.max(sc, axis=0, keepdims=True)
        idx = jnp.min(jnp.where(sc == mx, j, nsbp), axis=0, keepdims=True)
        pick = j == idx
        sel = jnp.where(pick, 1.0, sel)
        sc = jnp.where(pick, -2.0, sc)
    bias = jnp.where(j <= cur, jnp.where(sel > 0.5, 0.0, NEG), NEG)
    mb_ref[...] = bias.T.astype(BF16)


def _cmp_attention(slopes, qf, kc, vc, pool):
    t = qf.shape[0]
    nch = kc.shape[1]
    nsbp = pool.shape[0]
    return pl.pallas_call(
        _cmp_kernel,
        grid=(N_KV, t // QB),
        in_specs=[pl.BlockSpec(memory_space=pltpu.SMEM),
                  pl.BlockSpec((QB, HG * DH), lambda g, i: (i, g)),
                  pl.BlockSpec((1, nch, DH), lambda g, i: (g, 0, 0)),
                  pl.BlockSpec((1, nch, DH), lambda g, i: (g, 0, 0)),
                  pl.BlockSpec((nsbp, nch), lambda g, i: (0, 0))],
        out_specs=[pl.BlockSpec((QB, HG * DH), lambda g, i: (i, g)),
                   pl.BlockSpec((QB, nsbp), lambda g, i: (i, g))],
        out_shape=[jax.ShapeDtypeStruct((t, NSA_W), F32),
                   jax.ShapeDtypeStruct((t, N_KV * nsbp), BF16)],
        compiler_params=_params(2),
        name="cmp_attn_select",
    )(slopes, qf, kc, vc, pool)


def _sel_kernel(slope_ref, q_ref, ks_ref, vs_ref, mb_ref, oh_ref, o_ref,
                arel_scr, m_scr, l_scr, acc_scr):
    g = pl.program_id(0)
    i = pl.program_id(1)
    q0 = i * QB
    scale = np.float32(DH ** -0.5)
    nhalf = mb_ref.shape[1] // LANES

    qs = _stack_heads(q_ref[...]).astype(BF16)
    rel = (lax.broadcasted_iota(jnp.int32, (QB, KT), 0)
           - lax.broadcasted_iota(jnp.int32, (QB, KT), 1))
    relf = rel.astype(F32)
    for h in range(HG):
        arel_scr[h * QB:(h + 1) * QB, :] = slope_ref[g, h] * relf
    m_scr[...] = jnp.full(m_scr.shape, NEG, F32)
    l_scr[...] = jnp.zeros(l_scr.shape, F32)
    acc_scr[...] = jnp.zeros(acc_scr.shape, F32)

    def tile(tn, causal):
        k0 = pl.multiple_of(tn * KT, KT)
        k = ks_ref[pl.ds(k0, KT), :]
        v = vs_ref[pl.ds(k0, KT), :]
        s = lax.dot_general(qs, k, _NT, preferred_element_type=F32)
        half = lax.shift_right_logical(tn, OH_PERIOD.bit_length() - 1)
        mbh = mb_ref[:, 0:LANES]
        for hf in range(1, nhalf):
            mbh = jnp.where(half == hf, mb_ref[:, hf * LANES:(hf + 1) * LANES], mbh)
        bias = jnp.dot(mbh, oh_ref[tn & (OH_PERIOD - 1)], preferred_element_type=F32)
        dq = (q0 - k0).astype(F32)
        if causal:
            bias = jnp.where(rel + (q0 - k0) >= 0, bias, NEG)
        ps, alphas = [], []
        for h in range(HG):
            rows = slice(h * QB, (h + 1) * QB)
            sh = s[rows] * scale - arel_scr[rows, :] - slope_ref[g, h] * dq + bias
            m_prev = m_scr[rows, :]
            m_next = jnp.maximum(m_prev, jnp.max(sh, axis=1, keepdims=True))
            alpha = jnp.exp(m_prev - m_next)
            p = jnp.exp(sh - m_next)
            l_scr[rows, :] = alpha * l_scr[rows, :] + jnp.sum(p, axis=1, keepdims=True)
            m_scr[rows, :] = m_next
            ps.append(p.astype(BF16))
            alphas.append(alpha)
        pv = jnp.dot(jnp.concatenate(ps, axis=0), v, preferred_element_type=F32)
        acc_scr[...] = jnp.concatenate(alphas, axis=0) * acc_scr[...] + pv

    n_tiles = lax.shift_right_logical(q0 + QB - 1, KT.bit_length() - 1) + 1

    def body(tn, carry):
        tile(tn, False)
        return carry

    lax.fori_loop(0, n_tiles - 1, body, 0)
    tile(n_tiles - 1, True)
    _store_heads(o_ref, acc_scr[...] / l_scr[...])


def _sel_attention(slopes, qf, projb, mb, oh):
    t = qf.shape[0]
    nsbp = mb.shape[1] // N_KV
    return pl.pallas_call(
        _sel_kernel,
        grid=(N_KV, t // QB),
        in_specs=[pl.BlockSpec(memory_space=pltpu.SMEM),
                  pl.BlockSpec((QB, HG * DH), lambda g, i: (i, g)),
                  pl.BlockSpec((t, DH), lambda g, i: (0, g)),
                  pl.BlockSpec((t, DH), lambda g, i: (0, N_KV + g)),
                  pl.BlockSpec((QB, nsbp), lambda g, i: (i, g)),
                  pl.BlockSpec((OH_PERIOD, LANES, KT), lambda g, i: (0, 0, 0))],
        out_specs=pl.BlockSpec((QB, HG * DH), lambda g, i: (i, g)),
        out_shape=jax.ShapeDtypeStruct((t, NSA_W), F32),
        scratch_shapes=[pltpu.VMEM((HG * QB, KT), F32),
                        pltpu.VMEM((HG * QB, 1), F32),
                        pltpu.VMEM((HG * QB, 1), F32),
                        pltpu.VMEM((HG * QB, DH), F32)],
        compiler_params=_params(2),
        name="sel_attn",
    )(slopes, qf, projb, projb, mb, oh)


WSPAN = WINDOW + QB


def _win_kernel(slope_ref, q_ref, kw_ref, vw_ref, o_ref):
    g = pl.program_id(0)
    i = pl.program_id(1)
    q0 = i * QB
    scale = np.float32(DH ** -0.5)
    start = pl.multiple_of(jnp.maximum(q0 - WINDOW, 0), QB)

    qs = _stack_heads(q_ref[...]).astype(BF16)
    k = kw_ref[pl.ds(start, WSPAN), :]
    v = vw_ref[pl.ds(start, WSPAN), :]
    s = lax.dot_general(qs, k, _NT, preferred_element_type=F32)
    dist = ((q0 - start) + lax.broadcasted_iota(jnp.int32, (QB, WSPAN), 0)
            - lax.broadcasted_iota(jnp.int32, (QB, WSPAN), 1))
    valid = jnp.where(dist >= 0, jnp.where(dist < WINDOW, 1, 0), 0) == 1
    distf = dist.astype(F32)
    ps = []
    for h in range(HG):
        sh = s[h * QB:(h + 1) * QB] * scale - slope_ref[g, h] * distf
        sh = jnp.where(valid, sh, NEG)
        m = jnp.max(sh, axis=1, keepdims=True)
        p = jnp.where(valid, jnp.exp(sh - m), 0.0)
        l = jnp.sum(p, axis=1, keepdims=True)
        ps.append((p / jnp.maximum(l, 1e-30)).astype(BF16))
    o = jnp.dot(jnp.concatenate(ps, axis=0), v, preferred_element_type=F32)
    _store_heads(o_ref, o)


def _win_attention(slopes, qf, projb):
    t = qf.shape[0]
    return pl.pallas_call(
        _win_kernel,
        grid=(N_KV, t // QB),
        in_specs=[pl.BlockSpec(memory_space=pltpu.SMEM),
                  pl.BlockSpec((QB, HG * DH), lambda g, i: (i, g)),
                  pl.BlockSpec((t, DH), lambda g, i: (0, 2 * N_KV + g)),
                  pl.BlockSpec((t, DH), lambda g, i: (0, 3 * N_KV + g))],
        out_specs=pl.BlockSpec((QB, HG * DH), lambda g, i: (i, g)),
        out_shape=jax.ShapeDtypeStruct((t, NSA_W), F32),
        compiler_params=_params(2),
        name="win_attn",
    )(slopes, qf, projb, projb)


def _silu(z):
    return z * jax.nn.sigmoid(z)


def _mix_kernel(oc_ref, os_ref, ow_ref, gates_ref, zn_ref, u_ref, v_ref, zg_ref, zm_ref,
                qm_ref, kvm_ref, lng_ref, lnb_ref, wsp_ref, bsp_ref, y_ref):
    sig = jax.nn.sigmoid(gates_ref[...])
    for h in range(N_HEADS):
        sl = slice(h * DH, (h + 1) * DH)
        comb = (sig[:, 3 * h:3 * h + 1] * oc_ref[:, sl]
                + sig[:, 3 * h + 1:3 * h + 2] * os_ref[:, sl]
                + sig[:, 3 * h + 2:3 * h + 3] * ow_ref[:, sl])
        y_ref[:, sl] = (comb * _silu(zn_ref[:, sl])).astype(y_ref.dtype)

    vv = v_ref[...]
    mu = jnp.mean(vv, axis=-1, keepdims=True)
    var = jnp.mean(jnp.square(vv - mu), axis=-1, keepdims=True)
    vn = ((vv - mu) * lax.rsqrt(var + LN_EPS) * lng_ref[...] + lnb_ref[...]).astype(BF16)
    causal = (lax.broadcasted_iota(jnp.int32, (CHUNK, CHUNK), 0)
              >= lax.broadcasted_iota(jnp.int32, (CHUNK, CHUNK), 1))
    for gg in range(GM_GROUPS):
        sl = slice(gg * GM_DG, (gg + 1) * GM_DG)
        ws = jnp.where(causal, wsp_ref[gg], 0.0).astype(BF16)
        sv = jnp.dot(ws, vn[:, sl], preferred_element_type=F32) + bsp_ref[:, gg:gg + 1]
        y_ref[:, NSA_W + gg * GM_DG:NSA_W + (gg + 1) * GM_DG] = (
            u_ref[:, sl] * sv * _silu(zg_ref[:, sl])).astype(y_ref.dtype)

    mscale = np.float32(MEM_DH ** -0.5)
    for h in range(MEM_HEADS):
        sl = slice(h * MEM_DH, (h + 1) * MEM_DH)
        km = kvm_ref[:, h * MEM_DH:(h + 1) * MEM_DH]
        vm = kvm_ref[:, MEM_W + h * MEM_DH:MEM_W + (h + 1) * MEM_DH]
        s = lax.dot_general(qm_ref[:, sl], km, _NT, preferred_element_type=F32) * mscale
        m = jnp.max(s, axis=-1, keepdims=True)
        e = jnp.exp(s - m)
        p = e / jnp.sum(e, axis=-1, keepdims=True)
        o = jnp.dot(p.astype(BF16), vm, preferred_element_type=F32)
        y_ref[:, NSA_W + GM_W + h * MEM_DH:NSA_W + GM_W + (h + 1) * MEM_DH] = (
            o * _silu(zm_ref[:, sl])).astype(y_ref.dtype)


def _mix(oc, osel, ow, gates, proja, projb, kvm, lng, lnb, wsp, bsp_t):
    t = oc.shape[0]
    row = lambda c: (lambda i: (i, c))
    full2 = lambda i: (0, 0)
    return pl.pallas_call(
        _mix_kernel,
        grid=(t // QB,),
        in_specs=[pl.BlockSpec((QB, NSA_W), row(0)),
                  pl.BlockSpec((QB, NSA_W), row(0)),
                  pl.BlockSpec((QB, NSA_W), row(0)),
                  pl.BlockSpec((QB, LANES), row(0)),
                  pl.BlockSpec((QB, NSA_W), row(1)),
                  pl.BlockSpec((QB, GM_W), row(4)),
                  pl.BlockSpec((QB, GM_W), row(5)),
                  pl.BlockSpec((QB, GM_W), row(6)),
                  pl.BlockSpec((QB, MEM_W), row(7)),
                  pl.BlockSpec((QB, MEM_W), row(2)),
                  pl.BlockSpec((MEM_LEN, 2 * MEM_W), full2),
                  pl.BlockSpec((1, GM_W), full2),
                  pl.BlockSpec((1, GM_W), full2),
                  pl.BlockSpec((GM_GROUPS, CHUNK, CHUNK), lambda i: (0, 0, 0)),
                  pl.BlockSpec((CHUNK, GM_GROUPS), full2)],
        out_specs=pl.BlockSpec((QB, D_MODEL), row(0)),
        out_shape=jax.ShapeDtypeStruct((t, D_MODEL), BF16),
        compiler_params=_params(1),
        name="mix_heads",
    )(oc, osel, ow, gates, proja, proja, proja, proja, proja, projb, kvm, lng, lnb, wsp, bsp_t)


def _ln_kernel(alpha, x_ref, h_ref, g_ref, b_ref, o_ref):
    z = alpha * x_ref[...] + h_ref[...]
    mu = jnp.mean(z, axis=-1, keepdims=True)
    var = jnp.mean(jnp.square(z - mu), axis=-1, keepdims=True)
    o_ref[...] = (z - mu) * lax.rsqrt(var + LN_EPS) * g_ref[...] + b_ref[...]


def _residual_ln(x2, h, g, b, alpha, tm=256):
    t, d = x2.shape
    tm = min(tm, t)
    return pl.pallas_call(
        functools.partial(_ln_kernel, np.float32(alpha)),
        grid=(t // tm,),
        in_specs=[pl.BlockSpec((tm, d), lambda i: (i, 0)),
                  pl.BlockSpec((tm, d), lambda i: (i, 0)),
                  pl.BlockSpec((1, d), lambda i: (0, 0)),
                  pl.BlockSpec((1, d), lambda i: (0, 0))],
        out_specs=pl.BlockSpec((tm, d), lambda i: (i, 0)),
        out_shape=jax.ShapeDtypeStruct((t, d), F32),
        compiler_params=_params(1),
        name="residual_ln",
    )(x2, h, g, b)


def _pool_matrix(nsbp, nch):
    j = np.arange(nsbp)[:, None]
    c = np.arange(nch)[None, :]
    ratio = SLC_BLOCK // CMP_STRIDE
    return ((c >= ratio * j - 1) & (c <= ratio * j + ratio - 1)).astype(np.float32)


def _block_onehot():
    p = np.arange(OH_PERIOD)[:, None, None]
    c = np.arange(LANES)[None, :, None]
    l = np.arange(KT)[None, None, :]
    return (c == (BLK_PER_KT * p + l // SLC_BLOCK) % LANES).astype(np.float32)


def kernel(x, mem, w_in, w_cmp_k1, w_cmp_k2, w_cmp_v1, w_cmp_v2, pe_cmp_k, pe_cmp_v,
           gm_ln_g, gm_ln_b, w_spatial, b_spatial, w_mem_kv, w_out, ln_g, ln_b):
    b, t, _ = x.shape
    assert b == 1 and t % KT == 0 and t >= WSPAN
    alpha = 2.0 ** 0.25
    nch = t // CMP_STRIDE
    nsb = t // SLC_BLOCK
    nsbp = -(-nsb // LANES) * LANES

    sizes = (NSA_W, KV_W, KV_W, KV_W, KV_W, KV_W, KV_W, N_GATES, NSA_W, GM_W, GM_W, GM_W, MEM_W, MEM_W)
    offs = np.concatenate([[0], np.cumsum(sizes)])
    seg = lambda n: w_in[:, int(offs[n]):int(offs[n + 1])]
    w_a = jnp.concatenate([seg(0), seg(8), seg(9), seg(10), seg(11), seg(13), seg(1), seg(2)],
                          axis=1).astype(BF16)
    w_b = jnp.concatenate([seg(3), seg(4), seg(5), seg(6), seg(12)], axis=1).astype(BF16)
    w_g = jnp.pad(seg(7), ((0, 0), (0, LANES - N_GATES))).astype(BF16)

    x2 = x[0]
    xb = x2.astype(BF16)
    proja = _matmul(xb, w_a, F32, 512, 512, "proj_f32")
    projb = _matmul(xb, w_b, BF16, 512, 512, "proj_bf16")
    gates = _matmul(xb, w_g, F32, 512, LANES, "proj_gates")
    kvm = _matmul(mem[0].astype(BF16), w_mem_kv.astype(BF16), BF16, MEM_LEN, 512, "mem_kv")

    kcvc = proja[:, 8192:8192 + 2 * KV_W].reshape(nch, CMP_STRIDE, 2, N_KV, DH)
    ch = kcvc.transpose(2, 3, 0, 1, 4).reshape(2, N_KV, nch, CMP_STRIDE * DH)
    pe = jnp.stack([pe_cmp_k, pe_cmp_v])
    pe_lo = pe[:, :CMP_STRIDE].reshape(2, 1, CMP_STRIDE * DH)
    pe_hi = pe[:, CMP_STRIDE:].reshape(2, 1, CMP_STRIDE * DH)
    w1 = jnp.stack([w_cmp_k1, w_cmp_v1]).astype(BF16)
    w2 = jnp.stack([w_cmp_k2, w_cmp_v2]).astype(BF16)
    kvc = _compress(ch, pe_lo, pe_hi, w1, w2)

    slopes = jnp.exp2(-8.0 * jnp.arange(1, N_HEADS + 1, dtype=F32) / N_HEADS).reshape(N_KV, HG)
    pool = jnp.asarray(_pool_matrix(nsbp, nch), BF16)
    oh = jnp.asarray(_block_onehot(), BF16)

    o_c, mb = _cmp_attention(slopes, proja, kvc[0], kvc[1], pool)
    o_s = _sel_attention(slopes, proja, projb, mb, oh)
    o_w = _win_attention(slopes, proja, projb)

    y = _mix(o_c, o_s, o_w, gates, proja, projb, kvm,
             gm_ln_g.reshape(1, GM_W), gm_ln_b.reshape(1, GM_W), w_spatial, b_spatial.T)
    h = _matmul(y, w_out.astype(BF16), F32, 512, 1024, "out_proj")
    out = _residual_ln(x2, h, ln_g.reshape(1, D_MODEL), ln_b.reshape(1, D_MODEL), alpha)
    return out[None]
```

```python
import functools

import numpy as np
import jax
import jax.numpy as jnp
from jax import lax
from jax.experimental import pallas as pl
from jax.experimental.pallas import tpu as pltpu

D_MODEL = 4096
DH = 128
N_HEADS = 16
HG = 4
N_KV = 4
NSA_W = N_HEADS * DH
KV_W = N_KV * DH
CMP_LEN = 32
CMP_STRIDE = 16
SLC_BLOCK = 64
TOP_N = 16
WINDOW = 512
QB = 128
CHUNK = 128
GM_W = 1024
GM_GROUPS = 4
GM_DG = GM_W // GM_GROUPS
MEM_W = 1024
MEM_LEN = 256
MEM_HEADS = 4
MEM_DH = MEM_W // MEM_HEADS
LN_EPS = 1e-5
N_GATES = N_HEADS * 3

LANES = 128
KT = 512
BLK_PER_KT = KT // SLC_BLOCK
OH_PERIOD = LANES // BLK_PER_KT
NEG = -2.0 ** 100
VMEM_LIMIT = 48 * 1024 * 1024

F32 = jnp.float32
BF16 = jnp.bfloat16
_NT = (((1,), (1,)), ((), ()))


def _params(n_axes):
    return pltpu.CompilerParams(dimension_semantics=("arbitrary",) * n_axes,
                                vmem_limit_bytes=VMEM_LIMIT)


def _mm_kernel(a_ref, b_ref, o_ref):
    o_ref[...] = jnp.dot(a_ref[...], b_ref[...], preferred_element_type=F32).astype(o_ref.dtype)


def _matmul(a, b, out_dtype, tm, tn, name):
    m, k = a.shape
    n = b.shape[1]
    tm = min(tm, m)
    return pl.pallas_call(
        _mm_kernel,
        grid=(n // tn, m // tm),
        in_specs=[pl.BlockSpec((tm, k), lambda j, i: (i, 0)),
                  pl.BlockSpec((k, tn), lambda j, i: (0, j))],
        out_specs=pl.BlockSpec((tm, tn), lambda j, i: (i, j)),
        out_shape=jax.ShapeDtypeStruct((m, n), out_dtype),
        compiler_params=_params(2),
        name=name,
    )(a, b)


def _gelu_tanh(x):
    c = np.float32(np.sqrt(2.0 / np.pi))
    return x * (0.5 * (1.0 + jnp.tanh(c * (x + 0.044715 * (x * x * x)))))


def _compress_kernel(ch_ref, pe_lo_ref, pe_hi_ref, w1_ref, w2_ref, o_ref):
    ch = ch_ref[0, 0]
    nch = ch.shape[0]
    half = CMP_STRIDE * DH
    lo = (ch + pe_lo_ref[0]).astype(BF16)
    hi = (ch + pe_hi_ref[0]).astype(BF16)
    a = jnp.dot(lo, w1_ref[0, :half, :], preferred_element_type=F32)
    b = jnp.dot(hi, w1_ref[0, half:, :], preferred_element_type=F32)
    pre = a + pltpu.roll(b, nch - 1, axis=0)
    h = _gelu_tanh(pre).astype(BF16)
    o_ref[0, 0] = jnp.dot(h, w2_ref[0], preferred_element_type=F32).astype(o_ref.dtype)


def _compress(ch, pe_lo, pe_hi, w1, w2):
    _, g, nch, cw = ch.shape
    return pl.pallas_call(
        _compress_kernel,
        grid=(2, g),
        in_specs=[pl.BlockSpec((1, 1, nch, cw), lambda s, j: (s, j, 0, 0)),
                  pl.BlockSpec((1, 1, cw), lambda s, j: (s, 0, 0)),
                  pl.BlockSpec((1, 1, cw), lambda s, j: (s, 0, 0)),
                  pl.BlockSpec((1, 2 * cw, DH), lambda s, j: (s, 0, 0)),
                  pl.BlockSpec((1, DH, DH), lambda s, j: (s, 0, 0))],
        out_specs=pl.BlockSpec((1, 1, nch, DH), lambda s, j: (s, j, 0, 0)),
        out_shape=jax.ShapeDtypeStruct((2, g, nch, DH), BF16),
        compiler_params=_params(2),
        name="compress_kv",
    )(ch, pe_lo, pe_hi, w1, w2)


def _stack_heads(q):
    return jnp.concatenate([q[:, h * DH:(h + 1) * DH] for h in range(HG)], axis=0)


def _store_heads(o_ref, o):
    for h in range(HG):
        o_ref[:, h * DH:(h + 1) * DH] = o[h * QB:(h + 1) * QB, :]


def _cmp_kernel(slope_ref, q_ref, kc_ref, vc_ref, pool_ref, tmap_ref, oc_ref, mb_ref, tf_ref):
    g = pl.program_id(0)
    i = pl.program_id(1)
    q0 = i * QB
    nch = kc_ref.shape[1]
    nsbp = pool_ref.shape[0]
    scale = np.float32(DH ** -0.5)

    qs = _stack_heads(q_ref[...]).astype(BF16)
    s = lax.dot_general(qs, kc_ref[0], _NT, preferred_element_type=F32)

    row = lax.broadcasted_iota(jnp.int32, (QB, nch), 0)
    col = lax.broadcasted_iota(jnp.int32, (QB, nch), 1)
    dist = (q0 + row) - (col * CMP_STRIDE + (CMP_LEN - 1))
    valid = dist >= 0
    distf = dist.astype(F32)

    imp = jnp.zeros((QB, nch), F32)
    for h in range(HG):
        sh = s[h * QB:(h + 1) * QB] * scale - slope_ref[g, h] * distf
        sh = jnp.where(valid, sh, NEG)
        m = jnp.max(sh, axis=1, keepdims=True)
        p = jnp.where(valid, jnp.exp(sh - m), 0.0)
        l = jnp.sum(p, axis=1, keepdims=True)
        pn = p / jnp.maximum(l, 1e-30)
        imp = imp + pn
        oc_ref[:, h * DH:(h + 1) * DH] = jnp.dot(pn.astype(BF16), vc_ref[0],
                                                 preferred_element_type=F32)

    pool = pool_ref[...]
    p1 = imp.astype(BF16)
    r1 = imp - p1.astype(F32)
    p2 = r1.astype(BF16)
    p3 = (r1 - p2.astype(F32)).astype(BF16)
    imps = (lax.dot_general(pool, p1, _NT, preferred_element_type=F32)
            + lax.dot_general(pool, p2, _NT, preferred_element_type=F32)
            + lax.dot_general(pool, p3, _NT, preferred_element_type=F32))

    j = lax.broadcasted_iota(jnp.int32, (nsbp, QB), 0)
    tq = q0 + lax.broadcasted_iota(jnp.int32, (nsbp, QB), 1)
    cur = lax.shift_right_arithmetic(tq, 6)
    forced = jnp.where(j == 0, 1, jnp.where(j == cur, 1, jnp.where(j == cur - 1, 1, 0)))
    sc = jnp.where(j > cur, -1.0, jnp.where(forced == 1, np.float32(3e38), imps))
    sel = jnp.zeros((nsbp, QB), F32)
    for _ in range(min(TOP_N, nsbp)):
        mx = jnp.max(sc, axis=0, keepdims=True)
        idx = jnp.min(jnp.where(sc == mx, j, nsbp), axis=0, keepdims=True)
        pick = j == idx
        sel = jnp.where(pick, 1.0, sel)
        sc = jnp.where(pick, -2.0, sc)
    bias = jnp.where(j <= cur, jnp.where(sel > 0.5, 0.0, NEG), NEG)
    bias_t = bias.T
    mb_ref[...] = bias_t.astype(BF16)
    anyq = jnp.max(jnp.where(bias_t == 0.0, 1.0, 0.0), axis=0, keepdims=True)
    hit = jnp.dot(jnp.broadcast_to(anyq, (8, nsbp)).astype(BF16), tmap_ref[...],
                  preferred_element_type=F32)
    tf_ref[0, 0] = jnp.where(hit > 0.0, 1, 0).astype(jnp.int32)


def _cmp_attention(slopes, qf, kc, vc, pool, tmap):
    t = qf.shape[0]
    nch = kc.shape[1]
    nsbp = pool.shape[0]
    return pl.pallas_call(
        _cmp_kernel,
        grid=(N_KV, t // QB),
        in_specs=[pl.BlockSpec(memory_space=pltpu.SMEM),
                  pl.BlockSpec((QB, HG * DH), lambda g, i: (i, g)),
                  pl.BlockSpec((1, nch, DH), lambda g, i: (g, 0, 0)),
                  pl.BlockSpec((1, nch, DH), lambda g, i: (g, 0, 0)),
                  pl.BlockSpec((nsbp, nch), lambda g, i: (0, 0)),
                  pl.BlockSpec((nsbp, LANES), lambda g, i: (0, 0))],
        out_specs=[pl.BlockSpec((QB, HG * DH), lambda g, i: (i, g)),
                   pl.BlockSpec((QB, nsbp), lambda g, i: (i, g)),
                   pl.BlockSpec((1, 1, 8, LANES), lambda g, i: (g, i, 0, 0))],
        out_shape=[jax.ShapeDtypeStruct((t, NSA_W), F32),
                   jax.ShapeDtypeStruct((t, N_KV * nsbp), BF16),
                   jax.ShapeDtypeStruct((N_KV, t // QB, 8, LANES), jnp.int32)],
        compiler_params=_params(2),
        name="cmp_attn_select",
    )(slopes, qf, kc, vc, pool, tmap)


def _sel_kernel(tf_ref, slope_ref, q_ref, ks_ref, vs_ref, mb_ref, oh_ref, o_ref,
                arel_scr, m_scr, l_scr, acc_scr):
    g = pl.program_id(0)
    i = pl.program_id(1)
    q0 = i * QB
    scale = np.float32(DH ** -0.5)
    nhalf = mb_ref.shape[1] // LANES

    qs = _stack_heads(q_ref[...]).astype(BF16)
    rel = (lax.broadcasted_iota(jnp.int32, (QB, KT), 0)
           - lax.broadcasted_iota(jnp.int32, (QB, KT), 1))
    relf = rel.astype(F32)
    for h in range(HG):
        arel_scr[h * QB:(h + 1) * QB, :] = slope_ref[g, h] * relf
    m_scr[...] = jnp.full(m_scr.shape, NEG, F32)
    l_scr[...] = jnp.zeros(l_scr.shape, F32)
    acc_scr[...] = jnp.zeros(acc_scr.shape, F32)

    def tile(tn, causal):
        k0 = pl.multiple_of(tn * KT, KT)
        k = ks_ref[pl.ds(k0, KT), :]
        v = vs_ref[pl.ds(k0, KT), :]
        s = lax.dot_general(qs, k, _NT, preferred_element_type=F32)
        half = lax.shift_right_logical(tn, OH_PERIOD.bit_length() - 1)
        mbh = mb_ref[:, 0:LANES]
        for hf in range(1, nhalf):
            mbh = jnp.where(half == hf, mb_ref[:, hf * LANES:(hf + 1) * LANES], mbh)
        bias = jnp.dot(mbh, oh_ref[tn & (OH_PERIOD - 1)], preferred_element_type=F32)
        dq = (q0 - k0).astype(F32)
        if causal:
            bias = jnp.where(rel + (q0 - k0) >= 0, bias, NEG)
        ps, alphas = [], []
        for h in range(HG):
            rows = slice(h * QB, (h + 1) * QB)
            sh = s[rows] * scale - arel_scr[rows, :] - slope_ref[g, h] * dq + bias
            m_prev = m_scr[rows, :]
            m_next = jnp.maximum(m_prev, jnp.max(sh, axis=1, keepdims=True))
            alpha = jnp.exp(m_prev - m_next)
            p = jnp.exp(sh - m_next)
            l_scr[rows, :] = alpha * l_scr[rows, :] + jnp.sum(p, axis=1, keepdims=True)
            m_scr[rows, :] = m_next
            ps.append(p.astype(BF16))
            alphas.append(alpha)
        pv = jnp.dot(jnp.concatenate(ps, axis=0), v, preferred_element_type=F32)
        acc_scr[...] = jnp.concatenate(alphas, axis=0) * acc_scr[...] + pv

    n_tiles = lax.shift_right_logical(q0 + QB - 1, KT.bit_length() - 1) + 1

    def body(tn, carry):
        @pl.when(tf_ref[g, i, tn] != 0)
        def _():
            tile(tn, False)
        return carry

    lax.fori_loop(0, n_tiles - 1, body, 0)
    tile(n_tiles - 1, True)
    _store_heads(o_ref, acc_scr[...] / l_scr[...])


def _sel_attention(tflags, slopes, qf, projb, mb, oh):
    t = qf.shape[0]
    nsbp = mb.shape[1] // N_KV
    grid_spec = pltpu.PrefetchScalarGridSpec(
        num_scalar_prefetch=1,
        grid=(N_KV, t // QB),
        in_specs=[pl.BlockSpec(memory_space=pltpu.SMEM),
                  pl.BlockSpec((QB, HG * DH), lambda g, i, tf: (i, g)),
                  pl.BlockSpec((t, DH), lambda g, i, tf: (0, g)),
                  pl.BlockSpec((t, DH), lambda g, i, tf: (0, N_KV + g)),
                  pl.BlockSpec((QB, nsbp), lambda g, i, tf: (i, g)),
                  pl.BlockSpec((OH_PERIOD, LANES, KT), lambda g, i, tf: (0, 0, 0))],
        out_specs=pl.BlockSpec((QB, HG * DH), lambda g, i, tf: (i, g)),
        scratch_shapes=[pltpu.VMEM((HG * QB, KT), F32),
                        pltpu.VMEM((HG * QB, 1), F32),
                        pltpu.VMEM((HG * QB, 1), F32),
                        pltpu.VMEM((HG * QB, DH), F32)])
    return pl.pallas_call(
        _sel_kernel,
        grid_spec=grid_spec,
        out_shape=jax.ShapeDtypeStruct((t, NSA_W), F32),
        compiler_params=_params(2),
        name="sel_attn",
    )(tflags, slopes, qf, projb, projb, mb, oh)


WSPAN = WINDOW + QB


def _win_kernel(slope_ref, q_ref, kw_ref, vw_ref, o_ref):
    g = pl.program_id(0)
    i = pl.program_id(1)
    q0 = i * QB
    scale = np.float32(DH ** -0.5)
    start = pl.multiple_of(jnp.maximum(q0 - WINDOW, 0), QB)

    qs = _stack_heads(q_ref[...]).astype(BF16)
    k = kw_ref[pl.ds(start, WSPAN), :]
    v = vw_ref[pl.ds(start, WSPAN), :]
    s = lax.dot_general(qs, k, _NT, preferred_element_type=F32)
    dist = ((q0 - start) + lax.broadcasted_iota(jnp.int32, (QB, WSPAN), 0)
            - lax.broadcasted_iota(jnp.int32, (QB, WSPAN), 1))
    valid = jnp.where(dist >= 0, jnp.where(dist < WINDOW, 1, 0), 0) == 1
    distf = dist.astype(F32)
    ps = []
    for h in range(HG):
        sh = s[h * QB:(h + 1) * QB] * scale - slope_ref[g, h] * distf
        sh = jnp.where(valid, sh, NEG)
        m = jnp.max(sh, axis=1, keepdims=True)
        p = jnp.where(valid, jnp.exp(sh - m), 0.0)
        l = jnp.sum(p, axis=1, keepdims=True)
        ps.append((p / jnp.maximum(l, 1e-30)).astype(BF16))
    o = jnp.dot(jnp.concatenate(ps, axis=0), v, preferred_element_type=F32)
    _store_heads(o_ref, o)


def _win_attention(slopes, qf, projb):
    t = qf.shape[0]
    return pl.pallas_call(
        _win_kernel,
        grid=(N_KV, t // QB),
        in_specs=[pl.BlockSpec(memory_space=pltpu.SMEM),
                  pl.BlockSpec((QB, HG * DH), lambda g, i: (i, g)),
                  pl.BlockSpec((t, DH), lambda g, i: (0, 2 * N_KV + g)),
                  pl.BlockSpec((t, DH), lambda g, i: (0, 3 * N_KV + g))],
        out_specs=pl.BlockSpec((QB, HG * DH), lambda g, i: (i, g)),
        out_shape=jax.ShapeDtypeStruct((t, NSA_W), F32),
        compiler_params=_params(2),
        name="win_attn",
    )(slopes, qf, projb, projb)


def _silu(z):
    return z * jax.nn.sigmoid(z)


def _mix_kernel(oc_ref, os_ref, ow_ref, gates_ref, zn_ref, u_ref, v_ref, zg_ref, zm_ref,
                qm_ref, kvm_ref, lng_ref, lnb_ref, wsp_ref, bsp_ref, y_ref):
    sig = jax.nn.sigmoid(gates_ref[...])
    for h in range(N_HEADS):
        sl = slice(h * DH, (h + 1) * DH)
        comb = (sig[:, 3 * h:3 * h + 1] * oc_ref[:, sl]
                + sig[:, 3 * h + 1:3 * h + 2] * os_ref[:, sl]
                + sig[:, 3 * h + 2:3 * h + 3] * ow_ref[:, sl])
        y_ref[:, sl] = (comb * _silu(zn_ref[:, sl])).astype(y_ref.dtype)

    vv = v_ref[...]
    mu = jnp.mean(vv, axis=-1, keepdims=True)
    var = jnp.mean(jnp.square(vv - mu), axis=-1, keepdims=True)
    vn = ((vv - mu) * lax.rsqrt(var + LN_EPS) * lng_ref[...] + lnb_ref[...]).astype(BF16)
    causal = (lax.broadcasted_iota(jnp.int32, (CHUNK, CHUNK), 0)
              >= lax.broadcasted_iota(jnp.int32, (CHUNK, CHUNK), 1))
    for gg in range(GM_GROUPS):
        sl = slice(gg * GM_DG, (gg + 1) * GM_DG)
        ws = jnp.where(causal, wsp_ref[gg], 0.0).astype(BF16)
        sv = jnp.dot(ws, vn[:, sl], preferred_element_type=F32) + bsp_ref[:, gg:gg + 1]
        y_ref[:, NSA_W + gg * GM_DG:NSA_W + (gg + 1) * GM_DG] = (
            u_ref[:, sl] * sv * _silu(zg_ref[:, sl])).astype(y_ref.dtype)

    mscale = np.float32(MEM_DH ** -0.5)
    for h in range(MEM_HEADS):
        sl = slice(h * MEM_DH, (h + 1) * MEM_DH)
        km = kvm_ref[:, h * MEM_DH:(h + 1) * MEM_DH]
        vm = kvm_ref[:, MEM_W + h * MEM_DH:MEM_W + (h + 1) * MEM_DH]
        s = lax.dot_general(qm_ref[:, sl], km, _NT, preferred_element_type=F32) * mscale
        m = jnp.max(s, axis=-1, keepdims=True)
        e = jnp.exp(s - m)
        p = e / jnp.sum(e, axis=-1, keepdims=True)
        o = jnp.dot(p.astype(BF16), vm, preferred_element_type=F32)
        y_ref[:, NSA_W + GM_W + h * MEM_DH:NSA_W + GM_W + (h + 1) * MEM_DH] = (
            o * _silu(zm_ref[:, sl])).astype(y_ref.dtype)


def _mix(oc, osel, ow, gates, proja, projb, kvm, lng, lnb, wsp, bsp_t):
    t = oc.shape[0]
    row = lambda c: (lambda i: (i, c))
    full2 = lambda i: (0, 0)
    return pl.pallas_call(
        _mix_kernel,
        grid=(t // QB,),
        in_specs=[pl.BlockSpec((QB, NSA_W), row(0)),
                  pl.BlockSpec((QB, NSA_W), row(0)),
                  pl.BlockSpec((QB, NSA_W), row(0)),
                  pl.BlockSpec((QB, LANES), row(0)),
                  pl.BlockSpec((QB, NSA_W), row(1)),
                  pl.BlockSpec((QB, GM_W), row(4)),
                  pl.BlockSpec((QB, GM_W), row(5)),
                  pl.BlockSpec((QB, GM_W), row(6)),
                  pl.BlockSpec((QB, MEM_W), row(7)),
                  pl.BlockSpec((QB, MEM_W), row(2)),
                  pl.BlockSpec((MEM_LEN, 2 * MEM_W), full2),
                  pl.BlockSpec((1, GM_W), full2),
                  pl.BlockSpec((1, GM_W), full2),
                  pl.BlockSpec((GM_GROUPS, CHUNK, CHUNK), lambda i: (0, 0, 0)),
                  pl.BlockSpec((CHUNK, GM_GROUPS), full2)],
        out_specs=pl.BlockSpec((QB, D_MODEL), row(0)),
        out_shape=jax.ShapeDtypeStruct((t, D_MODEL), BF16),
        compiler_params=_params(1),
        name="mix_heads",
    )(oc, osel, ow, gates, proja, proja, proja, proja, proja, projb, kvm, lng, lnb, wsp, bsp_t)


def _ln_kernel(alpha, x_ref, h_ref, g_ref, b_ref, o_ref):
    z = alpha * x_ref[...] + h_ref[...]
    mu = jnp.mean(z, axis=-1, keepdims=True)
    var = jnp.mean(jnp.square(z - mu), axis=-1, keepdims=True)
    o_ref[...] = (z - mu) * lax.rsqrt(var + LN_EPS) * g_ref[...] + b_ref[...]


def _residual_ln(x2, h, g, b, alpha, tm=256):
    t, d = x2.shape
    tm = min(tm, t)
    return pl.pallas_call(
        functools.partial(_ln_kernel, np.float32(alpha)),
        grid=(t // tm,),
        in_specs=[pl.BlockSpec((tm, d), lambda i: (i, 0)),
                  pl.BlockSpec((tm, d), lambda i: (i, 0)),
                  pl.BlockSpec((1, d), lambda i: (0, 0)),
                  pl.BlockSpec((1, d), lambda i: (0, 0))],
        out_specs=pl.BlockSpec((tm, d), lambda i: (i, 0)),
        out_shape=jax.ShapeDtypeStruct((t, d), F32),
        compiler_params=_params(1),
        name="residual_ln",
    )(x2, h, g, b)


def _pool_matrix(nsbp, nch):
    j = np.arange(nsbp)[:, None]
    c = np.arange(nch)[None, :]
    ratio = SLC_BLOCK // CMP_STRIDE
    return ((c >= ratio * j - 1) & (c <= ratio * j + ratio - 1)).astype(np.float32)


def _tile_map(nsbp):
    j = np.arange(nsbp)[:, None]
    n = np.arange(LANES)[None, :]
    return (j // BLK_PER_KT == n).astype(np.float32)


def _block_onehot():
    p = np.arange(OH_PERIOD)[:, None, None]
    c = np.arange(LANES)[None, :, None]
    l = np.arange(KT)[None, None, :]
    return (c == (BLK_PER_KT * p + l // SLC_BLOCK) % LANES).astype(np.float32)


def kernel(x, mem, w_in, w_cmp_k1, w_cmp_k2, w_cmp_v1, w_cmp_v2, pe_cmp_k, pe_cmp_v,
           gm_ln_g, gm_ln_b, w_spatial, b_spatial, w_mem_kv, w_out, ln_g, ln_b):
    b, t, _ = x.shape
    assert b == 1 and t % KT == 0 and t >= WSPAN
    alpha = 2.0 ** 0.25
    nch = t // CMP_STRIDE
    nsb = t // SLC_BLOCK
    nsbp = -(-nsb // LANES) * LANES

    sizes = (NSA_W, KV_W, KV_W, KV_W, KV_W, KV_W, KV_W, N_GATES, NSA_W, GM_W, GM_W, GM_W, MEM_W, MEM_W)
    offs = np.concatenate([[0], np.cumsum(sizes)])
    seg = lambda n: w_in[:, int(offs[n]):int(offs[n + 1])]
    w_a = jnp.concatenate([seg(0), seg(8), seg(9), seg(10), seg(11), seg(13), seg(1), seg(2)],
                          axis=1).astype(BF16)
    w_b = jnp.concatenate([seg(3), seg(4), seg(5), seg(6), seg(12)], axis=1).astype(BF16)
    w_g = jnp.pad(seg(7), ((0, 0), (0, LANES - N_GATES))).astype(BF16)

    x2 = x[0]
    xb = x2.astype(BF16)
    proja = _matmul(xb, w_a, F32, 512, 512, "proj_f32")
    projb = _matmul(xb, w_b, BF16, 512, 512, "proj_bf16")
    gates = _matmul(xb, w_g, F32, 512, LANES, "proj_gates")
    kvm = _matmul(mem[0].astype(BF16), w_mem_kv.astype(BF16), BF16, MEM_LEN, 512, "mem_kv")

    kcvc = proja[:, 8192:8192 + 2 * KV_W].reshape(nch, CMP_STRIDE, 2, N_KV, DH)
    ch = kcvc.transpose(2, 3, 0, 1, 4).reshape(2, N_KV, nch, CMP_STRIDE * DH)
    pe = jnp.stack([pe_cmp_k, pe_cmp_v])
    pe_lo = pe[:, :CMP_STRIDE].reshape(2, 1, CMP_STRIDE * DH)
    pe_hi = pe[:, CMP_STRIDE:].reshape(2, 1, CMP_STRIDE * DH)
    w1 = jnp.stack([w_cmp_k1, w_cmp_v1]).astype(BF16)
    w2 = jnp.stack([w_cmp_k2, w_cmp_v2]).astype(BF16)
    kvc = _compress(ch, pe_lo, pe_hi, w1, w2)

    slopes = jnp.exp2(-8.0 * jnp.arange(1, N_HEADS + 1, dtype=F32) / N_HEADS).reshape(N_KV, HG)
    pool = jnp.asarray(_pool_matrix(nsbp, nch), BF16)
    oh = jnp.asarray(_block_onehot(), BF16)

    tmap = jnp.asarray(_tile_map(nsbp), BF16)
    o_c, mb, tf = _cmp_attention(slopes, proja, kvc[0], kvc[1], pool, tmap)
    tflags = tf[:, :, 0, :t // KT]
    o_s = _sel_attention(tflags, slopes, proja, projb, mb, oh)
    o_w = _win_attention(slopes, proja, projb)

    y = _mix(o_c, o_s, o_w, gates, proja, projb, kvm,
             gm_ln_g.reshape(1, GM_W), gm_ln_b.reshape(1, GM_W), w_spatial, b_spatial.T)
    h = _matmul(y, w_out.astype(BF16), F32, 512, 1024, "out_proj")
    out = _residual_ln(x2, h, ln_g.reshape(1, D_MODEL), ln_b.reshape(1, D_MODEL), alpha)
    return out[None]
```

```python
import functools

import numpy as np
import jax
import jax.numpy as jnp
from jax import lax
from jax.experimental import pallas as pl
from jax.experimental.pallas import tpu as pltpu

D_MODEL = 4096
DH = 128
N_HEADS = 16
HG = 4
N_KV = 4
NSA_W = N_HEADS * DH
KV_W = N_KV * DH
CMP_LEN = 32
CMP_STRIDE = 16
SLC_BLOCK = 64
TOP_N = 16
WINDOW = 512
QB = 128
CHUNK = 128
GM_W = 1024
GM_GROUPS = 4
GM_DG = GM_W // GM_GROUPS
MEM_W = 1024
MEM_LEN = 256
MEM_HEADS = 4
MEM_DH = MEM_W // MEM_HEADS
LN_EPS = 1e-5
N_GATES = N_HEADS * 3

LANES = 128
KT = 512
BLK_PER_KT = KT // SLC_BLOCK
OH_PERIOD = LANES // BLK_PER_KT
NEG = -2.0 ** 100
VMEM_LIMIT = 48 * 1024 * 1024

F32 = jnp.float32
BF16 = jnp.bfloat16
_NT = (((1,), (1,)), ((), ()))


def _params(n_axes):
    return pltpu.CompilerParams(dimension_semantics=("arbitrary",) * n_axes,
                                vmem_limit_bytes=VMEM_LIMIT)


def _mm_kernel(a_ref, b_ref, o_ref):
    o_ref[...] = jnp.dot(a_ref[...], b_ref[...], preferred_element_type=F32).astype(o_ref.dtype)


def _matmul(a, b, out_dtype, tm, tn, name):
    m, k = a.shape
    n = b.shape[1]
    tm = min(tm, m)
    return pl.pallas_call(
        _mm_kernel,
        grid=(n // tn, m // tm),
        in_specs=[pl.BlockSpec((tm, k), lambda j, i: (i, 0)),
                  pl.BlockSpec((k, tn), lambda j, i: (0, j))],
        out_specs=pl.BlockSpec((tm, tn), lambda j, i: (i, j)),
        out_shape=jax.ShapeDtypeStruct((m, n), out_dtype),
        compiler_params=_params(2),
        name=name,
    )(a, b)


def _gelu_tanh(x):
    c = np.float32(np.sqrt(2.0 / np.pi))
    return x * (0.5 * (1.0 + jnp.tanh(c * (x + 0.044715 * (x * x * x)))))


def _compress_kernel(ch_ref, pe_lo_ref, pe_hi_ref, w1_ref, w2_ref, o_ref):
    ch = ch_ref[0, 0]
    nch = ch.shape[0]
    half = CMP_STRIDE * DH
    lo = (ch + pe_lo_ref[0]).astype(BF16)
    hi = (ch + pe_hi_ref[0]).astype(BF16)
    a = jnp.dot(lo, w1_ref[0, :half, :], preferred_element_type=F32)
    b = jnp.dot(hi, w1_ref[0, half:, :], preferred_element_type=F32)
    pre = a + pltpu.roll(b, nch - 1, axis=0)
    h = _gelu_tanh(pre).astype(BF16)
    o_ref[0, 0] = jnp.dot(h, w2_ref[0], preferred_element_type=F32).astype(o_ref.dtype)


def _compress(ch, pe_lo, pe_hi, w1, w2):
    _, g, nch, cw = ch.shape
    return pl.pallas_call(
        _compress_kernel,
        grid=(2, g),
        in_specs=[pl.BlockSpec((1, 1, nch, cw), lambda s, j: (s, j, 0, 0)),
                  pl.BlockSpec((1, 1, cw), lambda s, j: (s, 0, 0)),
                  pl.BlockSpec((1, 1, cw), lambda s, j: (s, 0, 0)),
                  pl.BlockSpec((1, 2 * cw, DH), lambda s, j: (s, 0, 0)),
                  pl.BlockSpec((1, DH, DH), lambda s, j: (s, 0, 0))],
        out_specs=pl.BlockSpec((1, 1, nch, DH), lambda s, j: (s, j, 0, 0)),
        out_shape=jax.ShapeDtypeStruct((2, g, nch, DH), BF16),
        compiler_params=_params(2),
        name="compress_kv",
    )(ch, pe_lo, pe_hi, w1, w2)


def _stack_heads(q):
    return jnp.concatenate([q[:, h * DH:(h + 1) * DH] for h in range(HG)], axis=0)


def _store_heads(o_ref, o):
    for h in range(HG):
        o_ref[:, h * DH:(h + 1) * DH] = o[h * QB:(h + 1) * QB, :]


def _cmp_kernel(slope_ref, q_ref, kc_ref, vc_ref, pool_ref, tmap_ref, oc_ref, mb_ref, tf_ref):
    g = pl.program_id(0)
    i = pl.program_id(1)
    q0 = i * QB
    nch = kc_ref.shape[1]
    nsbp = pool_ref.shape[0]
    scale = np.float32(DH ** -0.5)

    qs = _stack_heads(q_ref[...]).astype(BF16)
    s = lax.dot_general(qs, kc_ref[0], _NT, preferred_element_type=F32)

    row = lax.broadcasted_iota(jnp.int32, (QB, nch), 0)
    col = lax.broadcasted_iota(jnp.int32, (QB, nch), 1)
    dist = (q0 + row) - (col * CMP_STRIDE + (CMP_LEN - 1))
    valid = dist >= 0
    distf = dist.astype(F32)

    imp = jnp.zeros((QB, nch), F32)
    for h in range(HG):
        sh = s[h * QB:(h + 1) * QB] * scale - slope_ref[g, h] * distf
        sh = jnp.where(valid, sh, NEG)
        m = jnp.max(sh, axis=1, keepdims=True)
        p = jnp.where(valid, jnp.exp(sh - m), 0.0)
        l = jnp.sum(p, axis=1, keepdims=True)
        pn = p / jnp.maximum(l, 1e-30)
        imp = imp + pn
        oc_ref[:, h * DH:(h + 1) * DH] = jnp.dot(pn.astype(BF16), vc_ref[0],
                                                 preferred_element_type=F32)

    pool = pool_ref[...]
    p1 = imp.astype(BF16)
    r1 = imp - p1.astype(F32)
    p2 = r1.astype(BF16)
    p3 = (r1 - p2.astype(F32)).astype(BF16)
    imps = (lax.dot_general(pool, p1, _NT, preferred_element_type=F32)
            + lax.dot_general(pool, p2, _NT, preferred_element_type=F32)
            + lax.dot_general(pool, p3, _NT, preferred_element_type=F32))

    j = lax.broadcasted_iota(jnp.int32, (nsbp, QB), 0)
    tq = q0 + lax.broadcasted_iota(jnp.int32, (nsbp, QB), 1)
    cur = lax.shift_right_arithmetic(tq, 6)
    forced = jnp.where(j == 0, 1, jnp.where(j == cur, 1, jnp.where(j == cur - 1, 1, 0)))
    sc = jnp.where(j > cur, -1.0, jnp.where(forced == 1, np.float32(3e38), imps))
    sel = jnp.zeros((nsbp, QB), F32)
    for _ in range(min(TOP_N, nsbp)):
        mx = jnp.max(sc, axis=0, keepdims=True)
        idx = jnp.min(jnp.where(sc == mx, j, nsbp), axis=0, keepdims=True)
        pick = j == idx
        sel = jnp.where(pick, 1.0, sel)
        sc = jnp.where(pick, -2.0, sc)
    bias = jnp.where(j <= cur, jnp.where(sel > 0.5, 0.0, NEG), NEG)
    bias_t = bias.T
    mb_ref[...] = bias_t.astype(BF16)
    anyq = jnp.max(jnp.where(bias_t == 0.0, 1.0, 0.0), axis=0, keepdims=True)
    hit = jnp.dot(jnp.broadcast_to(anyq, (8, nsbp)).astype(BF16), tmap_ref[...],
                  preferred_element_type=F32)
    tf_ref[0, 0] = jnp.where(hit > 0.0, 1, 0).astype(jnp.int32)


def _cmp_attention(slopes, qf, kc, vc, pool, tmap):
    t = qf.shape[0]
    nch = kc.shape[1]
    nsbp = pool.shape[0]
    return pl.pallas_call(
        _cmp_kernel,
        grid=(N_KV, t // QB),
        in_specs=[pl.BlockSpec(memory_space=pltpu.SMEM),
                  pl.BlockSpec((QB, HG * DH), lambda g, i: (i, g)),
                  pl.BlockSpec((1, nch, DH), lambda g, i: (g, 0, 0)),
                  pl.BlockSpec((1, nch, DH), lambda g, i: (g, 0, 0)),
                  pl.BlockSpec((nsbp, nch), lambda g, i: (0, 0)),
                  pl.BlockSpec((nsbp, LANES), lambda g, i: (0, 0))],
        out_specs=[pl.BlockSpec((QB, HG * DH), lambda g, i: (i, g)),
                   pl.BlockSpec((QB, nsbp), lambda g, i: (i, g)),
                   pl.BlockSpec((1, 1, 8, LANES), lambda g, i: (g, i, 0, 0))],
        out_shape=[jax.ShapeDtypeStruct((t, NSA_W), F32),
                   jax.ShapeDtypeStruct((t, N_KV * nsbp), BF16),
                   jax.ShapeDtypeStruct((N_KV, t // QB, 8, LANES), jnp.int32)],
        compiler_params=_params(2),
        name="cmp_attn_select",
    )(slopes, qf, kc, vc, pool, tmap)


EXT_LO0 = BLK_PER_KT
EXT_HI0 = BLK_PER_KT + 3
QSCALE = np.float32(DH ** -0.5 * np.log2(np.e))


def _aug_queries(qs, ext, sconst_ref):
    parts = []
    for h in range(HG):
        e = (ext + sconst_ref[0, h:h + 1, :]).astype(BF16)
        parts.append(jnp.concatenate([qs[h * QB:(h + 1) * QB], e], axis=1))
    return jnp.concatenate(parts, axis=0)


def _sel_kernel(lst_ref, cnt_ref, cs_ref, q_ref, ks_ref, vs_ref, mb_ref, pick_ref, kx_ref, vx_ref,
                sconst_ref, o_ref, m_scr, acc_scr, u_scr, p_scr, al_scr):
    g = pl.program_id(0)
    i = pl.program_id(1)
    q0 = i * QB
    nhalf = mb_ref.shape[1] // LANES
    n_prev = cnt_ref[g, i]

    qs = (_stack_heads(q_ref[...]) * QSCALE).astype(BF16)
    m_scr[...] = jnp.full(m_scr.shape, NEG, F32)
    acc_scr[...] = jnp.zeros(acc_scr.shape, F32)
    p_scr[1] = jnp.zeros(p_scr.shape[1:], BF16)
    al_scr[1] = jnp.ones(al_scr.shape[1:], F32)

    def score_stage(n, slot):
        tn = lst_ref[g, i, n]
        k0 = pl.multiple_of(tn * KT, KT)
        kaug = jnp.concatenate([ks_ref[pl.ds(k0, KT), :], kx_ref[...]], axis=1)
        half = lax.shift_right_logical(tn, OH_PERIOD.bit_length() - 1)
        mbh = mb_ref[:, 0:LANES]
        for hf in range(1, nhalf):
            mbh = jnp.where(half == hf, mb_ref[:, hf * LANES:(hf + 1) * LANES], mbh)
        ext = jnp.dot(mbh, pick_ref[tn & (OH_PERIOD - 1)], preferred_element_type=F32)
        u_scr[slot] = lax.dot_general(_aug_queries(qs, ext, sconst_ref), kaug, _NT,
                                      preferred_element_type=F32)

    def softmax_stage(n, slot, causal):
        k0 = lst_ref[g, i, n] * KT
        dq = (q0 - k0).astype(F32)
        if causal:
            visible = (lax.broadcasted_iota(jnp.int32, (QB, KT), 0)
                       - lax.broadcasted_iota(jnp.int32, (QB, KT), 1) + (q0 - k0)) >= 0
        for h in range(HG):
            rows = pl.ds(h * QB, QB)
            uh = u_scr[slot, rows, :]
            if causal:
                uh = jnp.where(visible, uh, NEG)
            c = cs_ref[g, h] * dq
            m_prev = m_scr[rows, :]
            m_next = jnp.maximum(m_prev, jnp.max(uh, axis=1, keepdims=True) - c)
            p_scr[slot, rows, :] = jnp.exp2(uh - (m_next + c)).astype(BF16)
            al_scr[slot, rows, :] = jnp.exp2(m_prev - m_next)
            m_scr[rows, :] = m_next

    def pv_stage(n, slot):
        k0 = pl.multiple_of(lst_ref[g, i, jnp.maximum(n, 0)] * KT, KT)
        vaug = jnp.concatenate([vs_ref[pl.ds(k0, KT), :], vx_ref[...]], axis=1)
        pv = jnp.dot(p_scr[slot], vaug, preferred_element_type=F32)
        acc_scr[...] = al_scr[slot] * acc_scr[...] + pv

    def finish():
        acc = acc_scr[...]
        _store_heads(o_ref, acc[:, 0:DH] / acc[:, DH:DH + 1])

    score_stage(0, 0)

    def pair(j, carry):
        n = 2 * j
        score_stage(n + 1, 1)
        softmax_stage(n, 0, False)
        pv_stage(n - 1, 1)
        score_stage(n + 2, 0)
        softmax_stage(n + 1, 1, False)
        pv_stage(n, 0)
        return carry

    lax.fori_loop(0, lax.shift_right_logical(n_prev, 1), pair, 0)
    n_even = n_prev & ~1

    @pl.when(n_prev == n_even)
    def _():
        softmax_stage(n_even, 0, True)
        pv_stage(n_even - 1, 1)
        pv_stage(n_even, 0)
        finish()

    @pl.when(n_prev != n_even)
    def _():
        score_stage(n_even + 1, 1)
        softmax_stage(n_even, 0, False)
        pv_stage(n_even - 1, 1)
        softmax_stage(n_even + 1, 1, True)
        pv_stage(n_even, 0)
        pv_stage(n_even + 1, 1)
        finish()


def _sel_attention(tile_list, tile_cnt, cslopes, qf, projb, mb, pick, kx, vx, sconst):
    t = qf.shape[0]
    nsbp = mb.shape[1] // N_KV
    grid_spec = pltpu.PrefetchScalarGridSpec(
        num_scalar_prefetch=2,
        grid=(N_KV, t // QB),
        in_specs=[pl.BlockSpec(memory_space=pltpu.SMEM),
                  pl.BlockSpec((QB, HG * DH), lambda g, i, *_: (i, g)),
                  pl.BlockSpec((t, DH), lambda g, i, *_: (0, g)),
                  pl.BlockSpec((t, DH), lambda g, i, *_: (0, N_KV + g)),
                  pl.BlockSpec((QB, nsbp), lambda g, i, *_: (i, g)),
                  pl.BlockSpec((OH_PERIOD, LANES, LANES), lambda g, i, *_: (0, 0, 0)),
                  pl.BlockSpec((KT, LANES), lambda g, i, *_: (0, 0)),
                  pl.BlockSpec((KT, LANES), lambda g, i, *_: (0, 0)),
                  pl.BlockSpec((1, HG, LANES), lambda g, i, *_: (g, 0, 0))],
        out_specs=pl.BlockSpec((QB, HG * DH), lambda g, i, *_: (i, g)),
        scratch_shapes=[pltpu.VMEM((HG * QB, 1), F32),
                        pltpu.VMEM((HG * QB, 2 * LANES), F32),
                        pltpu.VMEM((2, HG * QB, KT), F32),
                        pltpu.VMEM((2, HG * QB, KT), BF16),
                        pltpu.VMEM((2, HG * QB, 1), F32)])
    return pl.pallas_call(
        _sel_kernel,
        grid_spec=grid_spec,
        out_shape=jax.ShapeDtypeStruct((t, NSA_W), F32),
        compiler_params=_params(2),
        name="sel_attn",
    )(tile_list, tile_cnt, cslopes, qf, projb, projb, mb, pick, kx, vx, sconst)


def _tile_lists(tflags, t):
    nt = t // KT
    last = (jnp.arange(t // QB, dtype=jnp.int32) * QB + QB - 1) // KT
    ids = jnp.arange(nt, dtype=jnp.int32)
    active = (tflags != 0) & (ids[None, None, :] < last[None, :, None])
    cnt = jnp.sum(active, axis=-1).astype(jnp.int32)
    order = jnp.argsort(jnp.where(active, ids, nt + ids), axis=-1).astype(jnp.int32)
    lst = jnp.where(ids[None, None, :] == cnt[..., None], last[None, :, None], order)
    return lst, cnt


WSPAN = WINDOW + QB


def _win_kernel(q_ref, kw_ref, vw_ref, band_ref, kx_ref, vx_ref, sconst_ref, o_ref):
    i = pl.program_id(1)
    q0 = i * QB
    start = pl.multiple_of(jnp.maximum(q0 - WINDOW, 0), QB)

    qs = (_stack_heads(q_ref[...]) * QSCALE).astype(BF16)
    kaug = jnp.concatenate([kw_ref[pl.ds(start, WSPAN), :], kx_ref[...]], axis=1)
    vaug = jnp.concatenate([vw_ref[pl.ds(start, WSPAN), :], vx_ref[...]], axis=1)
    u = lax.dot_general(_aug_queries(qs, jnp.zeros((QB, LANES), F32), sconst_ref), kaug, _NT,
                        preferred_element_type=F32)
    band = band_ref[lax.shift_right_logical(q0 - start, QB.bit_length() - 1)]
    ps = []
    for h in range(HG):
        uh = u[h * QB:(h + 1) * QB] + band
        m = jnp.max(uh, axis=1, keepdims=True)
        ps.append(jnp.exp2(uh - m).astype(BF16))
    pv = jnp.dot(jnp.concatenate(ps, axis=0), vaug, preferred_element_type=F32)
    _store_heads(o_ref, pv[:, 0:DH] / pv[:, DH:DH + 1])


def _win_attention(qf, projb, band, kxw, vxw, sconst):
    t = qf.shape[0]
    return pl.pallas_call(
        _win_kernel,
        grid=(N_KV, t // QB),
        in_specs=[pl.BlockSpec((QB, HG * DH), lambda g, i: (i, g)),
                  pl.BlockSpec((t, DH), lambda g, i: (0, 2 * N_KV + g)),
                  pl.BlockSpec((t, DH), lambda g, i: (0, 3 * N_KV + g)),
                  pl.BlockSpec((WINDOW // QB + 1, QB, WSPAN), lambda g, i: (0, 0, 0)),
                  pl.BlockSpec((WSPAN, LANES), lambda g, i: (0, 0)),
                  pl.BlockSpec((WSPAN, LANES), lambda g, i: (0, 0)),
                  pl.BlockSpec((1, HG, LANES), lambda g, i: (g, 0, 0))],
        out_specs=pl.BlockSpec((QB, HG * DH), lambda g, i: (i, g)),
        out_shape=jax.ShapeDtypeStruct((t, NSA_W), F32),
        compiler_params=_params(2),
        name="win_attn",
    )(qf, projb, projb, band, kxw, vxw, sconst)


def _silu(z):
    return z * jax.nn.sigmoid(z)


def _mix_kernel(oc_ref, os_ref, ow_ref, gates_ref, zn_ref, u_ref, v_ref, zg_ref, zm_ref,
                qm_ref, kvm_ref, lng_ref, lnb_ref, wsp_ref, bsp_ref, y_ref):
    sig = jax.nn.sigmoid(gates_ref[...])
    for h in range(N_HEADS):
        sl = slice(h * DH, (h + 1) * DH)
        comb = (sig[:, 3 * h:3 * h + 1] * oc_ref[:, sl]
                + sig[:, 3 * h + 1:3 * h + 2] * os_ref[:, sl]
                + sig[:, 3 * h + 2:3 * h + 3] * ow_ref[:, sl])
        y_ref[:, sl] = (comb * _silu(zn_ref[:, sl])).astype(y_ref.dtype)

    vv = v_ref[...]
    mu = jnp.mean(vv, axis=-1, keepdims=True)
    var = jnp.mean(jnp.square(vv - mu), axis=-1, keepdims=True)
    vn = ((vv - mu) * lax.rsqrt(var + LN_EPS) * lng_ref[...] + lnb_ref[...]).astype(BF16)
    causal = (lax.broadcasted_iota(jnp.int32, (CHUNK, CHUNK), 0)
              >= lax.broadcasted_iota(jnp.int32, (CHUNK, CHUNK), 1))
    for gg in range(GM_GROUPS):
        sl = slice(gg * GM_DG, (gg + 1) * GM_DG)
        ws = jnp.where(causal, wsp_ref[gg], 0.0).astype(BF16)
        sv = jnp.dot(ws, vn[:, sl], preferred_element_type=F32) + bsp_ref[:, gg:gg + 1]
        y_ref[:, NSA_W + gg * GM_DG:NSA_W + (gg + 1) * GM_DG] = (
            u_ref[:, sl] * sv * _silu(zg_ref[:, sl])).astype(y_ref.dtype)

    mscale = np.float32(MEM_DH ** -0.5)
    for h in range(MEM_HEADS):
        sl = slice(h * MEM_DH, (h + 1) * MEM_DH)
        km = kvm_ref[:, h * MEM_DH:(h + 1) * MEM_DH]
        vm = kvm_ref[:, MEM_W + h * MEM_DH:MEM_W + (h + 1) * MEM_DH]
        s = lax.dot_general(qm_ref[:, sl], km, _NT, preferred_element_type=F32) * mscale
        m = jnp.max(s, axis=-1, keepdims=True)
        e = jnp.exp(s - m)
        p = e / jnp.sum(e, axis=-1, keepdims=True)
        o = jnp.dot(p.astype(BF16), vm, preferred_element_type=F32)
        y_ref[:, NSA_W + GM_W + h * MEM_DH:NSA_W + GM_W + (h + 1) * MEM_DH] = (
            o * _silu(zm_ref[:, sl])).astype(y_ref.dtype)


def _mix(oc, osel, ow, gates, proja, projb, kvm, lng, lnb, wsp, bsp_t):
    t = oc.shape[0]
    row = lambda c: (lambda i: (i, c))
    full2 = lambda i: (0, 0)
    return pl.pallas_call(
        _mix_kernel,
        grid=(t // QB,),
        in_specs=[pl.BlockSpec((QB, NSA_W), row(0)),
                  pl.BlockSpec((QB, NSA_W), row(0)),
                  pl.BlockSpec((QB, NSA_W), row(0)),
                  pl.BlockSpec((QB, LANES), row(0)),
                  pl.BlockSpec((QB, NSA_W), row(1)),
                  pl.BlockSpec((QB, GM_W), row(4)),
                  pl.BlockSpec((QB, GM_W), row(5)),
                  pl.BlockSpec((QB, GM_W), row(6)),
                  pl.BlockSpec((QB, MEM_W), row(7)),
                  pl.BlockSpec((QB, MEM_W), row(2)),
                  pl.BlockSpec((MEM_LEN, 2 * MEM_W), full2),
                  pl.BlockSpec((1, GM_W), full2),
                  pl.BlockSpec((1, GM_W), full2),
                  pl.BlockSpec((GM_GROUPS, CHUNK, CHUNK), lambda i: (0, 0, 0)),
                  pl.BlockSpec((CHUNK, GM_GROUPS), full2)],
        out_specs=pl.BlockSpec((QB, D_MODEL), row(0)),
        out_shape=jax.ShapeDtypeStruct((t, D_MODEL), BF16),
        compiler_params=_params(1),
        name="mix_heads",
    )(oc, osel, ow, gates, proja, proja, proja, proja, proja, projb, kvm, lng, lnb, wsp, bsp_t)


def _ln_kernel(alpha, x_ref, h_ref, g_ref, b_ref, o_ref):
    z = alpha * x_ref[...] + h_ref[...]
    mu = jnp.mean(z, axis=-1, keepdims=True)
    var = jnp.mean(jnp.square(z - mu), axis=-1, keepdims=True)
    o_ref[...] = (z - mu) * lax.rsqrt(var + LN_EPS) * g_ref[...] + b_ref[...]


def _residual_ln(x2, h, g, b, alpha, tm=256):
    t, d = x2.shape
    tm = min(tm, t)
    return pl.pallas_call(
        functools.partial(_ln_kernel, np.float32(alpha)),
        grid=(t // tm,),
        in_specs=[pl.BlockSpec((tm, d), lambda i: (i, 0)),
                  pl.BlockSpec((tm, d), lambda i: (i, 0)),
                  pl.BlockSpec((1, d), lambda i: (0, 0)),
                  pl.BlockSpec((1, d), lambda i: (0, 0))],
        out_specs=pl.BlockSpec((tm, d), lambda i: (i, 0)),
        out_shape=jax.ShapeDtypeStruct((t, d), F32),
        compiler_params=_params(1),
        name="residual_ln",
    )(x2, h, g, b)


def _pool_matrix(nsbp, nch):
    j = np.arange(nsbp)[:, None]
    c = np.arange(nch)[None, :]
    ratio = SLC_BLOCK // CMP_STRIDE
    return ((c >= ratio * j - 1) & (c <= ratio * j + ratio - 1)).astype(np.float32)


def _tile_map(nsbp):
    j = np.arange(nsbp)[:, None]
    n = np.arange(LANES)[None, :]
    return (j // BLK_PER_KT == n).astype(np.float32)


def _block_pick():
    p = np.arange(OH_PERIOD)[:, None, None]
    j = np.arange(LANES)[None, :, None]
    c = np.arange(LANES)[None, None, :]
    return ((c < BLK_PER_KT) & (j == BLK_PER_KT * p + c)).astype(np.float32)


def _key_ext(n_keys, with_blocks):
    l = np.arange(n_keys)[:, None]
    c = np.arange(LANES)[None, :]
    kx = np.zeros((n_keys, LANES), np.float32)
    if with_blocks:
        kx += (c < BLK_PER_KT) & (l // SLC_BLOCK == c)
    kx += np.where((c >= EXT_LO0) & (c < EXT_LO0 + 3), l % 256, 0)
    kx += np.where((c >= EXT_HI0) & (c < EXT_HI0 + 3), l // 256, 0)
    return kx


def _ones_lane0(n_keys):
    vx = np.zeros((n_keys, LANES), np.float32)
    vx[:, 0] = 1.0
    return vx


def _window_band():
    d = np.arange(WINDOW // QB + 1)[:, None, None] * QB
    dist = d + np.arange(QB)[None, :, None] - np.arange(WSPAN)[None, None, :]
    return np.where((dist >= 0) & (dist < WINDOW), 0.0, NEG).astype(np.float32)


def _slope_lanes(cslopes):
    to_bf16 = lambda v: v.astype(BF16).astype(np.float32)
    s1 = to_bf16(cslopes)
    r1 = cslopes - s1
    s2 = to_bf16(r1)
    s3 = to_bf16(r1 - s2)
    pieces = np.stack([s1, s2, s3, 256.0 * s1, 256.0 * s2, 256.0 * s3], axis=-1)
    return np.pad(pieces, ((0, 0), (0, 0), (EXT_LO0, LANES - EXT_LO0 - 6))).astype(np.float32)


def kernel(x, mem, w_in, w_cmp_k1, w_cmp_k2, w_cmp_v1, w_cmp_v2, pe_cmp_k, pe_cmp_v,
           gm_ln_g, gm_ln_b, w_spatial, b_spatial, w_mem_kv, w_out, ln_g, ln_b):
    b, t, _ = x.shape
    assert b == 1 and t % KT == 0 and t >= WSPAN
    alpha = 2.0 ** 0.25
    nch = t // CMP_STRIDE
    nsb = t // SLC_BLOCK
    nsbp = -(-nsb // LANES) * LANES

    sizes = (NSA_W, KV_W, KV_W, KV_W, KV_W, KV_W, KV_W, N_GATES, NSA_W, GM_W, GM_W, GM_W, MEM_W, MEM_W)
    offs = np.concatenate([[0], np.cumsum(sizes)])
    seg = lambda n: w_in[:, int(offs[n]):int(offs[n + 1])]
    w_a = jnp.concatenate([seg(0), seg(8), seg(9), seg(10), seg(11), seg(13), seg(1), seg(2)],
                          axis=1).astype(BF16)
    w_b = jnp.concatenate([seg(3), seg(4), seg(5), seg(6), seg(12)], axis=1).astype(BF16)
    w_g = jnp.pad(seg(7), ((0, 0), (0, LANES - N_GATES))).astype(BF16)

    x2 = x[0]
    xb = x2.astype(BF16)
    proja = _matmul(xb, w_a, F32, 512, 512, "proj_f32")
    projb = _matmul(xb, w_b, BF16, 512, 512, "proj_bf16")
    gates = _matmul(xb, w_g, F32, 512, LANES, "proj_gates")
    kvm = _matmul(mem[0].astype(BF16), w_mem_kv.astype(BF16), BF16, MEM_LEN, 512, "mem_kv")

    kcvc = proja[:, 8192:8192 + 2 * KV_W].reshape(nch, CMP_STRIDE, 2, N_KV, DH)
    ch = kcvc.transpose(2, 3, 0, 1, 4).reshape(2, N_KV, nch, CMP_STRIDE * DH)
    pe = jnp.stack([pe_cmp_k, pe_cmp_v])
    pe_lo = pe[:, :CMP_STRIDE].reshape(2, 1, CMP_STRIDE * DH)
    pe_hi = pe[:, CMP_STRIDE:].reshape(2, 1, CMP_STRIDE * DH)
    w1 = jnp.stack([w_cmp_k1, w_cmp_v1]).astype(BF16)
    w2 = jnp.stack([w_cmp_k2, w_cmp_v2]).astype(BF16)
    kvc = _compress(ch, pe_lo, pe_hi, w1, w2)

    slopes_np = np.exp2(-8.0 * np.arange(1, N_HEADS + 1, dtype=np.float32) / N_HEADS
                        ).astype(np.float32).reshape(N_KV, HG)
    cslopes_np = (slopes_np * np.float32(np.log2(np.e))).astype(np.float32)
    slopes = jnp.asarray(slopes_np)
    cslopes = jnp.asarray(cslopes_np)
    sconst = jnp.asarray(_slope_lanes(cslopes_np))
    pool = jnp.asarray(_pool_matrix(nsbp, nch), BF16)

    tmap = jnp.asarray(_tile_map(nsbp), BF16)
    o_c, mb, tf = _cmp_attention(slopes, proja, kvc[0], kvc[1], pool, tmap)
    tflags = tf[:, :, 0, :t // KT]
    tile_list, tile_cnt = _tile_lists(tflags, t)
    o_s = _sel_attention(tile_list, tile_cnt, cslopes, proja, projb, mb, jnp.asarray(_block_pick(), BF16),
                         jnp.asarray(_key_ext(KT, True), BF16), jnp.asarray(_ones_lane0(KT), BF16), sconst)
    o_w = _win_attention(proja, projb, jnp.asarray(_window_band(), F32),
                         jnp.asarray(_key_ext(WSPAN, False), BF16), jnp.asarray(_ones_lane0(WSPAN), BF16),
                         sconst)

    y = _mix(o_c, o_s, o_w, gates, proja, projb, kvm,
             gm_ln_g.reshape(1, GM_W), gm_ln_b.reshape(1, GM_W), w_spatial, b_spatial.T)
    h = _matmul(y, w_out.astype(BF16), F32, 512, 1024, "out_proj")
    out = _residual_ln(x2, h, ln_g.reshape(1, D_MODEL), ln_b.reshape(1, D_MODEL), alpha)
    return out[None]
```

```python
import functools

import numpy as np
import jax
import jax.numpy as jnp
from jax import lax
from jax.experimental import pallas as pl
from jax.experimental.pallas import tpu as pltpu

D_MODEL = 4096
DH = 128
N_HEADS = 16
HG = 4
N_KV = 4
NSA_W = N_HEADS * DH
KV_W = N_KV * DH
CMP_LEN = 32
CMP_STRIDE = 16
SLC_BLOCK = 64
TOP_N = 16
WINDOW = 512
QB = 128
CHUNK = 128
GM_W = 1024
GM_GROUPS = 4
GM_DG = GM_W // GM_GROUPS
MEM_W = 1024
MEM_LEN = 256
MEM_HEADS = 4
MEM_DH = MEM_W // MEM_HEADS
LN_EPS = 1e-5
N_GATES = N_HEADS * 3

LANES = 128
KT = 512
BLK_PER_KT = KT // SLC_BLOCK
OH_PERIOD = LANES // BLK_PER_KT
NEG = -2.0 ** 100
VMEM_LIMIT = 48 * 1024 * 1024

F32 = jnp.float32
BF16 = jnp.bfloat16
_NT = (((1,), (1,)), ((), ()))


def _params(n_axes):
    return pltpu.CompilerParams(dimension_semantics=("arbitrary",) * n_axes,
                                vmem_limit_bytes=VMEM_LIMIT)


def _mm_kernel(a_ref, b_ref, o_ref):
    o_ref[...] = jnp.dot(a_ref[...], b_ref[...], preferred_element_type=F32).astype(o_ref.dtype)


def _matmul(a, b, out_dtype, tm, tn, name):
    m, k = a.shape
    n = b.shape[1]
    tm = min(tm, m)
    return pl.pallas_call(
        _mm_kernel,
        grid=(n // tn, m // tm),
        in_specs=[pl.BlockSpec((tm, k), lambda j, i: (i, 0)),
                  pl.BlockSpec((k, tn), lambda j, i: (0, j))],
        out_specs=pl.BlockSpec((tm, tn), lambda j, i: (i, j)),
        out_shape=jax.ShapeDtypeStruct((m, n), out_dtype),
        compiler_params=_params(2),
        name=name,
    )(a, b)


def _gelu_tanh(x):
    c = np.float32(np.sqrt(2.0 / np.pi))
    return x * (0.5 * (1.0 + jnp.tanh(c * (x + 0.044715 * (x * x * x)))))


def _compress_kernel(ch_ref, pe_lo_ref, pe_hi_ref, w1_ref, w2_ref, o_ref):
    ch = ch_ref[0, 0]
    nch = ch.shape[0]
    half = CMP_STRIDE * DH
    lo = (ch + pe_lo_ref[0]).astype(BF16)
    hi = (ch + pe_hi_ref[0]).astype(BF16)
    a = jnp.dot(lo, w1_ref[0, :half, :], preferred_element_type=F32)
    b = jnp.dot(hi, w1_ref[0, half:, :], preferred_element_type=F32)
    pre = a + pltpu.roll(b, nch - 1, axis=0)
    h = _gelu_tanh(pre).astype(BF16)
    o_ref[0, 0] = jnp.dot(h, w2_ref[0], preferred_element_type=F32).astype(o_ref.dtype)


def _compress(ch, pe_lo, pe_hi, w1, w2):
    _, g, nch, cw = ch.shape
    return pl.pallas_call(
        _compress_kernel,
        grid=(2, g),
        in_specs=[pl.BlockSpec((1, 1, nch, cw), lambda s, j: (s, j, 0, 0)),
                  pl.BlockSpec((1, 1, cw), lambda s, j: (s, 0, 0)),
                  pl.BlockSpec((1, 1, cw), lambda s, j: (s, 0, 0)),
                  pl.BlockSpec((1, 2 * cw, DH), lambda s, j: (s, 0, 0)),
                  pl.BlockSpec((1, DH, DH), lambda s, j: (s, 0, 0))],
        out_specs=pl.BlockSpec((1, 1, nch, DH), lambda s, j: (s, j, 0, 0)),
        out_shape=jax.ShapeDtypeStruct((2, g, nch, DH), BF16),
        compiler_params=_params(2),
        name="compress_kv",
    )(ch, pe_lo, pe_hi, w1, w2)


def _stack_heads(q):
    return jnp.concatenate([q[:, h * DH:(h + 1) * DH] for h in range(HG)], axis=0)


def _store_heads(o_ref, o, blk=0):
    for h in range(HG):
        o_ref[blk * QB:(blk + 1) * QB, h * DH:(h + 1) * DH] = o[h * QB:(h + 1) * QB, :]


CMP_QBLOCKS = 2
CMP_COL_STEP = 256
FORCED_BLOCKS = 3
QSCALE = np.float32(DH ** -0.5 * np.log2(np.e))


def _cmp_block(g, i, w, r, cs_ref, q, kc_ref, vc_ref, pool_ref, tmap_ref, oc_ref, mb_ref, tf_ref, blk):
    q0 = i * QB
    rows_out = slice(blk * QB, (blk + 1) * QB)
    nsbp = mb_ref.shape[1]
    n_sel = min(TOP_N, r)

    qs = (_stack_heads(q) * QSCALE).astype(BF16)
    s = lax.dot_general(qs, kc_ref[0, 0:w, :], _NT, preferred_element_type=F32)
    cpos = lax.broadcasted_iota(jnp.int32, (1, w), 1) * CMP_STRIDE + (CMP_LEN - 1) - q0
    row = lax.broadcasted_iota(jnp.int32, (QB, 1), 0)
    maskb = jnp.where(cpos <= row, 0.0, NEG)
    cposf = cpos.astype(F32)
    row_ok = jnp.where(q0 + row >= CMP_LEN - 1, 1.0, 0.0)

    vc = vc_ref[0, 0:w, :]
    imp = jnp.zeros((QB, w), F32)
    for h in range(HG):
        t = s[h * QB:(h + 1) * QB] + cs_ref[g, h] * cposf + maskb
        m = jnp.max(t, axis=1, keepdims=True)
        p = jnp.exp2(t - m)
        l = jnp.sum(p, axis=1, keepdims=True)
        pn = p * (row_ok / jnp.maximum(l, 1e-30))
        imp = imp + pn
        oc_ref[rows_out, h * DH:(h + 1) * DH] = jnp.dot(pn.astype(BF16), vc, preferred_element_type=F32)

    pool = pool_ref[0:r, 0:w]
    p1 = imp.astype(BF16)
    r1 = imp - p1.astype(F32)
    p2 = r1.astype(BF16)
    p3 = (r1 - p2.astype(F32)).astype(BF16)
    imps = (lax.dot_general(pool, p1, _NT, preferred_element_type=F32)
            + lax.dot_general(pool, p2, _NT, preferred_element_type=F32)
            + lax.dot_general(pool, p3, _NT, preferred_element_type=F32))

    j = lax.broadcasted_iota(jnp.int32, (r, QB), 0)
    tq = q0 + lax.broadcasted_iota(jnp.int32, (r, QB), 1)
    cur = lax.shift_right_arithmetic(tq, SLC_BLOCK.bit_length() - 1)
    forced = jnp.where(j == 0, 1, jnp.where(j == cur, 1, jnp.where(j == cur - 1, 1, 0)))
    sc = jnp.where(j > cur, -1.0, jnp.where(forced == 1, -2.0, imps))
    for _ in range(n_sel - FORCED_BLOCKS):
        mx = jnp.max(sc, axis=0, keepdims=True)
        idx = jnp.min(jnp.where(sc == mx, j, r), axis=0, keepdims=True)
        sc = jnp.where(j == idx, -2.0, sc)
    bias = jnp.where(j <= cur, jnp.where(sc == -2.0, 0.0, NEG), NEG)
    bias_t = bias.T
    mb_ref[rows_out, 0:r] = bias_t.astype(BF16)
    if r < nsbp:
        mb_ref[rows_out, r:nsbp] = jnp.full((QB, nsbp - r), NEG, BF16)
    anyq = jnp.max(jnp.where(bias_t == 0.0, 1.0, 0.0), axis=0, keepdims=True)
    hit = jnp.dot(jnp.broadcast_to(anyq, (8, r)).astype(BF16), tmap_ref[0:r, :],
                  preferred_element_type=F32)
    tf_ref[0, blk] = jnp.where(hit > 0.0, 1, 0).astype(jnp.int32)


def _cmp_kernel(cs_ref, q_ref, kc_ref, vc_ref, pool_ref, tmap_ref, oc_ref, mb_ref, tf_ref):
    g = pl.program_id(0)
    i0 = pl.program_id(1) * CMP_QBLOCKS
    nch = kc_ref.shape[1]
    nsbp = pool_ref.shape[0]
    step = min(CMP_COL_STEP, nch)
    need = (i0 + CMP_QBLOCKS - 1) * (QB // CMP_STRIDE) + (QB - CMP_LEN) // CMP_STRIDE
    variant = lax.shift_right_logical(need, step.bit_length() - 1)
    for v in range(nch // step):
        w = (v + 1) * step
        r = min(nsbp, -(-(w // (SLC_BLOCK // CMP_STRIDE)) // LANES) * LANES)

        @pl.when(variant == v)
        def _(w=w, r=r):
            for blk in range(CMP_QBLOCKS):
                _cmp_block(g, i0 + blk, w, r, cs_ref, q_ref[blk * QB:(blk + 1) * QB, :], kc_ref, vc_ref,
                           pool_ref, tmap_ref, oc_ref, mb_ref, tf_ref, blk)


def _cmp_attention(cslopes, qf, kc, vc, pool, tmap):
    t = qf.shape[0]
    nch = kc.shape[1]
    nsbp = pool.shape[0]
    rows = CMP_QBLOCKS * QB
    return pl.pallas_call(
        _cmp_kernel,
        grid=(N_KV, t // rows),
        in_specs=[pl.BlockSpec(memory_space=pltpu.SMEM),
                  pl.BlockSpec((rows, HG * DH), lambda g, i: (i, g)),
                  pl.BlockSpec((1, nch, DH), lambda g, i: (g, 0, 0)),
                  pl.BlockSpec((1, nch, DH), lambda g, i: (g, 0, 0)),
                  pl.BlockSpec((nsbp, nch), lambda g, i: (0, 0)),
                  pl.BlockSpec((nsbp, LANES), lambda g, i: (0, 0))],
        out_specs=[pl.BlockSpec((rows, HG * DH), lambda g, i: (i, g)),
                   pl.BlockSpec((rows, nsbp), lambda g, i: (i, g)),
                   pl.BlockSpec((1, CMP_QBLOCKS, 8, LANES), lambda g, i: (g, i, 0, 0))],
        out_shape=[jax.ShapeDtypeStruct((t, NSA_W), F32),
                   jax.ShapeDtypeStruct((t, N_KV * nsbp), BF16),
                   jax.ShapeDtypeStruct((N_KV, t // QB, 8, LANES), jnp.int32)],
        compiler_params=_params(2),
        name="cmp_attn_select",
    )(cslopes, qf, kc, vc, pool, tmap)


EXT_LO0 = BLK_PER_KT
EXT_HI0 = BLK_PER_KT + 3


def _aug_queries(qs, ext, sconst_ref):
    parts = []
    for h in range(HG):
        e = (ext + sconst_ref[0, h:h + 1, :]).astype(BF16)
        parts.append(jnp.concatenate([qs[h * QB:(h + 1) * QB], e], axis=1))
    return jnp.concatenate(parts, axis=0)


def _sel_kernel(lst_ref, cnt_ref, cs_ref, q_ref, ks_ref, vs_ref, mb_ref, pick_ref, kx_ref, vx_ref,
                sconst_ref, o_ref, m_scr, acc_scr, u_scr, p_scr, al_scr):
    g = pl.program_id(0)
    i = pl.program_id(1)
    q0 = i * QB
    nhalf = mb_ref.shape[1] // LANES
    n_prev = cnt_ref[g, i]

    qs = (_stack_heads(q_ref[...]) * QSCALE).astype(BF16)
    m_scr[...] = jnp.full(m_scr.shape, NEG, F32)
    acc_scr[...] = jnp.zeros(acc_scr.shape, F32)
    p_scr[1] = jnp.zeros(p_scr.shape[1:], BF16)
    al_scr[1] = jnp.ones(al_scr.shape[1:], F32)

    def score_stage(n, slot):
        tn = lst_ref[g, i, n]
        k0 = pl.multiple_of(tn * KT, KT)
        kaug = jnp.concatenate([ks_ref[pl.ds(k0, KT), :], kx_ref[...]], axis=1)
        half = lax.shift_right_logical(tn, OH_PERIOD.bit_length() - 1)
        mbh = mb_ref[:, 0:LANES]
        for hf in range(1, nhalf):
            mbh = jnp.where(half == hf, mb_ref[:, hf * LANES:(hf + 1) * LANES], mbh)
        ext = jnp.dot(mbh, pick_ref[tn & (OH_PERIOD - 1)], preferred_element_type=F32)
        u_scr[slot] = lax.dot_general(_aug_queries(qs, ext, sconst_ref), kaug, _NT,
                                      preferred_element_type=F32)

    def softmax_stage(n, slot, causal):
        k0 = lst_ref[g, i, n] * KT
        dq = (q0 - k0).astype(F32)
        if causal:
            visible = (lax.broadcasted_iota(jnp.int32, (QB, KT), 0)
                       - lax.broadcasted_iota(jnp.int32, (QB, KT), 1) + (q0 - k0)) >= 0
        for h in range(HG):
            rows = pl.ds(h * QB, QB)
            uh = u_scr[slot, rows, :]
            if causal:
                uh = jnp.where(visible, uh, NEG)
            c = cs_ref[g, h] * dq
            m_prev = m_scr[rows, :]
            m_next = jnp.maximum(m_prev, jnp.max(uh, axis=1, keepdims=True) - c)
            p_scr[slot, rows, :] = jnp.exp2(uh - (m_next + c)).astype(BF16)
            al_scr[slot, rows, :] = jnp.exp2(m_prev - m_next)
            m_scr[rows, :] = m_next

    def pv_stage(n, slot):
        k0 = pl.multiple_of(lst_ref[g, i, jnp.maximum(n, 0)] * KT, KT)
        vaug = jnp.concatenate([vs_ref[pl.ds(k0, KT), :], vx_ref[...]], axis=1)
        pv = jnp.dot(p_scr[slot], vaug, preferred_element_type=F32)
        acc_scr[...] = al_scr[slot] * acc_scr[...] + pv

    def finish():
        acc = acc_scr[...]
        _store_heads(o_ref, acc[:, 0:DH] / acc[:, DH:DH + 1])

    score_stage(0, 0)

    def pair(j, carry):
        n = 2 * j
        score_stage(n + 1, 1)
        softmax_stage(n, 0, False)
        pv_stage(n - 1, 1)
        score_stage(n + 2, 0)
        softmax_stage(n + 1, 1, False)
        pv_stage(n, 0)
        return carry

    lax.fori_loop(0, lax.shift_right_logical(n_prev, 1), pair, 0)
    n_even = n_prev & ~1

    @pl.when(n_prev == n_even)
    def _():
        softmax_stage(n_even, 0, True)
        pv_stage(n_even - 1, 1)
        pv_stage(n_even, 0)
        finish()

    @pl.when(n_prev != n_even)
    def _():
        score_stage(n_even + 1, 1)
        softmax_stage(n_even, 0, False)
        pv_stage(n_even - 1, 1)
        softmax_stage(n_even + 1, 1, True)
        pv_stage(n_even, 0)
        pv_stage(n_even + 1, 1)
        finish()


def _sel_attention(tile_list, tile_cnt, cslopes, qf, projb, mb, pick, kx, vx, sconst):
    t = qf.shape[0]
    nsbp = mb.shape[1] // N_KV
    grid_spec = pltpu.PrefetchScalarGridSpec(
        num_scalar_prefetch=2,
        grid=(N_KV, t // QB),
        in_specs=[pl.BlockSpec(memory_space=pltpu.SMEM),
                  pl.BlockSpec((QB, HG * DH), lambda g, i, *_: (i, g)),
                  pl.BlockSpec((t, DH), lambda g, i, *_: (0, g)),
                  pl.BlockSpec((t, DH), lambda g, i, *_: (0, N_KV + g)),
                  pl.BlockSpec((QB, nsbp), lambda g, i, *_: (i, g)),
                  pl.BlockSpec((OH_PERIOD, LANES, LANES), lambda g, i, *_: (0, 0, 0)),
                  pl.BlockSpec((KT, LANES), lambda g, i, *_: (0, 0)),
                  pl.BlockSpec((KT, LANES), lambda g, i, *_: (0, 0)),
                  pl.BlockSpec((1, HG, LANES), lambda g, i, *_: (g, 0, 0))],
        out_specs=pl.BlockSpec((QB, HG * DH), lambda g, i, *_: (i, g)),
        scratch_shapes=[pltpu.VMEM((HG * QB, 1), F32),
                        pltpu.VMEM((HG * QB, 2 * LANES), F32),
                        pltpu.VMEM((2, HG * QB, KT), F32),
                        pltpu.VMEM((2, HG * QB, KT), BF16),
                        pltpu.VMEM((2, HG * QB, 1), F32)])
    return pl.pallas_call(
        _sel_kernel,
        grid_spec=grid_spec,
        out_shape=jax.ShapeDtypeStruct((t, NSA_W), F32),
        compiler_params=_params(2),
        name="sel_attn",
    )(tile_list, tile_cnt, cslopes, qf, projb, projb, mb, pick, kx, vx, sconst)


def _tile_lists(tflags, t):
    nt = t // KT
    last = (jnp.arange(t // QB, dtype=jnp.int32) * QB + QB - 1) // KT
    ids = jnp.arange(nt, dtype=jnp.int32)
    active = (tflags != 0) & (ids[None, None, :] < last[None, :, None])
    cnt = jnp.sum(active, axis=-1).astype(jnp.int32)
    order = jnp.argsort(jnp.where(active, ids, nt + ids), axis=-1).astype(jnp.int32)
    lst = jnp.where(ids[None, None, :] == cnt[..., None], last[None, :, None], order)
    return lst, cnt


WSPAN = WINDOW + QB


WIN_QBLOCKS = 2


def _win_kernel(q_ref, kw_ref, vw_ref, band_ref, kx_ref, vx_ref, sconst_ref, o_ref):
    for blk in range(WIN_QBLOCKS):
        q0 = (pl.program_id(1) * WIN_QBLOCKS + blk) * QB
        start = pl.multiple_of(jnp.maximum(q0 - WINDOW, 0), QB)

        qs = (_stack_heads(q_ref[blk * QB:(blk + 1) * QB, :]) * QSCALE).astype(BF16)
        kaug = jnp.concatenate([kw_ref[pl.ds(start, WSPAN), :], kx_ref[...]], axis=1)
        vaug = jnp.concatenate([vw_ref[pl.ds(start, WSPAN), :], vx_ref[...]], axis=1)
        u = lax.dot_general(_aug_queries(qs, jnp.zeros((QB, LANES), F32), sconst_ref), kaug, _NT,
                            preferred_element_type=F32)
        band = band_ref[lax.shift_right_logical(q0 - start, QB.bit_length() - 1)]
        ps = []
        for h in range(HG):
            uh = u[h * QB:(h + 1) * QB] + band
            m = jnp.max(uh, axis=1, keepdims=True)
            ps.append(jnp.exp2(uh - m).astype(BF16))
        pv = jnp.dot(jnp.concatenate(ps, axis=0), vaug, preferred_element_type=F32)
        _store_heads(o_ref, pv[:, 0:DH] / pv[:, DH:DH + 1], blk)


def _win_attention(qf, projb, band, kxw, vxw, sconst):
    t = qf.shape[0]
    rows = WIN_QBLOCKS * QB
    return pl.pallas_call(
        _win_kernel,
        grid=(N_KV, t // rows),
        in_specs=[pl.BlockSpec((rows, HG * DH), lambda g, i: (i, g)),
                  pl.BlockSpec((t, DH), lambda g, i: (0, 2 * N_KV + g)),
                  pl.BlockSpec((t, DH), lambda g, i: (0, 3 * N_KV + g)),
                  pl.BlockSpec((WINDOW // QB + 1, QB, WSPAN), lambda g, i: (0, 0, 0)),
                  pl.BlockSpec((WSPAN, LANES), lambda g, i: (0, 0)),
                  pl.BlockSpec((WSPAN, LANES), lambda g, i: (0, 0)),
                  pl.BlockSpec((1, HG, LANES), lambda g, i: (g, 0, 0))],
        out_specs=pl.BlockSpec((rows, HG * DH), lambda g, i: (i, g)),
        out_shape=jax.ShapeDtypeStruct((t, NSA_W), F32),
        compiler_params=_params(2),
        name="win_attn",
    )(qf, projb, projb, band, kxw, vxw, sconst)


def _silu(z):
    return z * jax.nn.sigmoid(z)


def _mix_kernel(oc_ref, os_ref, ow_ref, gates_ref, zn_ref, u_ref, v_ref, zg_ref, zm_ref,
                qm_ref, kvm_ref, lng_ref, lnb_ref, wsp_ref, bsp_ref, y_ref):
    sig = jax.nn.sigmoid(gates_ref[...])
    for h in range(N_HEADS):
        sl = slice(h * DH, (h + 1) * DH)
        comb = (sig[:, 3 * h:3 * h + 1] * oc_ref[:, sl]
                + sig[:, 3 * h + 1:3 * h + 2] * os_ref[:, sl]
                + sig[:, 3 * h + 2:3 * h + 3] * ow_ref[:, sl])
        y_ref[:, sl] = (comb * _silu(zn_ref[:, sl])).astype(y_ref.dtype)

    vv = v_ref[...]
    mu = jnp.mean(vv, axis=-1, keepdims=True)
    var = jnp.mean(jnp.square(vv - mu), axis=-1, keepdims=True)
    vn = ((vv - mu) * lax.rsqrt(var + LN_EPS) * lng_ref[...] + lnb_ref[...]).astype(BF16)
    causal = (lax.broadcasted_iota(jnp.int32, (CHUNK, CHUNK), 0)
              >= lax.broadcasted_iota(jnp.int32, (CHUNK, CHUNK), 1))
    for gg in range(GM_GROUPS):
        sl = slice(gg * GM_DG, (gg + 1) * GM_DG)
        ws = jnp.where(causal, wsp_ref[gg], 0.0).astype(BF16)
        sv = jnp.dot(ws, vn[:, sl], preferred_element_type=F32) + bsp_ref[:, gg:gg + 1]
        y_ref[:, NSA_W + gg * GM_DG:NSA_W + (gg + 1) * GM_DG] = (
            u_ref[:, sl] * sv * _silu(zg_ref[:, sl])).astype(y_ref.dtype)

    mscale = np.float32(MEM_DH ** -0.5)
    for h in range(MEM_HEADS):
        sl = slice(h * MEM_DH, (h + 1) * MEM_DH)
        km = kvm_ref[:, h * MEM_DH:(h + 1) * MEM_DH]
        vm = kvm_ref[:, MEM_W + h * MEM_DH:MEM_W + (h + 1) * MEM_DH]
        s = lax.dot_general(qm_ref[:, sl], km, _NT, preferred_element_type=F32) * mscale
        m = jnp.max(s, axis=-1, keepdims=True)
        e = jnp.exp(s - m)
        p = e / jnp.sum(e, axis=-1, keepdims=True)
        o = jnp.dot(p.astype(BF16), vm, preferred_element_type=F32)
        y_ref[:, NSA_W + GM_W + h * MEM_DH:NSA_W + GM_W + (h + 1) * MEM_DH] = (
            o * _silu(zm_ref[:, sl])).astype(y_ref.dtype)


def _mix(oc, osel, ow, gates, proja, projb, kvm, lng, lnb, wsp, bsp_t):
    t = oc.shape[0]
    row = lambda c: (lambda i: (i, c))
    full2 = lambda i: (0, 0)
    return pl.pallas_call(
        _mix_kernel,
        grid=(t // QB,),
        in_specs=[pl.BlockSpec((QB, NSA_W), row(0)),
                  pl.BlockSpec((QB, NSA_W), row(0)),
                  pl.BlockSpec((QB, NSA_W), row(0)),
                  pl.BlockSpec((QB, LANES), row(0)),
                  pl.BlockSpec((QB, NSA_W), row(1)),
                  pl.BlockSpec((QB, GM_W), row(4)),
                  pl.BlockSpec((QB, GM_W), row(5)),
                  pl.BlockSpec((QB, GM_W), row(6)),
                  pl.BlockSpec((QB, MEM_W), row(7)),
                  pl.BlockSpec((QB, MEM_W), row(2)),
                  pl.BlockSpec((MEM_LEN, 2 * MEM_W), full2),
                  pl.BlockSpec((1, GM_W), full2),
                  pl.BlockSpec((1, GM_W), full2),
                  pl.BlockSpec((GM_GROUPS, CHUNK, CHUNK), lambda i: (0, 0, 0)),
                  pl.BlockSpec((CHUNK, GM_GROUPS), full2)],
        out_specs=pl.BlockSpec((QB, D_MODEL), row(0)),
        out_shape=jax.ShapeDtypeStruct((t, D_MODEL), BF16),
        compiler_params=_params(1),
        name="mix_heads",
    )(oc, osel, ow, gates, proja, proja, proja, proja, proja, projb, kvm, lng, lnb, wsp, bsp_t)


def _ln_kernel(alpha, x_ref, h_ref, g_ref, b_ref, o_ref):
    z = alpha * x_ref[...] + h_ref[...]
    mu = jnp.mean(z, axis=-1, keepdims=True)
    var = jnp.mean(jnp.square(z - mu), axis=-1, keepdims=True)
    o_ref[...] = (z - mu) * lax.rsqrt(var + LN_EPS) * g_ref[...] + b_ref[...]


def _residual_ln(x2, h, g, b, alpha, tm=256):
    t, d = x2.shape
    tm = min(tm, t)
    return pl.pallas_call(
        functools.partial(_ln_kernel, np.float32(alpha)),
        grid=(t // tm,),
        in_specs=[pl.BlockSpec((tm, d), lambda i: (i, 0)),
                  pl.BlockSpec((tm, d), lambda i: (i, 0)),
                  pl.BlockSpec((1, d), lambda i: (0, 0)),
                  pl.BlockSpec((1, d), lambda i: (0, 0))],
        out_specs=pl.BlockSpec((tm, d), lambda i: (i, 0)),
        out_shape=jax.ShapeDtypeStruct((t, d), F32),
        compiler_params=_params(1),
        name="residual_ln",
    )(x2, h, g, b)


def _pool_matrix(nsbp, nch):
    j = np.arange(nsbp)[:, None]
    c = np.arange(nch)[None, :]
    ratio = SLC_BLOCK // CMP_STRIDE
    return ((c >= ratio * j - 1) & (c <= ratio * j + ratio - 1)).astype(np.float32)


def _tile_map(nsbp):
    j = np.arange(nsbp)[:, None]
    n = np.arange(LANES)[None, :]
    return (j // BLK_PER_KT == n).astype(np.float32)


def _block_pick():
    p = np.arange(OH_PERIOD)[:, None, None]
    j = np.arange(LANES)[None, :, None]
    c = np.arange(LANES)[None, None, :]
    return ((c < BLK_PER_KT) & (j == BLK_PER_KT * p + c)).astype(np.float32)


def _key_ext(n_keys, with_blocks):
    l = np.arange(n_keys)[:, None]
    c = np.arange(LANES)[None, :]
    kx = np.zeros((n_keys, LANES), np.float32)
    if with_blocks:
        kx += (c < BLK_PER_KT) & (l // SLC_BLOCK == c)
    kx += np.where((c >= EXT_LO0) & (c < EXT_LO0 + 3), l % 256, 0)
    kx += np.where((c >= EXT_HI0) & (c < EXT_HI0 + 3), l // 256, 0)
    return kx


def _ones_lane0(n_keys):
    vx = np.zeros((n_keys, LANES), np.float32)
    vx[:, 0] = 1.0
    return vx


def _window_band():
    d = np.arange(WINDOW // QB + 1)[:, None, None] * QB
    dist = d + np.arange(QB)[None, :, None] - np.arange(WSPAN)[None, None, :]
    return np.where((dist >= 0) & (dist < WINDOW), 0.0, NEG).astype(np.float32)


def _slope_lanes(cslopes):
    to_bf16 = lambda v: v.astype(BF16).astype(np.float32)
    s1 = to_bf16(cslopes)
    r1 = cslopes - s1
    s2 = to_bf16(r1)
    s3 = to_bf16(r1 - s2)
    pieces = np.stack([s1, s2, s3, 256.0 * s1, 256.0 * s2, 256.0 * s3], axis=-1)
    return np.pad(pieces, ((0, 0), (0, 0), (EXT_LO0, LANES - EXT_LO0 - 6))).astype(np.float32)


def kernel(x, mem, w_in, w_cmp_k1, w_cmp_k2, w_cmp_v1, w_cmp_v2, pe_cmp_k, pe_cmp_v,
           gm_ln_g, gm_ln_b, w_spatial, b_spatial, w_mem_kv, w_out, ln_g, ln_b):
    b, t, _ = x.shape
    assert b == 1 and t % KT == 0 and t >= WSPAN
    alpha = 2.0 ** 0.25
    nch = t // CMP_STRIDE
    nsb = t // SLC_BLOCK
    nsbp = -(-nsb // LANES) * LANES

    sizes = (NSA_W, KV_W, KV_W, KV_W, KV_W, KV_W, KV_W, N_GATES, NSA_W, GM_W, GM_W, GM_W, MEM_W, MEM_W)
    offs = np.concatenate([[0], np.cumsum(sizes)])
    seg = lambda n: w_in[:, int(offs[n]):int(offs[n + 1])]
    w_a = jnp.concatenate([seg(0), seg(8), seg(9), seg(10), seg(11), seg(13), seg(1), seg(2)],
                          axis=1).astype(BF16)
    w_b = jnp.concatenate([seg(3), seg(4), seg(5), seg(6), seg(12)], axis=1).astype(BF16)
    w_g = jnp.pad(seg(7), ((0, 0), (0, LANES - N_GATES))).astype(BF16)

    x2 = x[0]
    xb = x2.astype(BF16)
    proja = _matmul(xb, w_a, F32, 512, 512, "proj_f32")
    projb = _matmul(xb, w_b, BF16, 512, 512, "proj_bf16")
    gates = _matmul(xb, w_g, F32, 512, LANES, "proj_gates")
    kvm = _matmul(mem[0].astype(BF16), w_mem_kv.astype(BF16), BF16, MEM_LEN, 512, "mem_kv")

    kcvc = proja[:, 8192:8192 + 2 * KV_W].reshape(nch, CMP_STRIDE, 2, N_KV, DH)
    ch = kcvc.transpose(2, 3, 0, 1, 4).reshape(2, N_KV, nch, CMP_STRIDE * DH)
    pe = jnp.stack([pe_cmp_k, pe_cmp_v])
    pe_lo = pe[:, :CMP_STRIDE].reshape(2, 1, CMP_STRIDE * DH)
    pe_hi = pe[:, CMP_STRIDE:].reshape(2, 1, CMP_STRIDE * DH)
    w1 = jnp.stack([w_cmp_k1, w_cmp_v1]).astype(BF16)
    w2 = jnp.stack([w_cmp_k2, w_cmp_v2]).astype(BF16)
    kvc = _compress(ch, pe_lo, pe_hi, w1, w2)

    slopes_np = np.exp2(-8.0 * np.arange(1, N_HEADS + 1, dtype=np.float32) / N_HEADS
                        ).astype(np.float32).reshape(N_KV, HG)
    cslopes_np = (slopes_np * np.float32(np.log2(np.e))).astype(np.float32)
    cslopes = jnp.asarray(cslopes_np)
    sconst = jnp.asarray(_slope_lanes(cslopes_np))
    pool = jnp.asarray(_pool_matrix(nsbp, nch), BF16)

    tmap = jnp.asarray(_tile_map(nsbp), BF16)
    o_c, mb, tf = _cmp_attention(cslopes, proja, kvc[0], kvc[1], pool, tmap)
    tflags = tf[:, :, 0, :t // KT]
    tile_list, tile_cnt = _tile_lists(tflags, t)
    o_s = _sel_attention(tile_list, tile_cnt, cslopes, proja, projb, mb, jnp.asarray(_block_pick(), BF16),
                         jnp.asarray(_key_ext(KT, True), BF16), jnp.asarray(_ones_lane0(KT), BF16), sconst)
    o_w = _win_attention(proja, projb, jnp.asarray(_window_band(), F32),
                         jnp.asarray(_key_ext(WSPAN, False), BF16), jnp.asarray(_ones_lane0(WSPAN), BF16),
                         sconst)

    y = _mix(o_c, o_s, o_w, gates, proja, projb, kvm,
             gm_ln_g.reshape(1, GM_W), gm_ln_b.reshape(1, GM_W), w_spatial, b_spatial.T)
    h = _matmul(y, w_out.astype(BF16), F32, 512, 1024, "out_proj")
    out = _residual_ln(x2, h, ln_g.reshape(1, D_MODEL), ln_b.reshape(1, D_MODEL), alpha)
    return out[None]
```

```python
import functools

import numpy as np
import jax
import jax.numpy as jnp
from jax import lax
from jax.experimental import pallas as pl
from jax.experimental.pallas import tpu as pltpu

D_MODEL = 4096
DH = 128
N_HEADS = 16
HG = 4
N_KV = 4
NSA_W = N_HEADS * DH
KV_W = N_KV * DH
CMP_LEN = 32
CMP_STRIDE = 16
SLC_BLOCK = 64
TOP_N = 16
WINDOW = 512
QB = 128
CHUNK = 128
GM_W = 1024
GM_GROUPS = 4
GM_DG = GM_W // GM_GROUPS
MEM_W = 1024
MEM_LEN = 256
MEM_HEADS = 4
MEM_DH = MEM_W // MEM_HEADS
LN_EPS = 1e-5
N_GATES = N_HEADS * 3

LANES = 128
KT = 512
BLK_PER_KT = KT // SLC_BLOCK
OH_PERIOD = LANES // BLK_PER_KT
NEG = -2.0 ** 100
VMEM_LIMIT = 48 * 1024 * 1024

F32 = jnp.float32
BF16 = jnp.bfloat16
_NT = (((1,), (1,)), ((), ()))


def _params(n_axes):
    return pltpu.CompilerParams(dimension_semantics=("arbitrary",) * n_axes,
                                vmem_limit_bytes=VMEM_LIMIT)


def _mm_kernel(scale, a_ref, b_ref, o_ref, *b_bf16):
    if b_bf16:
        @pl.when(pl.program_id(1) == 0)
        def _():
            b_bf16[0][...] = b_ref[...].astype(BF16)
        b = b_bf16[0][...]
    else:
        b = b_ref[...]
    acc = jnp.dot(a_ref[...], b, preferred_element_type=F32)
    if scale is not None:
        acc = acc * scale
    o_ref[...] = acc.astype(o_ref.dtype)


def _matmul(a, b, out_dtype, tm, tn, name, col0=0, ncols=None, scale=None):
    m, k = a.shape
    ncols = b.shape[1] - col0 if ncols is None else ncols
    tm = min(tm, m)
    assert col0 % tn == 0 and ncols % tn == 0 and m % tm == 0
    jb = col0 // tn
    scratch = [] if b.dtype == BF16 else [pltpu.VMEM((k, tn), BF16)]
    return pl.pallas_call(
        functools.partial(_mm_kernel, scale),
        grid=(ncols // tn, m // tm),
        in_specs=[pl.BlockSpec((tm, k), lambda j, i: (i, 0)),
                  pl.BlockSpec((k, tn), lambda j, i: (0, jb + j))],
        out_specs=pl.BlockSpec((tm, tn), lambda j, i: (i, j)),
        out_shape=jax.ShapeDtypeStruct((m, ncols), out_dtype),
        scratch_shapes=scratch,
        compiler_params=_params(2),
        name=name,
    )(a, b)


def _gelu_tanh(x):
    c = np.float32(np.sqrt(2.0 / np.pi))
    return x * (0.5 * (1.0 + jnp.tanh(c * (x + 0.044715 * (x * x * x)))))


def _compress_kernel(x_ref, pe_ref, w1_ref, w2_ref, o_ref):
    nch = o_ref.shape[2]
    a = jnp.zeros((nch, DH), F32)
    b = jnp.zeros((nch, DH), F32)
    for r in range(CMP_STRIDE):
        xr = x_ref[pl.ds(r, nch, stride=CMP_STRIDE), :]
        lo = (xr + pe_ref[0, r:r + 1, :]).astype(BF16)
        hi = (xr + pe_ref[0, CMP_STRIDE + r:CMP_STRIDE + r + 1, :]).astype(BF16)
        a = a + jnp.dot(lo, w1_ref[0, r * DH:(r + 1) * DH, :], preferred_element_type=F32)
        b = b + jnp.dot(hi, w1_ref[0, (CMP_STRIDE + r) * DH:(CMP_STRIDE + r + 1) * DH, :],
                        preferred_element_type=F32)
    pre = a + pltpu.roll(b, nch - 1, axis=0)
    h = _gelu_tanh(pre).astype(BF16)
    o_ref[0, 0] = jnp.dot(h, w2_ref[0], preferred_element_type=F32).astype(o_ref.dtype)


def _compress(kcvc, pe, w1, w2):
    t = kcvc.shape[0]
    nch = t // CMP_STRIDE
    return pl.pallas_call(
        _compress_kernel,
        grid=(2, N_KV),
        in_specs=[pl.BlockSpec((t, DH), lambda s, j: (0, s * N_KV + j)),
                  pl.BlockSpec((1, CMP_LEN, DH), lambda s, j: (s, 0, 0)),
                  pl.BlockSpec((1, CMP_LEN * DH, DH), lambda s, j: (s, 0, 0)),
                  pl.BlockSpec((1, DH, DH), lambda s, j: (s, 0, 0))],
        out_specs=pl.BlockSpec((1, 1, nch, DH), lambda s, j: (s, j, 0, 0)),
        out_shape=jax.ShapeDtypeStruct((2, N_KV, nch, DH), BF16),
        compiler_params=_params(2),
        name="compress_kv",
    )(kcvc, pe, w1, w2)


def _stack_heads(q):
    return jnp.concatenate([q[:, h * DH:(h + 1) * DH] for h in range(HG)], axis=0)


def _store_heads(o_ref, o, blk=0):
    for h in range(HG):
        o_ref[blk * QB:(blk + 1) * QB, h * DH:(h + 1) * DH] = o[h * QB:(h + 1) * QB, :]


CMP_QBLOCKS = 2
CMP_COL_STEP = 256
FORCED_BLOCKS = 3
QSCALE = np.float32(DH ** -0.5 * np.log2(np.e))


def _cmp_block(g, i, w, r, cs_ref, q, kc_ref, vc_ref, pool_ref, tmap_ref, oc_ref, mb_ref, tf_ref, blk):
    q0 = i * QB
    rows_out = slice(blk * QB, (blk + 1) * QB)
    nsbp = mb_ref.shape[1]
    n_sel = min(TOP_N, r)

    qs = _stack_heads(q)
    s = lax.dot_general(qs, kc_ref[0, 0:w, :], _NT, preferred_element_type=F32)
    cpos = lax.broadcasted_iota(jnp.int32, (1, w), 1) * CMP_STRIDE + (CMP_LEN - 1) - q0
    row = lax.broadcasted_iota(jnp.int32, (QB, 1), 0)
    maskb = jnp.where(cpos <= row, 0.0, NEG)
    cposf = cpos.astype(F32)
    row_ok = jnp.where(q0 + row >= CMP_LEN - 1, 1.0, 0.0)

    vc = vc_ref[0, 0:w, :]
    imp = jnp.zeros((QB, w), F32)
    for h in range(HG):
        t = s[h * QB:(h + 1) * QB] + cs_ref[g, h] * cposf + maskb
        m = jnp.max(t, axis=1, keepdims=True)
        p = jnp.exp2(t - m)
        l = jnp.sum(p, axis=1, keepdims=True)
        pn = p * (row_ok / jnp.maximum(l, 1e-30))
        imp = imp + pn
        oc_ref[rows_out, h * DH:(h + 1) * DH] = jnp.dot(pn.astype(BF16), vc, preferred_element_type=F32)

    pool = pool_ref[0:r, 0:w]
    p1 = imp.astype(BF16)
    r1 = imp - p1.astype(F32)
    p2 = r1.astype(BF16)
    p3 = (r1 - p2.astype(F32)).astype(BF16)
    imps = (lax.dot_general(pool, p1, _NT, preferred_element_type=F32)
            + lax.dot_general(pool, p2, _NT, preferred_element_type=F32)
            + lax.dot_general(pool, p3, _NT, preferred_element_type=F32))

    j = lax.broadcasted_iota(jnp.int32, (r, QB), 0)
    tq = q0 + lax.broadcasted_iota(jnp.int32, (r, QB), 1)
    cur = lax.shift_right_arithmetic(tq, SLC_BLOCK.bit_length() - 1)
    forced = jnp.where(j == 0, 1, jnp.where(j == cur, 1, jnp.where(j == cur - 1, 1, 0)))
    sc = jnp.where(j > cur, -1.0, jnp.where(forced == 1, -2.0, imps))
    for _ in range(n_sel - FORCED_BLOCKS):
        mx = jnp.max(sc, axis=0, keepdims=True)
        idx = jnp.min(jnp.where(sc == mx, j, r), axis=0, keepdims=True)
        sc = jnp.where(j == idx, -2.0, sc)
    bias = jnp.where(j <= cur, jnp.where(sc == -2.0, 0.0, NEG), NEG)
    bias_t = bias.T
    mb_ref[rows_out, 0:r] = bias_t.astype(BF16)
    if r < nsbp:
        mb_ref[rows_out, r:nsbp] = jnp.full((QB, nsbp - r), NEG, BF16)
    anyq = jnp.max(jnp.where(bias_t == 0.0, 1.0, 0.0), axis=0, keepdims=True)
    hit = jnp.dot(jnp.broadcast_to(anyq, (8, r)).astype(BF16), tmap_ref[0:r, :],
                  preferred_element_type=F32)
    tf_ref[0, blk] = jnp.where(hit > 0.0, 1, 0).astype(jnp.int32)


def _cmp_kernel(cs_ref, q_ref, kc_ref, vc_ref, pool_ref, tmap_ref, oc_ref, mb_ref, tf_ref):
    g = pl.program_id(0)
    i0 = pl.program_id(1) * CMP_QBLOCKS
    nch = kc_ref.shape[1]
    nsbp = pool_ref.shape[0]
    step = min(CMP_COL_STEP, nch)
    need = (i0 + CMP_QBLOCKS - 1) * (QB // CMP_STRIDE) + (QB - CMP_LEN) // CMP_STRIDE
    variant = lax.shift_right_logical(need, step.bit_length() - 1)
    for v in range(nch // step):
        w = (v + 1) * step
        r = min(nsbp, -(-(w // (SLC_BLOCK // CMP_STRIDE)) // LANES) * LANES)

        @pl.when(variant == v)
        def _(w=w, r=r):
            for blk in range(CMP_QBLOCKS):
                _cmp_block(g, i0 + blk, w, r, cs_ref, q_ref[blk * QB:(blk + 1) * QB, :], kc_ref, vc_ref,
                           pool_ref, tmap_ref, oc_ref, mb_ref, tf_ref, blk)


def _cmp_attention(cslopes, qf, kc, vc, pool, tmap):
    t = qf.shape[0]
    nch = kc.shape[1]
    nsbp = pool.shape[0]
    rows = CMP_QBLOCKS * QB
    return pl.pallas_call(
        _cmp_kernel,
        grid=(N_KV, t // rows),
        in_specs=[pl.BlockSpec(memory_space=pltpu.SMEM),
                  pl.BlockSpec((rows, HG * DH), lambda g, i: (i, g)),
                  pl.BlockSpec((1, nch, DH), lambda g, i: (g, 0, 0)),
                  pl.BlockSpec((1, nch, DH), lambda g, i: (g, 0, 0)),
                  pl.BlockSpec((nsbp, nch), lambda g, i: (0, 0)),
                  pl.BlockSpec((nsbp, LANES), lambda g, i: (0, 0))],
        out_specs=[pl.BlockSpec((rows, HG * DH), lambda g, i: (i, g)),
                   pl.BlockSpec((rows, nsbp), lambda g, i: (i, g)),
                   pl.BlockSpec((1, CMP_QBLOCKS, 8, LANES), lambda g, i: (g, i, 0, 0))],
        out_shape=[jax.ShapeDtypeStruct((t, NSA_W), F32),
                   jax.ShapeDtypeStruct((t, N_KV * nsbp), BF16),
                   jax.ShapeDtypeStruct((N_KV, t // QB, 8, LANES), jnp.int32)],
        compiler_params=_params(2),
        name="cmp_attn_select",
    )(cslopes, qf, kc, vc, pool, tmap)


EXT_LO0 = BLK_PER_KT
EXT_HI0 = BLK_PER_KT + 3


def _aug_queries(qs, ext, sconst_ref):
    parts = []
    for h in range(HG):
        e = (ext + sconst_ref[0, h:h + 1, :]).astype(BF16)
        parts.append(jnp.concatenate([qs[h * QB:(h + 1) * QB], e], axis=1))
    return jnp.concatenate(parts, axis=0)


def _sel_kernel(lst_ref, cnt_ref, cs_ref, q_ref, ks_ref, vs_ref, mb_ref, pick_ref, kx_ref, vx_ref,
                sconst_ref, o_ref, m_scr, acc_scr, u_scr, p_scr, al_scr):
    g = pl.program_id(0)
    i = pl.program_id(1)
    q0 = i * QB
    nhalf = mb_ref.shape[1] // LANES
    n_prev = cnt_ref[g, i]

    qs = _stack_heads(q_ref[...])
    m_scr[...] = jnp.full(m_scr.shape, NEG, F32)
    acc_scr[...] = jnp.zeros(acc_scr.shape, F32)
    p_scr[1] = jnp.zeros(p_scr.shape[1:], BF16)
    al_scr[1] = jnp.ones(al_scr.shape[1:], F32)

    def score_stage(n, slot):
        tn = lst_ref[g, i, n]
        k0 = pl.multiple_of(tn * KT, KT)
        kaug = jnp.concatenate([ks_ref[pl.ds(k0, KT), :], kx_ref[...]], axis=1)
        half = lax.shift_right_logical(tn, OH_PERIOD.bit_length() - 1)
        mbh = mb_ref[:, 0:LANES]
        for hf in range(1, nhalf):
            mbh = jnp.where(half == hf, mb_ref[:, hf * LANES:(hf + 1) * LANES], mbh)
        ext = jnp.dot(mbh, pick_ref[tn & (OH_PERIOD - 1)], preferred_element_type=F32)
        u_scr[slot] = lax.dot_general(_aug_queries(qs, ext, sconst_ref), kaug, _NT,
                                      preferred_element_type=F32)

    def softmax_stage(n, slot, causal):
        k0 = lst_ref[g, i, n] * KT
        dq = (q0 - k0).astype(F32)
        if causal:
            visible = (lax.broadcasted_iota(jnp.int32, (QB, KT), 0)
                       - lax.broadcasted_iota(jnp.int32, (QB, KT), 1) + (q0 - k0)) >= 0
        for h in range(HG):
            rows = pl.ds(h * QB, QB)
            uh = u_scr[slot, rows, :]
            if causal:
                uh = jnp.where(visible, uh, NEG)
            c = cs_ref[g, h] * dq
            m_prev = m_scr[rows, :]
            m_next = jnp.maximum(m_prev, jnp.max(uh, axis=1, keepdims=True) - c)
            p_scr[slot, rows, :] = jnp.exp2(uh - (m_next + c)).astype(BF16)
            al_scr[slot, rows, :] = jnp.exp2(m_prev - m_next)
            m_scr[rows, :] = m_next

    def pv_stage(n, slot):
        k0 = pl.multiple_of(lst_ref[g, i, jnp.maximum(n, 0)] * KT, KT)
        vaug = jnp.concatenate([vs_ref[pl.ds(k0, KT), :], vx_ref[...]], axis=1)
        pv = jnp.dot(p_scr[slot], vaug, preferred_element_type=F32)
        acc_scr[...] = al_scr[slot] * acc_scr[...] + pv

    def finish():
        acc = acc_scr[...]
        _store_heads(o_ref, acc[:, 0:DH] / acc[:, DH:DH + 1])

    score_stage(0, 0)

    def pair(j, carry):
        n = 2 * j
        score_stage(n + 1, 1)
        softmax_stage(n, 0, False)
        pv_stage(n - 1, 1)
        score_stage(n + 2, 0)
        softmax_stage(n + 1, 1, False)
        pv_stage(n, 0)
        return carry

    lax.fori_loop(0, lax.shift_right_logical(n_prev, 1), pair, 0)
    n_even = n_prev & ~1

    @pl.when(n_prev == n_even)
    def _():
        softmax_stage(n_even, 0, True)
        pv_stage(n_even - 1, 1)
        pv_stage(n_even, 0)
        finish()

    @pl.when(n_prev != n_even)
    def _():
        score_stage(n_even + 1, 1)
        softmax_stage(n_even, 0, False)
        pv_stage(n_even - 1, 1)
        softmax_stage(n_even + 1, 1, True)
        pv_stage(n_even, 0)
        pv_stage(n_even + 1, 1)
        finish()


def _sel_attention(tile_list, tile_cnt, cslopes, qf, projb, mb, pick, kx, vx, sconst):
    t = qf.shape[0]
    nsbp = mb.shape[1] // N_KV
    grid_spec = pltpu.PrefetchScalarGridSpec(
        num_scalar_prefetch=2,
        grid=(N_KV, t // QB),
        in_specs=[pl.BlockSpec(memory_space=pltpu.SMEM),
                  pl.BlockSpec((QB, HG * DH), lambda g, i, *_: (i, g)),
                  pl.BlockSpec((t, DH), lambda g, i, *_: (0, g)),
                  pl.BlockSpec((t, DH), lambda g, i, *_: (0, N_KV + g)),
                  pl.BlockSpec((QB, nsbp), lambda g, i, *_: (i, g)),
                  pl.BlockSpec((OH_PERIOD, LANES, LANES), lambda g, i, *_: (0, 0, 0)),
                  pl.BlockSpec((KT, LANES), lambda g, i, *_: (0, 0)),
                  pl.BlockSpec((KT, LANES), lambda g, i, *_: (0, 0)),
                  pl.BlockSpec((1, HG, LANES), lambda g, i, *_: (g, 0, 0))],
        out_specs=pl.BlockSpec((QB, HG * DH), lambda g, i, *_: (i, g)),
        scratch_shapes=[pltpu.VMEM((HG * QB, 1), F32),
                        pltpu.VMEM((HG * QB, 2 * LANES), F32),
                        pltpu.VMEM((2, HG * QB, KT), F32),
                        pltpu.VMEM((2, HG * QB, KT), BF16),
                        pltpu.VMEM((2, HG * QB, 1), F32)])
    return pl.pallas_call(
        _sel_kernel,
        grid_spec=grid_spec,
        out_shape=jax.ShapeDtypeStruct((t, NSA_W), F32),
        compiler_params=_params(2),
        name="sel_attn",
    )(tile_list, tile_cnt, cslopes, qf, projb, projb, mb, pick, kx, vx, sconst)


def _tile_lists(tflags, t):
    nt = t // KT
    last = (jnp.arange(t // QB, dtype=jnp.int32) * QB + QB - 1) // KT
    ids = jnp.arange(nt, dtype=jnp.int32)
    active = (tflags != 0) & (ids[None, None, :] < last[None, :, None])
    cnt = jnp.sum(active, axis=-1).astype(jnp.int32)
    order = jnp.argsort(jnp.where(active, ids, nt + ids), axis=-1).astype(jnp.int32)
    lst = jnp.where(ids[None, None, :] == cnt[..., None], last[None, :, None], order)
    return lst, cnt


WSPAN = WINDOW + QB


WIN_QBLOCKS = 2


def _win_kernel(q_ref, kw_ref, vw_ref, band_ref, kx_ref, vx_ref, sconst_ref, o_ref):
    for blk in range(WIN_QBLOCKS):
        q0 = (pl.program_id(1) * WIN_QBLOCKS + blk) * QB
        start = pl.multiple_of(jnp.maximum(q0 - WINDOW, 0), QB)

        qs = _stack_heads(q_ref[blk * QB:(blk + 1) * QB, :])
        kaug = jnp.concatenate([kw_ref[pl.ds(start, WSPAN), :], kx_ref[...]], axis=1)
        vaug = jnp.concatenate([vw_ref[pl.ds(start, WSPAN), :], vx_ref[...]], axis=1)
        u = lax.dot_general(_aug_queries(qs, jnp.zeros((QB, LANES), F32), sconst_ref), kaug, _NT,
                            preferred_element_type=F32)
        band = band_ref[lax.shift_right_logical(q0 - start, QB.bit_length() - 1)]
        ps = []
        for h in range(HG):
            uh = u[h * QB:(h + 1) * QB] + band
            m = jnp.max(uh, axis=1, keepdims=True)
            ps.append(jnp.exp2(uh - m).astype(BF16))
        pv = jnp.dot(jnp.concatenate(ps, axis=0), vaug, preferred_element_type=F32)
        _store_heads(o_ref, pv[:, 0:DH] / pv[:, DH:DH + 1], blk)


def _win_attention(qf, projb, band, kxw, vxw, sconst):
    t = qf.shape[0]
    rows = WIN_QBLOCKS * QB
    return pl.pallas_call(
        _win_kernel,
        grid=(N_KV, t // rows),
        in_specs=[pl.BlockSpec((rows, HG * DH), lambda g, i: (i, g)),
                  pl.BlockSpec((t, DH), lambda g, i: (0, 2 * N_KV + g)),
                  pl.BlockSpec((t, DH), lambda g, i: (0, 3 * N_KV + g)),
                  pl.BlockSpec((WINDOW // QB + 1, QB, WSPAN), lambda g, i: (0, 0, 0)),
                  pl.BlockSpec((WSPAN, LANES), lambda g, i: (0, 0)),
                  pl.BlockSpec((WSPAN, LANES), lambda g, i: (0, 0)),
                  pl.BlockSpec((1, HG, LANES), lambda g, i: (g, 0, 0))],
        out_specs=pl.BlockSpec((rows, HG * DH), lambda g, i: (i, g)),
        out_shape=jax.ShapeDtypeStruct((t, NSA_W), F32),
        compiler_params=_params(2),
        name="win_attn",
    )(qf, projb, projb, band, kxw, vxw, sconst)


def _silu(z):
    return z * jax.nn.sigmoid(z)


def _mix_kernel(oc_ref, os_ref, ow_ref, gates_ref, zn_ref, u_ref, v_ref, zg_ref, zm_ref,
                qm_ref, kvm_ref, lng_ref, lnb_ref, wsp_ref, bsp_ref, y_ref):
    sig = jax.nn.sigmoid(gates_ref[...])
    for h in range(N_HEADS):
        sl = slice(h * DH, (h + 1) * DH)
        comb = (sig[:, 3 * h:3 * h + 1] * oc_ref[:, sl]
                + sig[:, 3 * h + 1:3 * h + 2] * os_ref[:, sl]
                + sig[:, 3 * h + 2:3 * h + 3] * ow_ref[:, sl])
        y_ref[:, sl] = (comb * _silu(zn_ref[:, sl])).astype(y_ref.dtype)

    vv = v_ref[...]
    mu = jnp.mean(vv, axis=-1, keepdims=True)
    var = jnp.mean(jnp.square(vv - mu), axis=-1, keepdims=True)
    vn = ((vv - mu) * lax.rsqrt(var + LN_EPS) * lng_ref[...] + lnb_ref[...]).astype(BF16)
    causal = (lax.broadcasted_iota(jnp.int32, (CHUNK, CHUNK), 0)
              >= lax.broadcasted_iota(jnp.int32, (CHUNK, CHUNK), 1))
    for gg in range(GM_GROUPS):
        sl = slice(gg * GM_DG, (gg + 1) * GM_DG)
        ws = jnp.where(causal, wsp_ref[gg], 0.0).astype(BF16)
        sv = jnp.dot(ws, vn[:, sl], preferred_element_type=F32) + bsp_ref[:, gg:gg + 1]
        y_ref[:, NSA_W + gg * GM_DG:NSA_W + (gg + 1) * GM_DG] = (
            u_ref[:, sl] * sv * _silu(zg_ref[:, sl])).astype(y_ref.dtype)

    mscale = np.float32(MEM_DH ** -0.5)
    for h in range(MEM_HEADS):
        sl = slice(h * MEM_DH, (h + 1) * MEM_DH)
        km = kvm_ref[:, h * MEM_DH:(h + 1) * MEM_DH]
        vm = kvm_ref[:, MEM_W + h * MEM_DH:MEM_W + (h + 1) * MEM_DH]
        s = lax.dot_general(qm_ref[:, sl], km, _NT, preferred_element_type=F32) * mscale
        m = jnp.max(s, axis=-1, keepdims=True)
        e = jnp.exp(s - m)
        p = e / jnp.sum(e, axis=-1, keepdims=True)
        o = jnp.dot(p.astype(BF16), vm, preferred_element_type=F32)
        y_ref[:, NSA_W + GM_W + h * MEM_DH:NSA_W + GM_W + (h + 1) * MEM_DH] = (
            o * _silu(zm_ref[:, sl])).astype(y_ref.dtype)


def _mix(oc, osel, ow, gates, zuvz, zm, qm, kvm, lng, lnb, wsp, bsp_t):
    t = oc.shape[0]
    row = lambda c: (lambda i: (i, c))
    full2 = lambda i: (0, 0)
    return pl.pallas_call(
        _mix_kernel,
        grid=(t // QB,),
        in_specs=[pl.BlockSpec((QB, NSA_W), row(0)),
                  pl.BlockSpec((QB, NSA_W), row(0)),
                  pl.BlockSpec((QB, NSA_W), row(0)),
                  pl.BlockSpec((QB, LANES), row(0)),
                  pl.BlockSpec((QB, NSA_W), row(0)),
                  pl.BlockSpec((QB, GM_W), row(2)),
                  pl.BlockSpec((QB, GM_W), row(3)),
                  pl.BlockSpec((QB, GM_W), row(4)),
                  pl.BlockSpec((QB, MEM_W), row(0)),
                  pl.BlockSpec((QB, MEM_W), row(0)),
                  pl.BlockSpec((MEM_LEN, 2 * MEM_W), full2),
                  pl.BlockSpec((1, GM_W), full2),
                  pl.BlockSpec((1, GM_W), full2),
                  pl.BlockSpec((GM_GROUPS, CHUNK, CHUNK), lambda i: (0, 0, 0)),
                  pl.BlockSpec((CHUNK, GM_GROUPS), full2)],
        out_specs=pl.BlockSpec((QB, D_MODEL), row(0)),
        out_shape=jax.ShapeDtypeStruct((t, D_MODEL), BF16),
        compiler_params=_params(1),
        name="mix_heads",
    )(oc, osel, ow, gates, zuvz, zuvz, zuvz, zuvz, zm, qm, kvm, lng, lnb, wsp, bsp_t)


def _ln_kernel(alpha, x_ref, h_ref, g_ref, b_ref, o_ref):
    z = alpha * x_ref[...] + h_ref[...]
    mu = jnp.mean(z, axis=-1, keepdims=True)
    var = jnp.mean(jnp.square(z - mu), axis=-1, keepdims=True)
    o_ref[...] = (z - mu) * lax.rsqrt(var + LN_EPS) * g_ref[...] + b_ref[...]


def _residual_ln(x2, h, g, b, alpha, tm=256):
    t, d = x2.shape
    tm = min(tm, t)
    return pl.pallas_call(
        functools.partial(_ln_kernel, np.float32(alpha)),
        grid=(t // tm,),
        in_specs=[pl.BlockSpec((tm, d), lambda i: (i, 0)),
                  pl.BlockSpec((tm, d), lambda i: (i, 0)),
                  pl.BlockSpec((1, d), lambda i: (0, 0)),
                  pl.BlockSpec((1, d), lambda i: (0, 0))],
        out_specs=pl.BlockSpec((tm, d), lambda i: (i, 0)),
        out_shape=jax.ShapeDtypeStruct((t, d), F32),
        compiler_params=_params(1),
        name="residual_ln",
    )(x2, h, g, b)


def _pool_matrix(nsbp, nch):
    j = np.arange(nsbp)[:, None]
    c = np.arange(nch)[None, :]
    ratio = SLC_BLOCK // CMP_STRIDE
    return ((c >= ratio * j - 1) & (c <= ratio * j + ratio - 1)).astype(np.float32)


def _tile_map(nsbp):
    j = np.arange(nsbp)[:, None]
    n = np.arange(LANES)[None, :]
    return (j // BLK_PER_KT == n).astype(np.float32)


def _block_pick():
    p = np.arange(OH_PERIOD)[:, None, None]
    j = np.arange(LANES)[None, :, None]
    c = np.arange(LANES)[None, None, :]
    return ((c < BLK_PER_KT) & (j == BLK_PER_KT * p + c)).astype(np.float32)


def _key_ext(n_keys, with_blocks):
    l = np.arange(n_keys)[:, None]
    c = np.arange(LANES)[None, :]
    kx = np.zeros((n_keys, LANES), np.float32)
    if with_blocks:
        kx += (c < BLK_PER_KT) & (l // SLC_BLOCK == c)
    kx += np.where((c >= EXT_LO0) & (c < EXT_LO0 + 3), l % 256, 0)
    kx += np.where((c >= EXT_HI0) & (c < EXT_HI0 + 3), l // 256, 0)
    return kx


def _ones_lane0(n_keys):
    vx = np.zeros((n_keys, LANES), np.float32)
    vx[:, 0] = 1.0
    return vx


def _window_band():
    d = np.arange(WINDOW // QB + 1)[:, None, None] * QB
    dist = d + np.arange(QB)[None, :, None] - np.arange(WSPAN)[None, None, :]
    return np.where((dist >= 0) & (dist < WINDOW), 0.0, NEG).astype(np.float32)


def _slope_lanes(cslopes):
    to_bf16 = lambda v: v.astype(BF16).astype(np.float32)
    s1 = to_bf16(cslopes)
    r1 = cslopes - s1
    s2 = to_bf16(r1)
    s3 = to_bf16(r1 - s2)
    pieces = np.stack([s1, s2, s3, 256.0 * s1, 256.0 * s2, 256.0 * s3], axis=-1)
    return np.pad(pieces, ((0, 0), (0, 0), (EXT_LO0, LANES - EXT_LO0 - 6))).astype(np.float32)


def kernel(x, mem, w_in, w_cmp_k1, w_cmp_k2, w_cmp_v1, w_cmp_v2, pe_cmp_k, pe_cmp_v,
           gm_ln_g, gm_ln_b, w_spatial, b_spatial, w_mem_kv, w_out, ln_g, ln_b):
    b, t, _ = x.shape
    assert b == 1 and t % KT == 0 and t >= WSPAN
    alpha = 2.0 ** 0.25
    nch = t // CMP_STRIDE
    nsb = t // SLC_BLOCK
    nsbp = -(-nsb // LANES) * LANES

    gate0 = NSA_W + 6 * KV_W
    w_tail = w_in[:, gate0 + N_GATES:].astype(BF16)
    x2 = x[0]
    xb = x2.astype(BF16)
    qp = _matmul(xb, w_in, BF16, 512, 512, "proj_q", 0, NSA_W, scale=QSCALE)
    kcvc = _matmul(xb, w_in, F32, 512, 512, "proj_kvc", NSA_W, 2 * KV_W)
    kv = _matmul(xb, w_in, BF16, 512, 512, "proj_kv", NSA_W + 2 * KV_W, 4 * KV_W)
    gates = _matmul(xb, w_in, F32, 512, LANES, "proj_gates", gate0, LANES)
    zuvz = _matmul(xb, w_tail, F32, 512, 512, "proj_zuvz", 0, NSA_W + 3 * GM_W)
    qm = _matmul(xb, w_tail, BF16, 512, 512, "proj_qm", NSA_W + 3 * GM_W, MEM_W)
    zm = _matmul(xb, w_tail, F32, 512, 512, "proj_zm", NSA_W + 3 * GM_W + MEM_W, MEM_W)
    kvm = _matmul(mem[0].astype(BF16), w_mem_kv, BF16, MEM_LEN, 512, "mem_kv")

    pe = jnp.stack([pe_cmp_k, pe_cmp_v])
    w1 = jnp.stack([w_cmp_k1, w_cmp_v1]).astype(BF16)
    w2 = jnp.stack([w_cmp_k2, w_cmp_v2]).astype(BF16)
    kvc = _compress(kcvc, pe, w1, w2)

    slopes_np = np.exp2(-8.0 * np.arange(1, N_HEADS + 1, dtype=np.float32) / N_HEADS
                        ).astype(np.float32).reshape(N_KV, HG)
    cslopes_np = (slopes_np * np.float32(np.log2(np.e))).astype(np.float32)
    cslopes = jnp.asarray(cslopes_np)
    sconst = jnp.asarray(_slope_lanes(cslopes_np))
    pool = jnp.asarray(_pool_matrix(nsbp, nch), BF16)

    tmap = jnp.asarray(_tile_map(nsbp), BF16)
    o_c, mb, tf = _cmp_attention(cslopes, qp, kvc[0], kvc[1], pool, tmap)
    tflags = tf[:, :, 0, :t // KT]
    tile_list, tile_cnt = _tile_lists(tflags, t)
    o_s = _sel_attention(tile_list, tile_cnt, cslopes, qp, kv, mb, jnp.asarray(_block_pick(), BF16),
                         jnp.asarray(_key_ext(KT, True), BF16), jnp.asarray(_ones_lane0(KT), BF16), sconst)
    o_w = _win_attention(qp, kv, jnp.asarray(_window_band(), F32),
                         jnp.asarray(_key_ext(WSPAN, False), BF16), jnp.asarray(_ones_lane0(WSPAN), BF16),
                         sconst)

    y = _mix(o_c, o_s, o_w, gates, zuvz, zm, qm, kvm,
             gm_ln_g.reshape(1, GM_W), gm_ln_b.reshape(1, GM_W), w_spatial, b_spatial.T)
    h = _matmul(y, w_out, F32, 512, 512, "out_proj")
    out = _residual_ln(x2, h, ln_g.reshape(1, D_MODEL), ln_b.reshape(1, D_MODEL), alpha)
    return out[None]
```

```python
import functools

import numpy as np
import jax
import jax.numpy as jnp
from jax import lax
from jax.experimental import pallas as pl
from jax.experimental.pallas import tpu as pltpu

D_MODEL = 4096
DH = 128
N_HEADS = 16
HG = 4
N_KV = 4
NSA_W = N_HEADS * DH
KV_W = N_KV * DH
CMP_LEN = 32
CMP_STRIDE = 16
SLC_BLOCK = 64
TOP_N = 16
WINDOW = 512
QB = 128
CHUNK = 128
GM_W = 1024
GM_GROUPS = 4
GM_DG = GM_W // GM_GROUPS
MEM_W = 1024
MEM_LEN = 256
MEM_HEADS = 4
MEM_DH = MEM_W // MEM_HEADS
LN_EPS = 1e-5
N_GATES = N_HEADS * 3

LANES = 128
KT = 512
BLK_PER_KT = KT // SLC_BLOCK
OH_PERIOD = LANES // BLK_PER_KT
NEG = -2.0 ** 100
PROJ_TM = 1024
PROJ_TN = 1024
VMEM_LIMIT = 48 * 1024 * 1024

F32 = jnp.float32
BF16 = jnp.bfloat16
_NT = (((1,), (1,)), ((), ()))


def _params(n_axes):
    return pltpu.CompilerParams(dimension_semantics=("arbitrary",) * n_axes,
                                vmem_limit_bytes=VMEM_LIMIT)


def _mm_kernel(scale, a_ref, b_ref, o_ref):
    acc = jnp.dot(a_ref[...], b_ref[...], preferred_element_type=F32)
    if scale is not None:
        acc = acc * scale
    o_ref[...] = acc.astype(o_ref.dtype)


def _matmul(a, b, out_dtype, tm, tn, name, col0=0, ncols=None, scale=None):
    m, k = a.shape
    ncols = b.shape[1] - col0 if ncols is None else ncols
    tm = min(tm, m)
    assert col0 % tn == 0 and ncols % tn == 0 and m % tm == 0
    jb = col0 // tn
    return pl.pallas_call(
        functools.partial(_mm_kernel, scale),
        grid=(ncols // tn, m // tm),
        in_specs=[pl.BlockSpec((tm, k), lambda j, i: (i, 0)),
                  pl.BlockSpec((k, tn), lambda j, i: (0, jb + j))],
        out_specs=pl.BlockSpec((tm, tn), lambda j, i: (i, j)),
        out_shape=jax.ShapeDtypeStruct((m, ncols), out_dtype),
        compiler_params=_params(2),
        name=name,
    )(a, b)


def _gelu_tanh(x):
    c = np.float32(np.sqrt(2.0 / np.pi))
    return x * (0.5 * (1.0 + jnp.tanh(c * (x + 0.044715 * (x * x * x)))))


def _compress_kernel(x_ref, pe_ref, w1_ref, w2_ref, o_ref):
    nch = o_ref.shape[2]
    a = jnp.zeros((nch, DH), F32)
    b = jnp.zeros((nch, DH), F32)
    for r in range(CMP_STRIDE):
        xr = x_ref[pl.ds(r, nch, stride=CMP_STRIDE), :]
        lo = (xr + pe_ref[0, r:r + 1, :]).astype(BF16)
        hi = (xr + pe_ref[0, CMP_STRIDE + r:CMP_STRIDE + r + 1, :]).astype(BF16)
        a = a + jnp.dot(lo, w1_ref[0, r * DH:(r + 1) * DH, :], preferred_element_type=F32)
        b = b + jnp.dot(hi, w1_ref[0, (CMP_STRIDE + r) * DH:(CMP_STRIDE + r + 1) * DH, :],
                        preferred_element_type=F32)
    pre = a + pltpu.roll(b, nch - 1, axis=0)
    h = _gelu_tanh(pre).astype(BF16)
    o_ref[0, 0] = jnp.dot(h, w2_ref[0], preferred_element_type=F32).astype(o_ref.dtype)


def _compress(kcvc, pe, w1, w2):
    t = kcvc.shape[0]
    nch = t // CMP_STRIDE
    return pl.pallas_call(
        _compress_kernel,
        grid=(2, N_KV),
        in_specs=[pl.BlockSpec((t, DH), lambda s, j: (0, s * N_KV + j)),
                  pl.BlockSpec((1, CMP_LEN, DH), lambda s, j: (s, 0, 0)),
                  pl.BlockSpec((1, CMP_LEN * DH, DH), lambda s, j: (s, 0, 0)),
                  pl.BlockSpec((1, DH, DH), lambda s, j: (s, 0, 0))],
        out_specs=pl.BlockSpec((1, 1, nch, DH), lambda s, j: (s, j, 0, 0)),
        out_shape=jax.ShapeDtypeStruct((2, N_KV, nch, DH), BF16),
        compiler_params=_params(2),
        name="compress_kv",
    )(kcvc, pe, w1, w2)


def _stack_heads(q):
    return jnp.concatenate([q[:, h * DH:(h + 1) * DH] for h in range(HG)], axis=0)


def _store_heads(o_ref, o, blk=0):
    for h in range(HG):
        o_ref[blk * QB:(blk + 1) * QB, h * DH:(h + 1) * DH] = o[h * QB:(h + 1) * QB, :]


CMP_QBLOCKS = 2
CMP_COL_STEP = 256
FORCED_BLOCKS = 3
QSCALE = np.float32(DH ** -0.5 * np.log2(np.e))


def _cmp_block(g, i, w, r, cs_ref, q, kc_ref, vc_ref, pool_ref, tmap_ref, oc_ref, mb_ref, tf_ref, blk):
    q0 = i * QB
    rows_out = slice(blk * QB, (blk + 1) * QB)
    nsbp = mb_ref.shape[1]
    n_sel = min(TOP_N, r)

    qs = _stack_heads(q)
    s = lax.dot_general(qs, kc_ref[0, 0:w, :], _NT, preferred_element_type=F32)
    cpos = lax.broadcasted_iota(jnp.int32, (1, w), 1) * CMP_STRIDE + (CMP_LEN - 1) - q0
    row = lax.broadcasted_iota(jnp.int32, (QB, 1), 0)
    maskb = jnp.where(cpos <= row, 0.0, NEG)
    cposf = cpos.astype(F32)
    row_ok = jnp.where(q0 + row >= CMP_LEN - 1, 1.0, 0.0)

    vc = vc_ref[0, 0:w, :]
    imp = jnp.zeros((QB, w), F32)
    for h in range(HG):
        t = s[h * QB:(h + 1) * QB] + cs_ref[g, h] * cposf + maskb
        m = jnp.max(t, axis=1, keepdims=True)
        p = jnp.exp2(t - m)
        l = jnp.sum(p, axis=1, keepdims=True)
        pn = p * (row_ok / jnp.maximum(l, 1e-30))
        imp = imp + pn
        oc_ref[rows_out, h * DH:(h + 1) * DH] = jnp.dot(pn.astype(BF16), vc, preferred_element_type=F32)

    pool = pool_ref[0:r, 0:w]
    p1 = imp.astype(BF16)
    r1 = imp - p1.astype(F32)
    p2 = r1.astype(BF16)
    p3 = (r1 - p2.astype(F32)).astype(BF16)
    imps = (lax.dot_general(pool, p1, _NT, preferred_element_type=F32)
            + lax.dot_general(pool, p2, _NT, preferred_element_type=F32)
            + lax.dot_general(pool, p3, _NT, preferred_element_type=F32))

    j = lax.broadcasted_iota(jnp.int32, (r, QB), 0)
    tq = q0 + lax.broadcasted_iota(jnp.int32, (r, QB), 1)
    cur = lax.shift_right_arithmetic(tq, SLC_BLOCK.bit_length() - 1)
    forced = jnp.where(j == 0, 1, jnp.where(j == cur, 1, jnp.where(j == cur - 1, 1, 0)))
    sc = jnp.where(j > cur, -1.0, jnp.where(forced == 1, -2.0, imps))
    for _ in range(n_sel - FORCED_BLOCKS):
        mx = jnp.max(sc, axis=0, keepdims=True)
        idx = jnp.min(jnp.where(sc == mx, j, r), axis=0, keepdims=True)
        sc = jnp.where(j == idx, -2.0, sc)
    bias = jnp.where(j <= cur, jnp.where(sc == -2.0, 0.0, NEG), NEG)
    bias_t = bias.T
    mb_ref[rows_out, 0:r] = bias_t.astype(BF16)
    if r < nsbp:
        mb_ref[rows_out, r:nsbp] = jnp.full((QB, nsbp - r), NEG, BF16)
    anyq = jnp.max(jnp.where(bias_t == 0.0, 1.0, 0.0), axis=0, keepdims=True)
    hit = jnp.dot(jnp.broadcast_to(anyq, (8, r)).astype(BF16), tmap_ref[0:r, :],
                  preferred_element_type=F32)
    tf_ref[0, blk] = jnp.where(hit > 0.0, 1, 0).astype(jnp.int32)


def _cmp_kernel(cs_ref, q_ref, kc_ref, vc_ref, pool_ref, tmap_ref, oc_ref, mb_ref, tf_ref):
    g = pl.program_id(0)
    i0 = pl.program_id(1) * CMP_QBLOCKS
    nch = kc_ref.shape[1]
    nsbp = pool_ref.shape[0]
    step = min(CMP_COL_STEP, nch)
    need = (i0 + CMP_QBLOCKS - 1) * (QB // CMP_STRIDE) + (QB - CMP_LEN) // CMP_STRIDE
    variant = lax.shift_right_logical(need, step.bit_length() - 1)
    for v in range(nch // step):
        w = (v + 1) * step
        r = min(nsbp, -(-(w // (SLC_BLOCK // CMP_STRIDE)) // LANES) * LANES)

        @pl.when(variant == v)
        def _(w=w, r=r):
            for blk in range(CMP_QBLOCKS):
                _cmp_block(g, i0 + blk, w, r, cs_ref, q_ref[blk * QB:(blk + 1) * QB, :], kc_ref, vc_ref,
                           pool_ref, tmap_ref, oc_ref, mb_ref, tf_ref, blk)


def _cmp_attention(cslopes, qf, kc, vc, pool, tmap):
    t = qf.shape[0]
    nch = kc.shape[1]
    nsbp = pool.shape[0]
    rows = CMP_QBLOCKS * QB
    return pl.pallas_call(
        _cmp_kernel,
        grid=(N_KV, t // rows),
        in_specs=[pl.BlockSpec(memory_space=pltpu.SMEM),
                  pl.BlockSpec((rows, HG * DH), lambda g, i: (i, g)),
                  pl.BlockSpec((1, nch, DH), lambda g, i: (g, 0, 0)),
                  pl.BlockSpec((1, nch, DH), lambda g, i: (g, 0, 0)),
                  pl.BlockSpec((nsbp, nch), lambda g, i: (0, 0)),
                  pl.BlockSpec((nsbp, LANES), lambda g, i: (0, 0))],
        out_specs=[pl.BlockSpec((rows, HG * DH), lambda g, i: (i, g)),
                   pl.BlockSpec((rows, nsbp), lambda g, i: (i, g)),
                   pl.BlockSpec((1, CMP_QBLOCKS, 8, LANES), lambda g, i: (g, i, 0, 0))],
        out_shape=[jax.ShapeDtypeStruct((t, NSA_W), F32),
                   jax.ShapeDtypeStruct((t, N_KV * nsbp), BF16),
                   jax.ShapeDtypeStruct((N_KV, t // QB, 8, LANES), jnp.int32)],
        compiler_params=_params(2),
        name="cmp_attn_select",
    )(cslopes, qf, kc, vc, pool, tmap)


EXT_LO0 = BLK_PER_KT
EXT_HI0 = BLK_PER_KT + 3


def _aug_queries(qs, ext, sconst_ref):
    parts = []
    for h in range(HG):
        e = (ext + sconst_ref[0, h:h + 1, :]).astype(BF16)
        parts.append(jnp.concatenate([qs[h * QB:(h + 1) * QB], e], axis=1))
    return jnp.concatenate(parts, axis=0)


def _sel_kernel(lst_ref, cnt_ref, cs_ref, q_ref, ks_ref, vs_ref, mb_ref, pick_ref, kx_ref, vx_ref,
                sconst_ref, o_ref, m_scr, acc_scr, u_scr, p_scr, al_scr):
    g = pl.program_id(0)
    i = pl.program_id(1)
    q0 = i * QB
    nhalf = mb_ref.shape[1] // LANES
    n_prev = cnt_ref[g, i]

    qs = _stack_heads(q_ref[...])
    m_scr[...] = jnp.full(m_scr.shape, NEG, F32)
    acc_scr[...] = jnp.zeros(acc_scr.shape, F32)
    p_scr[1] = jnp.zeros(p_scr.shape[1:], BF16)
    al_scr[1] = jnp.ones(al_scr.shape[1:], F32)

    def score_stage(n, slot):
        tn = lst_ref[g, i, n]
        k0 = pl.multiple_of(tn * KT, KT)
        kaug = jnp.concatenate([ks_ref[pl.ds(k0, KT), :], kx_ref[...]], axis=1)
        half = lax.shift_right_logical(tn, OH_PERIOD.bit_length() - 1)
        mbh = mb_ref[:, 0:LANES]
        for hf in range(1, nhalf):
            mbh = jnp.where(half == hf, mb_ref[:, hf * LANES:(hf + 1) * LANES], mbh)
        ext = jnp.dot(mbh, pick_ref[tn & (OH_PERIOD - 1)], preferred_element_type=F32)
        u_scr[slot] = lax.dot_general(_aug_queries(qs, ext, sconst_ref), kaug, _NT,
                                      preferred_element_type=F32)

    def softmax_stage(n, slot, causal):
        k0 = lst_ref[g, i, n] * KT
        dq = (q0 - k0).astype(F32)
        if causal:
            visible = (lax.broadcasted_iota(jnp.int32, (QB, KT), 0)
                       - lax.broadcasted_iota(jnp.int32, (QB, KT), 1) + (q0 - k0)) >= 0
        for h in range(HG):
            rows = pl.ds(h * QB, QB)
            uh = u_scr[slot, rows, :]
            if causal:
                uh = jnp.where(visible, uh, NEG)
            c = cs_ref[g, h] * dq
            m_prev = m_scr[rows, :]
            m_next = jnp.maximum(m_prev, jnp.max(uh, axis=1, keepdims=True) - c)
            p_scr[slot, rows, :] = jnp.exp2(uh - (m_next + c)).astype(BF16)
            al_scr[slot, rows, :] = jnp.exp2(m_prev - m_next)
            m_scr[rows, :] = m_next

    def pv_stage(n, slot):
        k0 = pl.multiple_of(lst_ref[g, i, jnp.maximum(n, 0)] * KT, KT)
        vaug = jnp.concatenate([vs_ref[pl.ds(k0, KT), :], vx_ref[...]], axis=1)
        pv = jnp.dot(p_scr[slot], vaug, preferred_element_type=F32)
        acc_scr[...] = al_scr[slot] * acc_scr[...] + pv

    def finish():
        acc = acc_scr[...]
        _store_heads(o_ref, acc[:, 0:DH] / acc[:, DH:DH + 1])

    score_stage(0, 0)

    def pair(j, carry):
        n = 2 * j
        score_stage(n + 1, 1)
        softmax_stage(n, 0, False)
        pv_stage(n - 1, 1)
        score_stage(n + 2, 0)
        softmax_stage(n + 1, 1, False)
        pv_stage(n, 0)
        return carry

    lax.fori_loop(0, lax.shift_right_logical(n_prev, 1), pair, 0)
    n_even = n_prev & ~1

    @pl.when(n_prev == n_even)
    def _():
        softmax_stage(n_even, 0, True)
        pv_stage(n_even - 1, 1)
        pv_stage(n_even, 0)
        finish()

    @pl.when(n_prev != n_even)
    def _():
        score_stage(n_even + 1, 1)
        softmax_stage(n_even, 0, False)
        pv_stage(n_even - 1, 1)
        softmax_stage(n_even + 1, 1, True)
        pv_stage(n_even, 0)
        pv_stage(n_even + 1, 1)
        finish()


def _sel_attention(tile_list, tile_cnt, cslopes, qf, projb, mb, pick, kx, vx, sconst):
    t = qf.shape[0]
    nsbp = mb.shape[1] // N_KV
    grid_spec = pltpu.PrefetchScalarGridSpec(
        num_scalar_prefetch=2,
        grid=(N_KV, t // QB),
        in_specs=[pl.BlockSpec(memory_space=pltpu.SMEM),
                  pl.BlockSpec((QB, HG * DH), lambda g, i, *_: (i, g)),
                  pl.BlockSpec((t, DH), lambda g, i, *_: (0, g)),
                  pl.BlockSpec((t, DH), lambda g, i, *_: (0, N_KV + g)),
                  pl.BlockSpec((QB, nsbp), lambda g, i, *_: (i, g)),
                  pl.BlockSpec((OH_PERIOD, LANES, LANES), lambda g, i, *_: (0, 0, 0)),
                  pl.BlockSpec((KT, LANES), lambda g, i, *_: (0, 0)),
                  pl.BlockSpec((KT, LANES), lambda g, i, *_: (0, 0)),
                  pl.BlockSpec((1, HG, LANES), lambda g, i, *_: (g, 0, 0))],
        out_specs=pl.BlockSpec((QB, HG * DH), lambda g, i, *_: (i, g)),
        scratch_shapes=[pltpu.VMEM((HG * QB, 1), F32),
                        pltpu.VMEM((HG * QB, 2 * LANES), F32),
                        pltpu.VMEM((2, HG * QB, KT), F32),
                        pltpu.VMEM((2, HG * QB, KT), BF16),
                        pltpu.VMEM((2, HG * QB, 1), F32)])
    return pl.pallas_call(
        _sel_kernel,
        grid_spec=grid_spec,
        out_shape=jax.ShapeDtypeStruct((t, NSA_W), F32),
        compiler_params=_params(2),
        name="sel_attn",
    )(tile_list, tile_cnt, cslopes, qf, projb, projb, mb, pick, kx, vx, sconst)


def _tile_lists(tflags, t):
    nt = t // KT
    last = (jnp.arange(t // QB, dtype=jnp.int32) * QB + QB - 1) // KT
    ids = jnp.arange(nt, dtype=jnp.int32)
    active = (tflags != 0) & (ids[None, None, :] < last[None, :, None])
    cnt = jnp.sum(active, axis=-1).astype(jnp.int32)
    order = jnp.argsort(jnp.where(active, ids, nt + ids), axis=-1).astype(jnp.int32)
    lst = jnp.where(ids[None, None, :] == cnt[..., None], last[None, :, None], order)
    return lst, cnt


WSPAN = WINDOW + QB


WIN_QBLOCKS = 2


def _win_kernel(q_ref, kw_ref, vw_ref, band_ref, kx_ref, vx_ref, sconst_ref, o_ref):
    for blk in range(WIN_QBLOCKS):
        q0 = (pl.program_id(1) * WIN_QBLOCKS + blk) * QB
        start = pl.multiple_of(jnp.maximum(q0 - WINDOW, 0), QB)

        qs = _stack_heads(q_ref[blk * QB:(blk + 1) * QB, :])
        kaug = jnp.concatenate([kw_ref[pl.ds(start, WSPAN), :], kx_ref[...]], axis=1)
        vaug = jnp.concatenate([vw_ref[pl.ds(start, WSPAN), :], vx_ref[...]], axis=1)
        u = lax.dot_general(_aug_queries(qs, jnp.zeros((QB, LANES), F32), sconst_ref), kaug, _NT,
                            preferred_element_type=F32)
        band = band_ref[lax.shift_right_logical(q0 - start, QB.bit_length() - 1)]
        ps = []
        for h in range(HG):
            uh = u[h * QB:(h + 1) * QB] + band
            m = jnp.max(uh, axis=1, keepdims=True)
            ps.append(jnp.exp2(uh - m).astype(BF16))
        pv = jnp.dot(jnp.concatenate(ps, axis=0), vaug, preferred_element_type=F32)
        _store_heads(o_ref, pv[:, 0:DH] / pv[:, DH:DH + 1], blk)


def _win_attention(qf, projb, band, kxw, vxw, sconst):
    t = qf.shape[0]
    rows = WIN_QBLOCKS * QB
    return pl.pallas_call(
        _win_kernel,
        grid=(N_KV, t // rows),
        in_specs=[pl.BlockSpec((rows, HG * DH), lambda g, i: (i, g)),
                  pl.BlockSpec((t, DH), lambda g, i: (0, 2 * N_KV + g)),
                  pl.BlockSpec((t, DH), lambda g, i: (0, 3 * N_KV + g)),
                  pl.BlockSpec((WINDOW // QB + 1, QB, WSPAN), lambda g, i: (0, 0, 0)),
                  pl.BlockSpec((WSPAN, LANES), lambda g, i: (0, 0)),
                  pl.BlockSpec((WSPAN, LANES), lambda g, i: (0, 0)),
                  pl.BlockSpec((1, HG, LANES), lambda g, i: (g, 0, 0))],
        out_specs=pl.BlockSpec((rows, HG * DH), lambda g, i: (i, g)),
        out_shape=jax.ShapeDtypeStruct((t, NSA_W), F32),
        compiler_params=_params(2),
        name="win_attn",
    )(qf, projb, projb, band, kxw, vxw, sconst)


def _silu(z):
    return z * jax.nn.sigmoid(z)


def _mix_kernel(oc_ref, os_ref, ow_ref, gates_ref, zn_ref, u_ref, v_ref, zg_ref, zm_ref,
                qm_ref, kvm_ref, lng_ref, lnb_ref, wsp_ref, bsp_ref, y_ref):
    sig = jax.nn.sigmoid(gates_ref[...])
    for h in range(N_HEADS):
        sl = slice(h * DH, (h + 1) * DH)
        comb = (sig[:, 3 * h:3 * h + 1] * oc_ref[:, sl]
                + sig[:, 3 * h + 1:3 * h + 2] * os_ref[:, sl]
                + sig[:, 3 * h + 2:3 * h + 3] * ow_ref[:, sl])
        y_ref[:, sl] = (comb * _silu(zn_ref[:, sl])).astype(y_ref.dtype)

    vv = v_ref[...]
    mu = jnp.mean(vv, axis=-1, keepdims=True)
    var = jnp.mean(jnp.square(vv - mu), axis=-1, keepdims=True)
    vn = ((vv - mu) * lax.rsqrt(var + LN_EPS) * lng_ref[...] + lnb_ref[...]).astype(BF16)
    causal = (lax.broadcasted_iota(jnp.int32, (CHUNK, CHUNK), 0)
              >= lax.broadcasted_iota(jnp.int32, (CHUNK, CHUNK), 1))
    for gg in range(GM_GROUPS):
        sl = slice(gg * GM_DG, (gg + 1) * GM_DG)
        ws = jnp.where(causal, wsp_ref[gg], 0.0).astype(BF16)
        sv = jnp.dot(ws, vn[:, sl], preferred_element_type=F32) + bsp_ref[:, gg:gg + 1]
        y_ref[:, NSA_W + gg * GM_DG:NSA_W + (gg + 1) * GM_DG] = (
            u_ref[:, sl] * sv * _silu(zg_ref[:, sl])).astype(y_ref.dtype)

    mscale = np.float32(MEM_DH ** -0.5)
    for h in range(MEM_HEADS):
        sl = slice(h * MEM_DH, (h + 1) * MEM_DH)
        km = kvm_ref[:, h * MEM_DH:(h + 1) * MEM_DH]
        vm = kvm_ref[:, MEM_W + h * MEM_DH:MEM_W + (h + 1) * MEM_DH]
        s = lax.dot_general(qm_ref[:, sl], km, _NT, preferred_element_type=F32) * mscale
        m = jnp.max(s, axis=-1, keepdims=True)
        e = jnp.exp(s - m)
        p = e / jnp.sum(e, axis=-1, keepdims=True)
        o = jnp.dot(p.astype(BF16), vm, preferred_element_type=F32)
        y_ref[:, NSA_W + GM_W + h * MEM_DH:NSA_W + GM_W + (h + 1) * MEM_DH] = (
            o * _silu(zm_ref[:, sl])).astype(y_ref.dtype)


def _mix(oc, osel, ow, gates, zuvz, zm, qm, kvm, lng, lnb, wsp, bsp_t):
    t = oc.shape[0]
    row = lambda c: (lambda i: (i, c))
    full2 = lambda i: (0, 0)
    return pl.pallas_call(
        _mix_kernel,
        grid=(t // QB,),
        in_specs=[pl.BlockSpec((QB, NSA_W), row(0)),
                  pl.BlockSpec((QB, NSA_W), row(0)),
                  pl.BlockSpec((QB, NSA_W), row(0)),
                  pl.BlockSpec((QB, LANES), row(0)),
                  pl.BlockSpec((QB, NSA_W), row(0)),
                  pl.BlockSpec((QB, GM_W), row(2)),
                  pl.BlockSpec((QB, GM_W), row(3)),
                  pl.BlockSpec((QB, GM_W), row(4)),
                  pl.BlockSpec((QB, MEM_W), row(0)),
                  pl.BlockSpec((QB, MEM_W), row(0)),
                  pl.BlockSpec((MEM_LEN, 2 * MEM_W), full2),
                  pl.BlockSpec((1, GM_W), full2),
                  pl.BlockSpec((1, GM_W), full2),
                  pl.BlockSpec((GM_GROUPS, CHUNK, CHUNK), lambda i: (0, 0, 0)),
                  pl.BlockSpec((CHUNK, GM_GROUPS), full2)],
        out_specs=pl.BlockSpec((QB, D_MODEL), row(0)),
        out_shape=jax.ShapeDtypeStruct((t, D_MODEL), BF16),
        compiler_params=_params(1),
        name="mix_heads",
    )(oc, osel, ow, gates, zuvz, zuvz, zuvz, zuvz, zm, qm, kvm, lng, lnb, wsp, bsp_t)


def _ln_kernel(alpha, x_ref, h_ref, g_ref, b_ref, o_ref):
    z = alpha * x_ref[...] + h_ref[...]
    mu = jnp.mean(z, axis=-1, keepdims=True)
    var = jnp.mean(jnp.square(z - mu), axis=-1, keepdims=True)
    o_ref[...] = (z - mu) * lax.rsqrt(var + LN_EPS) * g_ref[...] + b_ref[...]


def _residual_ln(x2, h, g, b, alpha, tm=256):
    t, d = x2.shape
    tm = min(tm, t)
    return pl.pallas_call(
        functools.partial(_ln_kernel, np.float32(alpha)),
        grid=(t // tm,),
        in_specs=[pl.BlockSpec((tm, d), lambda i: (i, 0)),
                  pl.BlockSpec((tm, d), lambda i: (i, 0)),
                  pl.BlockSpec((1, d), lambda i: (0, 0)),
                  pl.BlockSpec((1, d), lambda i: (0, 0))],
        out_specs=pl.BlockSpec((tm, d), lambda i: (i, 0)),
        out_shape=jax.ShapeDtypeStruct((t, d), F32),
        compiler_params=_params(1),
        name="residual_ln",
    )(x2, h, g, b)


def _pool_matrix(nsbp, nch):
    j = np.arange(nsbp)[:, None]
    c = np.arange(nch)[None, :]
    ratio = SLC_BLOCK // CMP_STRIDE
    return ((c >= ratio * j - 1) & (c <= ratio * j + ratio - 1)).astype(np.float32)


def _tile_map(nsbp):
    j = np.arange(nsbp)[:, None]
    n = np.arange(LANES)[None, :]
    return (j // BLK_PER_KT == n).astype(np.float32)


def _block_pick():
    p = np.arange(OH_PERIOD)[:, None, None]
    j = np.arange(LANES)[None, :, None]
    c = np.arange(LANES)[None, None, :]
    return ((c < BLK_PER_KT) & (j == BLK_PER_KT * p + c)).astype(np.float32)


def _key_ext(n_keys, with_blocks):
    l = np.arange(n_keys)[:, None]
    c = np.arange(LANES)[None, :]
    kx = np.zeros((n_keys, LANES), np.float32)
    if with_blocks:
        kx += (c < BLK_PER_KT) & (l // SLC_BLOCK == c)
    kx += np.where((c >= EXT_LO0) & (c < EXT_LO0 + 3), l % 256, 0)
    kx += np.where((c >= EXT_HI0) & (c < EXT_HI0 + 3), l // 256, 0)
    return kx


def _ones_lane0(n_keys):
    vx = np.zeros((n_keys, LANES), np.float32)
    vx[:, 0] = 1.0
    return vx


def _window_band():
    d = np.arange(WINDOW // QB + 1)[:, None, None] * QB
    dist = d + np.arange(QB)[None, :, None] - np.arange(WSPAN)[None, None, :]
    return np.where((dist >= 0) & (dist < WINDOW), 0.0, NEG).astype(np.float32)


def _slope_lanes(cslopes):
    to_bf16 = lambda v: v.astype(BF16).astype(np.float32)
    s1 = to_bf16(cslopes)
    r1 = cslopes - s1
    s2 = to_bf16(r1)
    s3 = to_bf16(r1 - s2)
    pieces = np.stack([s1, s2, s3, 256.0 * s1, 256.0 * s2, 256.0 * s3], axis=-1)
    return np.pad(pieces, ((0, 0), (0, 0), (EXT_LO0, LANES - EXT_LO0 - 6))).astype(np.float32)


def kernel(x, mem, w_in, w_cmp_k1, w_cmp_k2, w_cmp_v1, w_cmp_v2, pe_cmp_k, pe_cmp_v,
           gm_ln_g, gm_ln_b, w_spatial, b_spatial, w_mem_kv, w_out, ln_g, ln_b):
    b, t, _ = x.shape
    assert b == 1 and t % KT == 0 and t >= WSPAN
    alpha = 2.0 ** 0.25
    nch = t // CMP_STRIDE
    nsb = t // SLC_BLOCK
    nsbp = -(-nsb // LANES) * LANES

    gate0 = NSA_W + 6 * KV_W
    w_head = w_in[:, :gate0 + LANES].astype(BF16)
    w_tail = w_in[:, gate0 + N_GATES:].astype(BF16)
    x2 = x[0]
    xb = x2.astype(BF16)
    tm, tn = PROJ_TM, PROJ_TN
    qp = _matmul(xb, w_head, BF16, tm, tn, "proj_q", 0, NSA_W, scale=QSCALE)
    kcvc = _matmul(xb, w_head, F32, tm, tn, "proj_kvc", NSA_W, 2 * KV_W)
    kv = _matmul(xb, w_head, BF16, tm, tn, "proj_kv", NSA_W + 2 * KV_W, 4 * KV_W)
    gates = _matmul(xb, w_head, F32, tm, LANES, "proj_gates", gate0, LANES)
    zuvz = _matmul(xb, w_tail, F32, tm, tn, "proj_zuvz", 0, NSA_W + 3 * GM_W)
    qm = _matmul(xb, w_tail, BF16, tm, tn, "proj_qm", NSA_W + 3 * GM_W, MEM_W)
    zm = _matmul(xb, w_tail, F32, tm, tn, "proj_zm", NSA_W + 3 * GM_W + MEM_W, MEM_W)
    kvm = _matmul(mem[0].astype(BF16), w_mem_kv.astype(BF16), BF16, MEM_LEN, tn, "mem_kv")

    pe = jnp.stack([pe_cmp_k, pe_cmp_v])
    w1 = jnp.stack([w_cmp_k1, w_cmp_v1]).astype(BF16)
    w2 = jnp.stack([w_cmp_k2, w_cmp_v2]).astype(BF16)
    kvc = _compress(kcvc, pe, w1, w2)

    slopes_np = np.exp2(-8.0 * np.arange(1, N_HEADS + 1, dtype=np.float32) / N_HEADS
                        ).astype(np.float32).reshape(N_KV, HG)
    cslopes_np = (slopes_np * np.float32(np.log2(np.e))).astype(np.float32)
    cslopes = jnp.asarray(cslopes_np)
    sconst = jnp.asarray(_slope_lanes(cslopes_np))
    pool = jnp.asarray(_pool_matrix(nsbp, nch), BF16)

    tmap = jnp.asarray(_tile_map(nsbp), BF16)
    o_c, mb, tf = _cmp_attention(cslopes, qp, kvc[0], kvc[1], pool, tmap)
    tflags = tf[:, :, 0, :t // KT]
    tile_list, tile_cnt = _tile_lists(tflags, t)
    o_s = _sel_attention(tile_list, tile_cnt, cslopes, qp, kv, mb, jnp.asarray(_block_pick(), BF16),
                         jnp.asarray(_key_ext(KT, True), BF16), jnp.asarray(_ones_lane0(KT), BF16), sconst)
    o_w = _win_attention(qp, kv, jnp.asarray(_window_band(), F32),
                         jnp.asarray(_key_ext(WSPAN, False), BF16), jnp.asarray(_ones_lane0(WSPAN), BF16),
                         sconst)

    y = _mix(o_c, o_s, o_w, gates, zuvz, zm, qm, kvm,
             gm_ln_g.reshape(1, GM_W), gm_ln_b.reshape(1, GM_W), w_spatial, b_spatial.T)
    h = _matmul(y, w_out.astype(BF16), F32, PROJ_TM, PROJ_TN, "out_proj")
    out = _residual_ln(x2, h, ln_g.reshape(1, D_MODEL), ln_b.reshape(1, D_MODEL), alpha)
    return out[None]
```

```python
import functools

import numpy as np
import jax
import jax.numpy as jnp
from jax import lax
from jax.experimental import pallas as pl
from jax.experimental.pallas import tpu as pltpu

D_MODEL = 4096
DH = 128
N_HEADS = 16
HG = 4
N_KV = 4
NSA_W = N_HEADS * DH
KV_W = N_KV * DH
CMP_LEN = 32
CMP_STRIDE = 16
SLC_BLOCK = 64
TOP_N = 16
WINDOW = 512
QB = 128
CHUNK = 128
GM_W = 1024
GM_GROUPS = 4
GM_DG = GM_W // GM_GROUPS
MEM_W = 1024
MEM_LEN = 256
MEM_HEADS = 4
MEM_DH = MEM_W // MEM_HEADS
LN_EPS = 1e-5
N_GATES = N_HEADS * 3

LANES = 128
KT = 512
BLK_PER_KT = KT // SLC_BLOCK
OH_PERIOD = LANES // BLK_PER_KT
NEG = -2.0 ** 100
PROJ_TM = 1024
PROJ_TN = 1024
VMEM_LIMIT = 48 * 1024 * 1024

F32 = jnp.float32
BF16 = jnp.bfloat16
_NT = (((1,), (1,)), ((), ()))


def _params(n_axes):
    return pltpu.CompilerParams(dimension_semantics=("arbitrary",) * n_axes,
                                vmem_limit_bytes=VMEM_LIMIT)


def _mm_kernel(scale, trans_b, a_ref, b_ref, o_ref):
    if trans_b:
        acc = lax.dot_general(a_ref[...], b_ref[...], _NT, preferred_element_type=F32)
    else:
        acc = jnp.dot(a_ref[...], b_ref[...], preferred_element_type=F32)
    if scale is not None:
        acc = acc * scale
    o_ref[...] = acc.astype(o_ref.dtype)


def _matmul(a, b, out_dtype, tm, tn, name, col0=0, ncols=None, scale=None, trans_b=False):
    m, k = a.shape
    n_all = b.shape[0] if trans_b else b.shape[1]
    ncols = n_all - col0 if ncols is None else ncols
    tm = min(tm, m)
    assert ncols % tn == 0 and m % tm == 0
    if trans_b:
        assert col0 % 16 == 0
        b_spec = pl.BlockSpec((pl.Element(tn), pl.Element(k)),
                              lambda j, i: (pl.multiple_of(col0 + j * tn, 16), 0))
    else:
        assert col0 % tn == 0
        b_spec = pl.BlockSpec((k, tn), lambda j, i: (0, col0 // tn + j))
    return pl.pallas_call(
        functools.partial(_mm_kernel, scale, trans_b),
        grid=(ncols // tn, m // tm),
        in_specs=[pl.BlockSpec((tm, k), lambda j, i: (i, 0)), b_spec],
        out_specs=pl.BlockSpec((tm, tn), lambda j, i: (i, j)),
        out_shape=jax.ShapeDtypeStruct((m, ncols), out_dtype),
        compiler_params=_params(2),
        name=name,
    )(a, b)


def _gelu_tanh(x):
    c = np.float32(np.sqrt(2.0 / np.pi))
    return x * (0.5 * (1.0 + jnp.tanh(c * (x + 0.044715 * (x * x * x)))))


def _compress_kernel(x_ref, pe_ref, w1_ref, w2_ref, o_ref):
    nch = o_ref.shape[2]
    a = jnp.zeros((nch, DH), F32)
    b = jnp.zeros((nch, DH), F32)
    for r in range(CMP_STRIDE):
        xr = x_ref[pl.ds(r, nch, stride=CMP_STRIDE), :]
        lo = (xr + pe_ref[0, r:r + 1, :]).astype(BF16)
        hi = (xr + pe_ref[0, CMP_STRIDE + r:CMP_STRIDE + r + 1, :]).astype(BF16)
        a = a + jnp.dot(lo, w1_ref[0, r * DH:(r + 1) * DH, :], preferred_element_type=F32)
        b = b + jnp.dot(hi, w1_ref[0, (CMP_STRIDE + r) * DH:(CMP_STRIDE + r + 1) * DH, :],
                        preferred_element_type=F32)
    pre = a + pltpu.roll(b, nch - 1, axis=0)
    h = _gelu_tanh(pre).astype(BF16)
    o_ref[0, 0] = jnp.dot(h, w2_ref[0], preferred_element_type=F32).astype(o_ref.dtype)


def _compress(kcvc, pe, w1, w2):
    t = kcvc.shape[0]
    nch = t // CMP_STRIDE
    return pl.pallas_call(
        _compress_kernel,
        grid=(2, N_KV),
        in_specs=[pl.BlockSpec((t, DH), lambda s, j: (0, s * N_KV + j)),
                  pl.BlockSpec((1, CMP_LEN, DH), lambda s, j: (s, 0, 0)),
                  pl.BlockSpec((1, CMP_LEN * DH, DH), lambda s, j: (s, 0, 0)),
                  pl.BlockSpec((1, DH, DH), lambda s, j: (s, 0, 0))],
        out_specs=pl.BlockSpec((1, 1, nch, DH), lambda s, j: (s, j, 0, 0)),
        out_shape=jax.ShapeDtypeStruct((2, N_KV, nch, DH), BF16),
        compiler_params=_params(2),
        name="compress_kv",
    )(kcvc, pe, w1, w2)


def _stack_heads(q):
    return jnp.concatenate([q[:, h * DH:(h + 1) * DH] for h in range(HG)], axis=0)


def _store_heads(o_ref, o, blk=0):
    for h in range(HG):
        o_ref[blk * QB:(blk + 1) * QB, h * DH:(h + 1) * DH] = o[h * QB:(h + 1) * QB, :]


CMP_QBLOCKS = 2
CMP_COL_STEP = 256
FORCED_BLOCKS = 3
QSCALE = np.float32(DH ** -0.5 * np.log2(np.e))


def _cmp_block(g, i, w, r, cs_ref, q, kc_ref, vc_ref, pool_ref, tmap_ref, oc_ref, mb_ref, tf_ref, blk):
    q0 = i * QB
    rows_out = slice(blk * QB, (blk + 1) * QB)
    nsbp = mb_ref.shape[1]
    n_sel = min(TOP_N, r)

    qs = _stack_heads(q)
    s = lax.dot_general(qs, kc_ref[0, 0:w, :], _NT, preferred_element_type=F32)
    cpos = lax.broadcasted_iota(jnp.int32, (1, w), 1) * CMP_STRIDE + (CMP_LEN - 1) - q0
    row = lax.broadcasted_iota(jnp.int32, (QB, 1), 0)
    maskb = jnp.where(cpos <= row, 0.0, NEG)
    cposf = cpos.astype(F32)
    row_ok = jnp.where(q0 + row >= CMP_LEN - 1, 1.0, 0.0)

    vc = vc_ref[0, 0:w, :]
    imp = jnp.zeros((QB, w), F32)
    for h in range(HG):
        t = s[h * QB:(h + 1) * QB] + cs_ref[g, h] * cposf + maskb
        m = jnp.max(t, axis=1, keepdims=True)
        p = jnp.exp2(t - m)
        l = jnp.sum(p, axis=1, keepdims=True)
        pn = p * (row_ok / jnp.maximum(l, 1e-30))
        imp = imp + pn
        oc_ref[rows_out, h * DH:(h + 1) * DH] = jnp.dot(pn.astype(BF16), vc, preferred_element_type=F32)

    pool = pool_ref[0:r, 0:w]
    p1 = imp.astype(BF16)
    r1 = imp - p1.astype(F32)
    p2 = r1.astype(BF16)
    p3 = (r1 - p2.astype(F32)).astype(BF16)
    imps = (lax.dot_general(pool, p1, _NT, preferred_element_type=F32)
            + lax.dot_general(pool, p2, _NT, preferred_element_type=F32)
            + lax.dot_general(pool, p3, _NT, preferred_element_type=F32))

    j = lax.broadcasted_iota(jnp.int32, (r, QB), 0)
    tq = q0 + lax.broadcasted_iota(jnp.int32, (r, QB), 1)
    cur = lax.shift_right_arithmetic(tq, SLC_BLOCK.bit_length() - 1)
    forced = jnp.where(j == 0, 1, jnp.where(j == cur, 1, jnp.where(j == cur - 1, 1, 0)))
    sc = jnp.where(j > cur, -1.0, jnp.where(forced == 1, -2.0, imps))
    for _ in range(n_sel - FORCED_BLOCKS):
        mx = jnp.max(sc, axis=0, keepdims=True)
        idx = jnp.min(jnp.where(sc == mx, j, r), axis=0, keepdims=True)
        sc = jnp.where(j == idx, -2.0, sc)
    bias = jnp.where(j <= cur, jnp.where(sc == -2.0, 0.0, NEG), NEG)
    bias_t = bias.T
    mb_ref[rows_out, 0:r] = bias_t.astype(BF16)
    if r < nsbp:
        mb_ref[rows_out, r:nsbp] = jnp.full((QB, nsbp - r), NEG, BF16)
    anyq = jnp.max(jnp.where(bias_t == 0.0, 1.0, 0.0), axis=0, keepdims=True)
    hit = jnp.dot(jnp.broadcast_to(anyq, (8, r)).astype(BF16), tmap_ref[0:r, :],
                  preferred_element_type=F32)
    tf_ref[0, blk] = jnp.where(hit > 0.0, 1, 0).astype(jnp.int32)


def _cmp_kernel(cs_ref, q_ref, kc_ref, vc_ref, pool_ref, tmap_ref, oc_ref, mb_ref, tf_ref):
    g = pl.program_id(0)
    i0 = pl.program_id(1) * CMP_QBLOCKS
    nch = kc_ref.shape[1]
    nsbp = pool_ref.shape[0]
    step = min(CMP_COL_STEP, nch)
    need = (i0 + CMP_QBLOCKS - 1) * (QB // CMP_STRIDE) + (QB - CMP_LEN) // CMP_STRIDE
    variant = lax.shift_right_logical(need, step.bit_length() - 1)
    for v in range(nch // step):
        w = (v + 1) * step
        r = min(nsbp, -(-(w // (SLC_BLOCK // CMP_STRIDE)) // LANES) * LANES)

        @pl.when(variant == v)
        def _(w=w, r=r):
            for blk in range(CMP_QBLOCKS):
                _cmp_block(g, i0 + blk, w, r, cs_ref, q_ref[blk * QB:(blk + 1) * QB, :], kc_ref, vc_ref,
                           pool_ref, tmap_ref, oc_ref, mb_ref, tf_ref, blk)


def _cmp_attention(cslopes, qf, kc, vc, pool, tmap):
    t = qf.shape[0]
    nch = kc.shape[1]
    nsbp = pool.shape[0]
    rows = CMP_QBLOCKS * QB
    return pl.pallas_call(
        _cmp_kernel,
        grid=(N_KV, t // rows),
        in_specs=[pl.BlockSpec(memory_space=pltpu.SMEM),
                  pl.BlockSpec((rows, HG * DH), lambda g, i: (i, g)),
                  pl.BlockSpec((1, nch, DH), lambda g, i: (g, 0, 0)),
                  pl.BlockSpec((1, nch, DH), lambda g, i: (g, 0, 0)),
                  pl.BlockSpec((nsbp, nch), lambda g, i: (0, 0)),
                  pl.BlockSpec((nsbp, LANES), lambda g, i: (0, 0))],
        out_specs=[pl.BlockSpec((rows, HG * DH), lambda g, i: (i, g)),
                   pl.BlockSpec((rows, nsbp), lambda g, i: (i, g)),
                   pl.BlockSpec((1, CMP_QBLOCKS, 8, LANES), lambda g, i: (g, i, 0, 0))],
        out_shape=[jax.ShapeDtypeStruct((t, NSA_W), F32),
                   jax.ShapeDtypeStruct((t, N_KV * nsbp), BF16),
                   jax.ShapeDtypeStruct((N_KV, t // QB, 8, LANES), jnp.int32)],
        compiler_params=_params(2),
        name="cmp_attn_select",
    )(cslopes, qf, kc, vc, pool, tmap)


EXT_LO0 = BLK_PER_KT
EXT_HI0 = BLK_PER_KT + 3


def _aug_queries(qs, ext, sconst_ref):
    parts = []
    for h in range(HG):
        e = (ext + sconst_ref[0, h:h + 1, :]).astype(BF16)
        parts.append(jnp.concatenate([qs[h * QB:(h + 1) * QB], e], axis=1))
    return jnp.concatenate(parts, axis=0)


def _sel_kernel(lst_ref, cnt_ref, cs_ref, q_ref, ks_ref, vs_ref, mb_ref, pick_ref, kx_ref, vx_ref,
                sconst_ref, o_ref, m_scr, acc_scr, u_scr, p_scr, al_scr):
    g = pl.program_id(0)
    i = pl.program_id(1)
    q0 = i * QB
    nhalf = mb_ref.shape[1] // LANES
    n_prev = cnt_ref[g, i]

    qs = _stack_heads(q_ref[...])
    m_scr[...] = jnp.full(m_scr.shape, NEG, F32)
    acc_scr[...] = jnp.zeros(acc_scr.shape, F32)
    p_scr[1] = jnp.zeros(p_scr.shape[1:], BF16)
    al_scr[1] = jnp.ones(al_scr.shape[1:], F32)

    def score_stage(n, slot):
        tn = lst_ref[g, i, n]
        k0 = pl.multiple_of(tn * KT, KT)
        kaug = jnp.concatenate([ks_ref[pl.ds(k0, KT), :], kx_ref[...]], axis=1)
        half = lax.shift_right_logical(tn, OH_PERIOD.bit_length() - 1)
        mbh = mb_ref[:, 0:LANES]
        for hf in range(1, nhalf):
            mbh = jnp.where(half == hf, mb_ref[:, hf * LANES:(hf + 1) * LANES], mbh)
        ext = jnp.dot(mbh, pick_ref[tn & (OH_PERIOD - 1)], preferred_element_type=F32)
        u_scr[slot] = lax.dot_general(_aug_queries(qs, ext, sconst_ref), kaug, _NT,
                                      preferred_element_type=F32)

    def softmax_stage(n, slot, causal):
        k0 = lst_ref[g, i, n] * KT
        dq = (q0 - k0).astype(F32)
        if causal:
            visible = (lax.broadcasted_iota(jnp.int32, (QB, KT), 0)
                       - lax.broadcasted_iota(jnp.int32, (QB, KT), 1) + (q0 - k0)) >= 0
        for h in range(HG):
            rows = pl.ds(h * QB, QB)
            uh = u_scr[slot, rows, :]
            if causal:
                uh = jnp.where(visible, uh, NEG)
            c = cs_ref[g, h] * dq
            m_prev = m_scr[rows, :]
            m_next = jnp.maximum(m_prev, jnp.max(uh, axis=1, keepdims=True) - c)
            p_scr[slot, rows, :] = jnp.exp2(uh - (m_next + c)).astype(BF16)
            al_scr[slot, rows, :] = jnp.exp2(m_prev - m_next)
            m_scr[rows, :] = m_next

    def pv_stage(n, slot):
        k0 = pl.multiple_of(lst_ref[g, i, jnp.maximum(n, 0)] * KT, KT)
        vaug = jnp.concatenate([vs_ref[pl.ds(k0, KT), :], vx_ref[...]], axis=1)
        pv = jnp.dot(p_scr[slot], vaug, preferred_element_type=F32)
        acc_scr[...] = al_scr[slot] * acc_scr[...] + pv

    def finish():
        acc = acc_scr[...]
        _store_heads(o_ref, acc[:, 0:DH] / acc[:, DH:DH + 1])

    score_stage(0, 0)

    def pair(j, carry):
        n = 2 * j
        score_stage(n + 1, 1)
        softmax_stage(n, 0, False)
        pv_stage(n - 1, 1)
        score_stage(n + 2, 0)
        softmax_stage(n + 1, 1, False)
        pv_stage(n, 0)
        return carry

    lax.fori_loop(0, lax.shift_right_logical(n_prev, 1), pair, 0)
    n_even = n_prev & ~1

    @pl.when(n_prev == n_even)
    def _():
        softmax_stage(n_even, 0, True)
        pv_stage(n_even - 1, 1)
        pv_stage(n_even, 0)
        finish()

    @pl.when(n_prev != n_even)
    def _():
        score_stage(n_even + 1, 1)
        softmax_stage(n_even, 0, False)
        pv_stage(n_even - 1, 1)
        softmax_stage(n_even + 1, 1, True)
        pv_stage(n_even, 0)
        pv_stage(n_even + 1, 1)
        finish()


def _sel_attention(tile_list, tile_cnt, cslopes, qf, projb, mb, pick, kx, vx, sconst):
    t = qf.shape[0]
    nsbp = mb.shape[1] // N_KV
    grid_spec = pltpu.PrefetchScalarGridSpec(
        num_scalar_prefetch=2,
        grid=(N_KV, t // QB),
        in_specs=[pl.BlockSpec(memory_space=pltpu.SMEM),
                  pl.BlockSpec((QB, HG * DH), lambda g, i, *_: (i, g)),
                  pl.BlockSpec((t, DH), lambda g, i, *_: (0, g)),
                  pl.BlockSpec((t, DH), lambda g, i, *_: (0, N_KV + g)),
                  pl.BlockSpec((QB, nsbp), lambda g, i, *_: (i, g)),
                  pl.BlockSpec((OH_PERIOD, LANES, LANES), lambda g, i, *_: (0, 0, 0)),
                  pl.BlockSpec((KT, LANES), lambda g, i, *_: (0, 0)),
                  pl.BlockSpec((KT, LANES), lambda g, i, *_: (0, 0)),
                  pl.BlockSpec((1, HG, LANES), lambda g, i, *_: (g, 0, 0))],
        out_specs=pl.BlockSpec((QB, HG * DH), lambda g, i, *_: (i, g)),
        scratch_shapes=[pltpu.VMEM((HG * QB, 1), F32),
                        pltpu.VMEM((HG * QB, 2 * LANES), F32),
                        pltpu.VMEM((2, HG * QB, KT), F32),
                        pltpu.VMEM((2, HG * QB, KT), BF16),
                        pltpu.VMEM((2, HG * QB, 1), F32)])
    return pl.pallas_call(
        _sel_kernel,
        grid_spec=grid_spec,
        out_shape=jax.ShapeDtypeStruct((t, NSA_W), F32),
        compiler_params=_params(2),
        name="sel_attn",
    )(tile_list, tile_cnt, cslopes, qf, projb, projb, mb, pick, kx, vx, sconst)


def _tile_lists(tflags, t):
    nt = t // KT
    last = (jnp.arange(t // QB, dtype=jnp.int32) * QB + QB - 1) // KT
    ids = jnp.arange(nt, dtype=jnp.int32)
    active = (tflags != 0) & (ids[None, None, :] < last[None, :, None])
    cnt = jnp.sum(active, axis=-1).astype(jnp.int32)
    order = jnp.argsort(jnp.where(active, ids, nt + ids), axis=-1).astype(jnp.int32)
    lst = jnp.where(ids[None, None, :] == cnt[..., None], last[None, :, None], order)
    return lst, cnt


WSPAN = WINDOW + QB


WIN_QBLOCKS = 2


def _win_kernel(q_ref, kw_ref, vw_ref, band_ref, kx_ref, vx_ref, sconst_ref, o_ref):
    for blk in range(WIN_QBLOCKS):
        q0 = (pl.program_id(1) * WIN_QBLOCKS + blk) * QB
        start = pl.multiple_of(jnp.maximum(q0 - WINDOW, 0), QB)

        qs = _stack_heads(q_ref[blk * QB:(blk + 1) * QB, :])
        kaug = jnp.concatenate([kw_ref[pl.ds(start, WSPAN), :], kx_ref[...]], axis=1)
        vaug = jnp.concatenate([vw_ref[pl.ds(start, WSPAN), :], vx_ref[...]], axis=1)
        u = lax.dot_general(_aug_queries(qs, jnp.zeros((QB, LANES), F32), sconst_ref), kaug, _NT,
                            preferred_element_type=F32)
        band = band_ref[lax.shift_right_logical(q0 - start, QB.bit_length() - 1)]
        ps = []
        for h in range(HG):
            uh = u[h * QB:(h + 1) * QB] + band
            m = jnp.max(uh, axis=1, keepdims=True)
            ps.append(jnp.exp2(uh - m).astype(BF16))
        pv = jnp.dot(jnp.concatenate(ps, axis=0), vaug, preferred_element_type=F32)
        _store_heads(o_ref, pv[:, 0:DH] / pv[:, DH:DH + 1], blk)


def _win_attention(qf, projb, band, kxw, vxw, sconst):
    t = qf.shape[0]
    rows = WIN_QBLOCKS * QB
    return pl.pallas_call(
        _win_kernel,
        grid=(N_KV, t // rows),
        in_specs=[pl.BlockSpec((rows, HG * DH), lambda g, i: (i, g)),
                  pl.BlockSpec((t, DH), lambda g, i: (0, 2 * N_KV + g)),
                  pl.BlockSpec((t, DH), lambda g, i: (0, 3 * N_KV + g)),
                  pl.BlockSpec((WINDOW // QB + 1, QB, WSPAN), lambda g, i: (0, 0, 0)),
                  pl.BlockSpec((WSPAN, LANES), lambda g, i: (0, 0)),
                  pl.BlockSpec((WSPAN, LANES), lambda g, i: (0, 0)),
                  pl.BlockSpec((1, HG, LANES), lambda g, i: (g, 0, 0))],
        out_specs=pl.BlockSpec((rows, HG * DH), lambda g, i: (i, g)),
        out_shape=jax.ShapeDtypeStruct((t, NSA_W), F32),
        compiler_params=_params(2),
        name="win_attn",
    )(qf, projb, projb, band, kxw, vxw, sconst)


def _silu(z):
    return z * jax.nn.sigmoid(z)


def _mix_kernel(oc_ref, os_ref, ow_ref, gates_ref, zn_ref, u_ref, v_ref, zg_ref, zm_ref,
                qm_ref, kvm_ref, lng_ref, lnb_ref, wsp_ref, bsp_ref, y_ref):
    sig = jax.nn.sigmoid(gates_ref[...])
    for h in range(N_HEADS):
        sl = slice(h * DH, (h + 1) * DH)
        comb = (sig[:, 3 * h:3 * h + 1] * oc_ref[:, sl]
                + sig[:, 3 * h + 1:3 * h + 2] * os_ref[:, sl]
                + sig[:, 3 * h + 2:3 * h + 3] * ow_ref[:, sl])
        y_ref[:, sl] = (comb * _silu(zn_ref[:, sl])).astype(y_ref.dtype)

    vv = v_ref[...]
    mu = jnp.mean(vv, axis=-1, keepdims=True)
    var = jnp.mean(jnp.square(vv - mu), axis=-1, keepdims=True)
    vn = ((vv - mu) * lax.rsqrt(var + LN_EPS) * lng_ref[...] + lnb_ref[...]).astype(BF16)
    causal = (lax.broadcasted_iota(jnp.int32, (CHUNK, CHUNK), 0)
              >= lax.broadcasted_iota(jnp.int32, (CHUNK, CHUNK), 1))
    for gg in range(GM_GROUPS):
        sl = slice(gg * GM_DG, (gg + 1) * GM_DG)
        ws = jnp.where(causal, wsp_ref[gg], 0.0).astype(BF16)
        sv = jnp.dot(ws, vn[:, sl], preferred_element_type=F32) + bsp_ref[:, gg:gg + 1]
        y_ref[:, NSA_W + gg * GM_DG:NSA_W + (gg + 1) * GM_DG] = (
            u_ref[:, sl] * sv * _silu(zg_ref[:, sl])).astype(y_ref.dtype)

    mscale = np.float32(MEM_DH ** -0.5)
    for h in range(MEM_HEADS):
        sl = slice(h * MEM_DH, (h + 1) * MEM_DH)
        km = kvm_ref[:, h * MEM_DH:(h + 1) * MEM_DH]
        vm = kvm_ref[:, MEM_W + h * MEM_DH:MEM_W + (h + 1) * MEM_DH]
        s = lax.dot_general(qm_ref[:, sl], km, _NT, preferred_element_type=F32) * mscale
        m = jnp.max(s, axis=-1, keepdims=True)
        e = jnp.exp(s - m)
        p = e / jnp.sum(e, axis=-1, keepdims=True)
        o = jnp.dot(p.astype(BF16), vm, preferred_element_type=F32)
        y_ref[:, NSA_W + GM_W + h * MEM_DH:NSA_W + GM_W + (h + 1) * MEM_DH] = (
            o * _silu(zm_ref[:, sl])).astype(y_ref.dtype)


def _mix(oc, osel, ow, gates, zuvz, zm, qm, kvm, lng, lnb, wsp, bsp_t):
    t = oc.shape[0]
    row = lambda c: (lambda i: (i, c))
    full2 = lambda i: (0, 0)
    return pl.pallas_call(
        _mix_kernel,
        grid=(t // QB,),
        in_specs=[pl.BlockSpec((QB, NSA_W), row(0)),
                  pl.BlockSpec((QB, NSA_W), row(0)),
                  pl.BlockSpec((QB, NSA_W), row(0)),
                  pl.BlockSpec((QB, LANES), row(0)),
                  pl.BlockSpec((QB, NSA_W), row(0)),
                  pl.BlockSpec((QB, GM_W), row(2)),
                  pl.BlockSpec((QB, GM_W), row(3)),
                  pl.BlockSpec((QB, GM_W), row(4)),
                  pl.BlockSpec((QB, MEM_W), row(0)),
                  pl.BlockSpec((QB, MEM_W), row(0)),
                  pl.BlockSpec((MEM_LEN, 2 * MEM_W), full2),
                  pl.BlockSpec((1, GM_W), full2),
                  pl.BlockSpec((1, GM_W), full2),
                  pl.BlockSpec((GM_GROUPS, CHUNK, CHUNK), lambda i: (0, 0, 0)),
                  pl.BlockSpec((CHUNK, GM_GROUPS), full2)],
        out_specs=pl.BlockSpec((QB, D_MODEL), row(0)),
        out_shape=jax.ShapeDtypeStruct((t, D_MODEL), BF16),
        compiler_params=_params(1),
        name="mix_heads",
    )(oc, osel, ow, gates, zuvz, zuvz, zuvz, zuvz, zm, qm, kvm, lng, lnb, wsp, bsp_t)


def _ln_kernel(alpha, x_ref, h_ref, g_ref, b_ref, o_ref):
    z = alpha * x_ref[...] + h_ref[...]
    mu = jnp.mean(z, axis=-1, keepdims=True)
    var = jnp.mean(jnp.square(z - mu), axis=-1, keepdims=True)
    o_ref[...] = (z - mu) * lax.rsqrt(var + LN_EPS) * g_ref[...] + b_ref[...]


def _residual_ln(x2, h, g, b, alpha, tm=256):
    t, d = x2.shape
    tm = min(tm, t)
    return pl.pallas_call(
        functools.partial(_ln_kernel, np.float32(alpha)),
        grid=(t // tm,),
        in_specs=[pl.BlockSpec((tm, d), lambda i: (i, 0)),
                  pl.BlockSpec((tm, d), lambda i: (i, 0)),
                  pl.BlockSpec((1, d), lambda i: (0, 0)),
                  pl.BlockSpec((1, d), lambda i: (0, 0))],
        out_specs=pl.BlockSpec((tm, d), lambda i: (i, 0)),
        out_shape=jax.ShapeDtypeStruct((t, d), F32),
        compiler_params=_params(1),
        name="residual_ln",
    )(x2, h, g, b)


def _pool_matrix(nsbp, nch):
    j = np.arange(nsbp)[:, None]
    c = np.arange(nch)[None, :]
    ratio = SLC_BLOCK // CMP_STRIDE
    return ((c >= ratio * j - 1) & (c <= ratio * j + ratio - 1)).astype(np.float32)


def _tile_map(nsbp):
    j = np.arange(nsbp)[:, None]
    n = np.arange(LANES)[None, :]
    return (j // BLK_PER_KT == n).astype(np.float32)


def _block_pick():
    p = np.arange(OH_PERIOD)[:, None, None]
    j = np.arange(LANES)[None, :, None]
    c = np.arange(LANES)[None, None, :]
    return ((c < BLK_PER_KT) & (j == BLK_PER_KT * p + c)).astype(np.float32)


def _key_ext(n_keys, with_blocks):
    l = np.arange(n_keys)[:, None]
    c = np.arange(LANES)[None, :]
    kx = np.zeros((n_keys, LANES), np.float32)
    if with_blocks:
        kx += (c < BLK_PER_KT) & (l // SLC_BLOCK == c)
    kx += np.where((c >= EXT_LO0) & (c < EXT_LO0 + 3), l % 256, 0)
    kx += np.where((c >= EXT_HI0) & (c < EXT_HI0 + 3), l // 256, 0)
    return kx


def _ones_lane0(n_keys):
    vx = np.zeros((n_keys, LANES), np.float32)
    vx[:, 0] = 1.0
    return vx


def _window_band():
    d = np.arange(WINDOW // QB + 1)[:, None, None] * QB
    dist = d + np.arange(QB)[None, :, None] - np.arange(WSPAN)[None, None, :]
    return np.where((dist >= 0) & (dist < WINDOW), 0.0, NEG).astype(np.float32)


def _slope_lanes(cslopes):
    to_bf16 = lambda v: v.astype(BF16).astype(np.float32)
    s1 = to_bf16(cslopes)
    r1 = cslopes - s1
    s2 = to_bf16(r1)
    s3 = to_bf16(r1 - s2)
    pieces = np.stack([s1, s2, s3, 256.0 * s1, 256.0 * s2, 256.0 * s3], axis=-1)
    return np.pad(pieces, ((0, 0), (0, 0), (EXT_LO0, LANES - EXT_LO0 - 6))).astype(np.float32)


def kernel(x, mem, w_in, w_cmp_k1, w_cmp_k2, w_cmp_v1, w_cmp_v2, pe_cmp_k, pe_cmp_v,
           gm_ln_g, gm_ln_b, w_spatial, b_spatial, w_mem_kv, w_out, ln_g, ln_b):
    b, t, _ = x.shape
    assert b == 1 and t % KT == 0 and t >= WSPAN
    alpha = 2.0 ** 0.25
    nch = t // CMP_STRIDE
    nsb = t // SLC_BLOCK
    nsbp = -(-nsb // LANES) * LANES

    gate0 = NSA_W + 6 * KV_W
    w_t = w_in.T.astype(BF16)
    tail0 = gate0 + N_GATES
    x2 = x[0]
    xb = x2.astype(BF16)
    tm, tn = PROJ_TM, PROJ_TN
    proj = functools.partial(_matmul, xb, w_t, trans_b=True)
    qp = proj(BF16, tm, tn, "proj_q", 0, NSA_W, scale=QSCALE)
    kcvc = proj(F32, tm, tn, "proj_kvc", NSA_W, 2 * KV_W)
    kv = proj(BF16, tm, tn, "proj_kv", NSA_W + 2 * KV_W, 4 * KV_W)
    gates = proj(F32, tm, LANES, "proj_gates", gate0, LANES)
    zuvz = proj(F32, tm, tn, "proj_zuvz", tail0, NSA_W + 3 * GM_W)
    qm = proj(BF16, tm, tn, "proj_qm", tail0 + NSA_W + 3 * GM_W, MEM_W)
    zm = proj(F32, tm, tn, "proj_zm", tail0 + NSA_W + 3 * GM_W + MEM_W, MEM_W)
    kvm = _matmul(mem[0].astype(BF16), w_mem_kv.astype(BF16), BF16, MEM_LEN, tn, "mem_kv")

    pe = jnp.stack([pe_cmp_k, pe_cmp_v])
    w1 = jnp.stack([w_cmp_k1, w_cmp_v1]).astype(BF16)
    w2 = jnp.stack([w_cmp_k2, w_cmp_v2]).astype(BF16)
    kvc = _compress(kcvc, pe, w1, w2)

    slopes_np = np.exp2(-8.0 * np.arange(1, N_HEADS + 1, dtype=np.float32) / N_HEADS
                        ).astype(np.float32).reshape(N_KV, HG)
    cslopes_np = (slopes_np * np.float32(np.log2(np.e))).astype(np.float32)
    cslopes = jnp.asarray(cslopes_np)
    sconst = jnp.asarray(_slope_lanes(cslopes_np))
    pool = jnp.asarray(_pool_matrix(nsbp, nch), BF16)

    tmap = jnp.asarray(_tile_map(nsbp), BF16)
    o_c, mb, tf = _cmp_attention(cslopes, qp, kvc[0], kvc[1], pool, tmap)
    tflags = tf[:, :, 0, :t // KT]
    tile_list, tile_cnt = _tile_lists(tflags, t)
    o_s = _sel_attention(tile_list, tile_cnt, cslopes, qp, kv, mb, jnp.asarray(_block_pick(), BF16),
                         jnp.asarray(_key_ext(KT, True), BF16), jnp.asarray(_ones_lane0(KT), BF16), sconst)
    o_w = _win_attention(qp, kv, jnp.asarray(_window_band(), F32),
                         jnp.asarray(_key_ext(WSPAN, False), BF16), jnp.asarray(_ones_lane0(WSPAN), BF16),
                         sconst)

    y = _mix(o_c, o_s, o_w, gates, zuvz, zm, qm, kvm,
             gm_ln_g.reshape(1, GM_W), gm_ln_b.reshape(1, GM_W), w_spatial, b_spatial.T)
    h = _matmul(y, w_out.astype(BF16), F32, PROJ_TM, PROJ_TN, "out_proj")
    out = _residual_ln(x2, h, ln_g.reshape(1, D_MODEL), ln_b.reshape(1, D_MODEL), alpha)
    return out[None]
```

```python
import functools

import numpy as np
import jax
import jax.numpy as jnp
from jax import lax
from jax.experimental import pallas as pl
from jax.experimental.pallas import tpu as pltpu

D_MODEL = 4096
DH = 128
N_HEADS = 16
HG = 4
N_KV = 4
NSA_W = N_HEADS * DH
KV_W = N_KV * DH
CMP_LEN = 32
CMP_STRIDE = 16
SLC_BLOCK = 64
TOP_N = 16
WINDOW = 512
QB = 128
CHUNK = 128
GM_W = 1024
GM_GROUPS = 4
GM_DG = GM_W // GM_GROUPS
MEM_W = 1024
MEM_LEN = 256
MEM_HEADS = 4
MEM_DH = MEM_W // MEM_HEADS
LN_EPS = 1e-5
N_GATES = N_HEADS * 3

LANES = 128
KT = 512
BLK_PER_KT = KT // SLC_BLOCK
OH_PERIOD = LANES // BLK_PER_KT
NEG = -2.0 ** 100
PROJ_TM = 1024
PROJ_TN = 1024
VMEM_LIMIT = 48 * 1024 * 1024

F32 = jnp.float32
BF16 = jnp.bfloat16
_NT = (((1,), (1,)), ((), ()))


def _params(n_axes):
    return pltpu.CompilerParams(dimension_semantics=("arbitrary",) * n_axes,
                                vmem_limit_bytes=VMEM_LIMIT)


def _mm_kernel(scale, trans_b, a_ref, b_ref, o_ref):
    if trans_b:
        acc = lax.dot_general(a_ref[...], b_ref[...], _NT, preferred_element_type=F32)
    else:
        acc = jnp.dot(a_ref[...], b_ref[...], preferred_element_type=F32)
    if scale is not None:
        acc = acc * scale
    o_ref[...] = acc.astype(o_ref.dtype)


def _matmul(a, b, out_dtype, tm, tn, name, col0=0, ncols=None, scale=None, trans_b=False):
    m, k = a.shape
    n_all = b.shape[0] if trans_b else b.shape[1]
    ncols = n_all - col0 if ncols is None else ncols
    tm = min(tm, m)
    assert ncols % tn == 0 and m % tm == 0
    if trans_b:
        assert col0 % 16 == 0
        b_spec = pl.BlockSpec((pl.Element(tn), pl.Element(k)),
                              lambda j, i: (pl.multiple_of(col0 + j * tn, 16), 0))
    else:
        assert col0 % tn == 0
        b_spec = pl.BlockSpec((k, tn), lambda j, i: (0, col0 // tn + j))
    return pl.pallas_call(
        functools.partial(_mm_kernel, scale, trans_b),
        grid=(ncols // tn, m // tm),
        in_specs=[pl.BlockSpec((tm, k), lambda j, i: (i, 0)), b_spec],
        out_specs=pl.BlockSpec((tm, tn), lambda j, i: (i, j)),
        out_shape=jax.ShapeDtypeStruct((m, ncols), out_dtype),
        compiler_params=_params(2),
        name=name,
    )(a, b)


def _gelu_tanh(x):
    c = np.float32(np.sqrt(2.0 / np.pi))
    return x * (0.5 * (1.0 + jnp.tanh(c * (x + 0.044715 * (x * x * x)))))


def _compress_kernel(x_ref, pe_ref, w1_ref, w2_ref, o_ref):
    nch = o_ref.shape[2]
    a = jnp.zeros((nch, DH), F32)
    b = jnp.zeros((nch, DH), F32)
    for r in range(CMP_STRIDE):
        xr = x_ref[pl.ds(r, nch, stride=CMP_STRIDE), :]
        lo = (xr + pe_ref[0, r:r + 1, :]).astype(BF16)
        hi = (xr + pe_ref[0, CMP_STRIDE + r:CMP_STRIDE + r + 1, :]).astype(BF16)
        a = a + jnp.dot(lo, w1_ref[0, r * DH:(r + 1) * DH, :], preferred_element_type=F32)
        b = b + jnp.dot(hi, w1_ref[0, (CMP_STRIDE + r) * DH:(CMP_STRIDE + r + 1) * DH, :],
                        preferred_element_type=F32)
    pre = a + pltpu.roll(b, nch - 1, axis=0)
    h = _gelu_tanh(pre).astype(BF16)
    o_ref[0, 0] = jnp.dot(h, w2_ref[0], preferred_element_type=F32).astype(o_ref.dtype)


def _compress(kcvc, pe, w1, w2):
    t = kcvc.shape[0]
    nch = t // CMP_STRIDE
    return pl.pallas_call(
        _compress_kernel,
        grid=(2, N_KV),
        in_specs=[pl.BlockSpec((t, DH), lambda s, j: (0, s * N_KV + j)),
                  pl.BlockSpec((1, CMP_LEN, DH), lambda s, j: (s, 0, 0)),
                  pl.BlockSpec((1, CMP_LEN * DH, DH), lambda s, j: (s, 0, 0)),
                  pl.BlockSpec((1, DH, DH), lambda s, j: (s, 0, 0))],
        out_specs=pl.BlockSpec((1, 1, nch, DH), lambda s, j: (s, j, 0, 0)),
        out_shape=jax.ShapeDtypeStruct((2, N_KV, nch, DH), BF16),
        compiler_params=_params(2),
        name="compress_kv",
    )(kcvc, pe, w1, w2)


def _stack_heads(q):
    return jnp.concatenate([q[:, h * DH:(h + 1) * DH] for h in range(HG)], axis=0)


def _store_heads(o_ref, o, blk=0, rows=QB):
    for h in range(HG):
        o_ref[blk * rows:(blk + 1) * rows, h * DH:(h + 1) * DH] = o[h * rows:(h + 1) * rows, :]


CMP_QBLOCKS = 2
CMP_COL_STEP = 256
FORCED_BLOCKS = 3
QSCALE = np.float32(DH ** -0.5 * np.log2(np.e))


def _cmp_block(g, i, w, r, cs_ref, q, kc_ref, vc_ref, pool_ref, tmap_ref, oc_ref, mb_ref, tf_ref, blk):
    q0 = i * QB
    rows_out = slice(blk * QB, (blk + 1) * QB)
    nsbp = mb_ref.shape[1]
    n_sel = min(TOP_N, r)

    qs = _stack_heads(q)
    s = lax.dot_general(qs, kc_ref[0, 0:w, :], _NT, preferred_element_type=F32)
    cpos = lax.broadcasted_iota(jnp.int32, (1, w), 1) * CMP_STRIDE + (CMP_LEN - 1) - q0
    row = lax.broadcasted_iota(jnp.int32, (QB, 1), 0)
    maskb = jnp.where(cpos <= row, 0.0, NEG)
    cposf = cpos.astype(F32)
    row_ok = jnp.where(q0 + row >= CMP_LEN - 1, 1.0, 0.0)

    vc = vc_ref[0, 0:w, :]
    imp = jnp.zeros((QB, w), F32)
    for h in range(HG):
        t = s[h * QB:(h + 1) * QB] + cs_ref[g, h] * cposf + maskb
        m = jnp.max(t, axis=1, keepdims=True)
        p = jnp.exp2(t - m)
        l = jnp.sum(p, axis=1, keepdims=True)
        pn = p * (row_ok / jnp.maximum(l, 1e-30))
        imp = imp + pn
        oc_ref[rows_out, h * DH:(h + 1) * DH] = jnp.dot(pn.astype(BF16), vc, preferred_element_type=F32)

    pool = pool_ref[0:r, 0:w]
    p1 = imp.astype(BF16)
    r1 = imp - p1.astype(F32)
    p2 = r1.astype(BF16)
    p3 = (r1 - p2.astype(F32)).astype(BF16)
    imps = (lax.dot_general(pool, p1, _NT, preferred_element_type=F32)
            + lax.dot_general(pool, p2, _NT, preferred_element_type=F32)
            + lax.dot_general(pool, p3, _NT, preferred_element_type=F32))

    j = lax.broadcasted_iota(jnp.int32, (r, QB), 0)
    tq = q0 + lax.broadcasted_iota(jnp.int32, (r, QB), 1)
    cur = lax.shift_right_arithmetic(tq, SLC_BLOCK.bit_length() - 1)
    forced = jnp.where(j == 0, 1, jnp.where(j == cur, 1, jnp.where(j == cur - 1, 1, 0)))
    sc = jnp.where(j > cur, -1.0, jnp.where(forced == 1, -2.0, imps))
    for _ in range(n_sel - FORCED_BLOCKS):
        mx = jnp.max(sc, axis=0, keepdims=True)
        idx = jnp.min(jnp.where(sc == mx, j, r), axis=0, keepdims=True)
        sc = jnp.where(j == idx, -2.0, sc)
    bias = jnp.where(j <= cur, jnp.where(sc == -2.0, 0.0, NEG), NEG)
    bias_t = bias.T
    mb_ref[rows_out, 0:r] = bias_t.astype(BF16)
    if r < nsbp:
        mb_ref[rows_out, r:nsbp] = jnp.full((QB, nsbp - r), NEG, BF16)
    anyq = jnp.max(jnp.where(bias_t == 0.0, 1.0, 0.0), axis=0, keepdims=True)
    hit = jnp.dot(jnp.broadcast_to(anyq, (8, r)).astype(BF16), tmap_ref[0:r, :],
                  preferred_element_type=F32)
    tf_ref[0, blk] = jnp.where(hit > 0.0, 1, 0).astype(jnp.int32)


def _cmp_kernel(cs_ref, q_ref, kc_ref, vc_ref, pool_ref, tmap_ref, oc_ref, mb_ref, tf_ref):
    g = pl.program_id(0)
    i0 = pl.program_id(1) * CMP_QBLOCKS
    nch = kc_ref.shape[1]
    nsbp = pool_ref.shape[0]
    step = min(CMP_COL_STEP, nch)
    need = (i0 + CMP_QBLOCKS - 1) * (QB // CMP_STRIDE) + (QB - CMP_LEN) // CMP_STRIDE
    variant = lax.shift_right_logical(need, step.bit_length() - 1)
    for v in range(nch // step):
        w = (v + 1) * step
        r = min(nsbp, -(-(w // (SLC_BLOCK // CMP_STRIDE)) // LANES) * LANES)

        @pl.when(variant == v)
        def _(w=w, r=r):
            for blk in range(CMP_QBLOCKS):
                _cmp_block(g, i0 + blk, w, r, cs_ref, q_ref[blk * QB:(blk + 1) * QB, :], kc_ref, vc_ref,
                           pool_ref, tmap_ref, oc_ref, mb_ref, tf_ref, blk)


def _cmp_attention(cslopes, qf, kc, vc, pool, tmap):
    t = qf.shape[0]
    nch = kc.shape[1]
    nsbp = pool.shape[0]
    rows = CMP_QBLOCKS * QB
    return pl.pallas_call(
        _cmp_kernel,
        grid=(N_KV, t // rows),
        in_specs=[pl.BlockSpec(memory_space=pltpu.SMEM),
                  pl.BlockSpec((rows, HG * DH), lambda g, i: (i, g)),
                  pl.BlockSpec((1, nch, DH), lambda g, i: (g, 0, 0)),
                  pl.BlockSpec((1, nch, DH), lambda g, i: (g, 0, 0)),
                  pl.BlockSpec((nsbp, nch), lambda g, i: (0, 0)),
                  pl.BlockSpec((nsbp, LANES), lambda g, i: (0, 0))],
        out_specs=[pl.BlockSpec((rows, HG * DH), lambda g, i: (i, g)),
                   pl.BlockSpec((rows, nsbp), lambda g, i: (i, g)),
                   pl.BlockSpec((1, CMP_QBLOCKS, 8, LANES), lambda g, i: (g, i, 0, 0))],
        out_shape=[jax.ShapeDtypeStruct((t, NSA_W), F32),
                   jax.ShapeDtypeStruct((t, N_KV * nsbp), BF16),
                   jax.ShapeDtypeStruct((N_KV, t // QB, 8, LANES), jnp.int32)],
        compiler_params=_params(2),
        name="cmp_attn_select",
    )(cslopes, qf, kc, vc, pool, tmap)


SEL_QB = 256
EXT_LO0 = BLK_PER_KT
EXT_HI0 = BLK_PER_KT + 3


def _aug_queries(qs, ext, sconst_ref, rows=QB):
    parts = []
    for h in range(HG):
        e = (ext + sconst_ref[0, h:h + 1, :]).astype(BF16)
        parts.append(jnp.concatenate([qs[h * rows:(h + 1) * rows], e], axis=1))
    return jnp.concatenate(parts, axis=0)


def _sel_kernel(lst_ref, cnt_ref, cs_ref, q_ref, ks_ref, vs_ref, mb_ref, pick_ref, kx_ref, vx_ref,
                sconst_ref, o_ref, m_scr, acc_scr, u_scr, p_scr, al_scr):
    g = pl.program_id(0)
    i = pl.program_id(1)
    q0 = i * SEL_QB
    nhalf = mb_ref.shape[1] // LANES
    n_prev = cnt_ref[g, i]

    qs = _stack_heads(q_ref[...])
    m_scr[...] = jnp.full(m_scr.shape, NEG, F32)
    acc_scr[...] = jnp.zeros(acc_scr.shape, F32)
    p_scr[1] = jnp.zeros(p_scr.shape[1:], BF16)
    al_scr[1] = jnp.ones(al_scr.shape[1:], F32)

    def score_stage(n, slot):
        tn = lst_ref[g, i, n]
        k0 = pl.multiple_of(tn * KT, KT)
        kaug = jnp.concatenate([ks_ref[pl.ds(k0, KT), :], kx_ref[...]], axis=1)
        half = lax.shift_right_logical(tn, OH_PERIOD.bit_length() - 1)
        mbh = mb_ref[:, 0:LANES]
        for hf in range(1, nhalf):
            mbh = jnp.where(half == hf, mb_ref[:, hf * LANES:(hf + 1) * LANES], mbh)
        ext = jnp.dot(mbh, pick_ref[tn & (OH_PERIOD - 1)], preferred_element_type=F32)
        u_scr[slot] = lax.dot_general(_aug_queries(qs, ext, sconst_ref, SEL_QB), kaug, _NT,
                                      preferred_element_type=F32)

    def softmax_stage(n, slot, causal):
        k0 = lst_ref[g, i, n] * KT
        dq = (q0 - k0).astype(F32)
        if causal:
            visible = (lax.broadcasted_iota(jnp.int32, (SEL_QB, KT), 0)
                       - lax.broadcasted_iota(jnp.int32, (SEL_QB, KT), 1) + (q0 - k0)) >= 0
        for h in range(HG):
            rows = pl.ds(h * SEL_QB, SEL_QB)
            uh = u_scr[slot, rows, :]
            if causal:
                uh = jnp.where(visible, uh, NEG)
            c = cs_ref[g, h] * dq
            m_prev = m_scr[rows, :]
            m_next = jnp.maximum(m_prev, jnp.max(uh, axis=1, keepdims=True) - c)
            p_scr[slot, rows, :] = jnp.exp2(uh - (m_next + c)).astype(BF16)
            al_scr[slot, rows, :] = jnp.exp2(m_prev - m_next)
            m_scr[rows, :] = m_next

    def pv_stage(n, slot):
        k0 = pl.multiple_of(lst_ref[g, i, jnp.maximum(n, 0)] * KT, KT)
        vaug = jnp.concatenate([vs_ref[pl.ds(k0, KT), :], vx_ref[...]], axis=1)
        pv = jnp.dot(p_scr[slot], vaug, preferred_element_type=F32)
        acc_scr[...] = al_scr[slot] * acc_scr[...] + pv

    def finish():
        acc = acc_scr[...]
        _store_heads(o_ref, acc[:, 0:DH] / acc[:, DH:DH + 1], rows=SEL_QB)

    score_stage(0, 0)

    def pair(j, carry):
        n = 2 * j
        score_stage(n + 1, 1)
        softmax_stage(n, 0, False)
        pv_stage(n - 1, 1)
        score_stage(n + 2, 0)
        softmax_stage(n + 1, 1, False)
        pv_stage(n, 0)
        return carry

    lax.fori_loop(0, lax.shift_right_logical(n_prev, 1), pair, 0)
    n_even = n_prev & ~1

    @pl.when(n_prev == n_even)
    def _():
        softmax_stage(n_even, 0, True)
        pv_stage(n_even - 1, 1)
        pv_stage(n_even, 0)
        finish()

    @pl.when(n_prev != n_even)
    def _():
        score_stage(n_even + 1, 1)
        softmax_stage(n_even, 0, False)
        pv_stage(n_even - 1, 1)
        softmax_stage(n_even + 1, 1, True)
        pv_stage(n_even, 0)
        pv_stage(n_even + 1, 1)
        finish()


def _sel_attention(tile_list, tile_cnt, cslopes, qf, projb, mb, pick, kx, vx, sconst):
    t = qf.shape[0]
    nsbp = mb.shape[1] // N_KV
    grid_spec = pltpu.PrefetchScalarGridSpec(
        num_scalar_prefetch=2,
        grid=(N_KV, t // SEL_QB),
        in_specs=[pl.BlockSpec(memory_space=pltpu.SMEM),
                  pl.BlockSpec((SEL_QB, HG * DH), lambda g, i, *_: (i, g)),
                  pl.BlockSpec((t, DH), lambda g, i, *_: (0, g)),
                  pl.BlockSpec((t, DH), lambda g, i, *_: (0, N_KV + g)),
                  pl.BlockSpec((SEL_QB, nsbp), lambda g, i, *_: (i, g)),
                  pl.BlockSpec((OH_PERIOD, LANES, LANES), lambda g, i, *_: (0, 0, 0)),
                  pl.BlockSpec((KT, LANES), lambda g, i, *_: (0, 0)),
                  pl.BlockSpec((KT, LANES), lambda g, i, *_: (0, 0)),
                  pl.BlockSpec((1, HG, LANES), lambda g, i, *_: (g, 0, 0))],
        out_specs=pl.BlockSpec((SEL_QB, HG * DH), lambda g, i, *_: (i, g)),
        scratch_shapes=[pltpu.VMEM((HG * SEL_QB, 1), F32),
                        pltpu.VMEM((HG * SEL_QB, 2 * LANES), F32),
                        pltpu.VMEM((2, HG * SEL_QB, KT), F32),
                        pltpu.VMEM((2, HG * SEL_QB, KT), BF16),
                        pltpu.VMEM((2, HG * SEL_QB, 1), F32)])
    return pl.pallas_call(
        _sel_kernel,
        grid_spec=grid_spec,
        out_shape=jax.ShapeDtypeStruct((t, NSA_W), F32),
        compiler_params=_params(2),
        name="sel_attn",
    )(tile_list, tile_cnt, cslopes, qf, projb, projb, mb, pick, kx, vx, sconst)


def _tile_lists(tflags, t):
    nt = t // KT
    nq = t // SEL_QB
    tflags = jnp.max(tflags.reshape(N_KV, nq, SEL_QB // QB, nt), axis=2)
    last = (jnp.arange(nq, dtype=jnp.int32) * SEL_QB + SEL_QB - 1) // KT
    ids = jnp.arange(nt, dtype=jnp.int32)
    active = (tflags != 0) & (ids[None, None, :] < last[None, :, None])
    cnt = jnp.sum(active, axis=-1).astype(jnp.int32)
    order = jnp.argsort(jnp.where(active, ids, nt + ids), axis=-1).astype(jnp.int32)
    lst = jnp.where(ids[None, None, :] == cnt[..., None], last[None, :, None], order)
    return lst, cnt


WSPAN = WINDOW + QB


WIN_QBLOCKS = 2


def _win_kernel(q_ref, kw_ref, vw_ref, band_ref, kx_ref, vx_ref, sconst_ref, o_ref):
    for blk in range(WIN_QBLOCKS):
        q0 = (pl.program_id(1) * WIN_QBLOCKS + blk) * QB
        start = pl.multiple_of(jnp.maximum(q0 - WINDOW, 0), QB)

        qs = _stack_heads(q_ref[blk * QB:(blk + 1) * QB, :])
        kaug = jnp.concatenate([kw_ref[pl.ds(start, WSPAN), :], kx_ref[...]], axis=1)
        vaug = jnp.concatenate([vw_ref[pl.ds(start, WSPAN), :], vx_ref[...]], axis=1)
        u = lax.dot_general(_aug_queries(qs, jnp.zeros((QB, LANES), F32), sconst_ref), kaug, _NT,
                            preferred_element_type=F32)
        band = band_ref[lax.shift_right_logical(q0 - start, QB.bit_length() - 1)]
        ps = []
        for h in range(HG):
            uh = u[h * QB:(h + 1) * QB] + band
            m = jnp.max(uh, axis=1, keepdims=True)
            ps.append(jnp.exp2(uh - m).astype(BF16))
        pv = jnp.dot(jnp.concatenate(ps, axis=0), vaug, preferred_element_type=F32)
        _store_heads(o_ref, pv[:, 0:DH] / pv[:, DH:DH + 1], blk)


def _win_attention(qf, projb, band, kxw, vxw, sconst):
    t = qf.shape[0]
    rows = WIN_QBLOCKS * QB
    return pl.pallas_call(
        _win_kernel,
        grid=(N_KV, t // rows),
        in_specs=[pl.BlockSpec((rows, HG * DH), lambda g, i: (i, g)),
                  pl.BlockSpec((t, DH), lambda g, i: (0, 2 * N_KV + g)),
                  pl.BlockSpec((t, DH), lambda g, i: (0, 3 * N_KV + g)),
                  pl.BlockSpec((WINDOW // QB + 1, QB, WSPAN), lambda g, i: (0, 0, 0)),
                  pl.BlockSpec((WSPAN, LANES), lambda g, i: (0, 0)),
                  pl.BlockSpec((WSPAN, LANES), lambda g, i: (0, 0)),
                  pl.BlockSpec((1, HG, LANES), lambda g, i: (g, 0, 0))],
        out_specs=pl.BlockSpec((rows, HG * DH), lambda g, i: (i, g)),
        out_shape=jax.ShapeDtypeStruct((t, NSA_W), F32),
        compiler_params=_params(2),
        name="win_attn",
    )(qf, projb, projb, band, kxw, vxw, sconst)


def _silu(z):
    return z * jax.nn.sigmoid(z)


def _mix_kernel(oc_ref, os_ref, ow_ref, gates_ref, zn_ref, u_ref, v_ref, zg_ref, zm_ref,
                qm_ref, kvm_ref, lng_ref, lnb_ref, wsp_ref, bsp_ref, y_ref):
    sig = jax.nn.sigmoid(gates_ref[...])
    for h in range(N_HEADS):
        sl = slice(h * DH, (h + 1) * DH)
        comb = (sig[:, 3 * h:3 * h + 1] * oc_ref[:, sl]
                + sig[:, 3 * h + 1:3 * h + 2] * os_ref[:, sl]
                + sig[:, 3 * h + 2:3 * h + 3] * ow_ref[:, sl])
        y_ref[:, sl] = (comb * _silu(zn_ref[:, sl])).astype(y_ref.dtype)

    vv = v_ref[...]
    mu = jnp.mean(vv, axis=-1, keepdims=True)
    var = jnp.mean(jnp.square(vv - mu), axis=-1, keepdims=True)
    vn = ((vv - mu) * lax.rsqrt(var + LN_EPS) * lng_ref[...] + lnb_ref[...]).astype(BF16)
    causal = (lax.broadcasted_iota(jnp.int32, (CHUNK, CHUNK), 0)
              >= lax.broadcasted_iota(jnp.int32, (CHUNK, CHUNK), 1))
    for gg in range(GM_GROUPS):
        sl = slice(gg * GM_DG, (gg + 1) * GM_DG)
        ws = jnp.where(causal, wsp_ref[gg], 0.0).astype(BF16)
        sv = jnp.dot(ws, vn[:, sl], preferred_element_type=F32) + bsp_ref[:, gg:gg + 1]
        y_ref[:, NSA_W + gg * GM_DG:NSA_W + (gg + 1) * GM_DG] = (
            u_ref[:, sl] * sv * _silu(zg_ref[:, sl])).astype(y_ref.dtype)

    mscale = np.float32(MEM_DH ** -0.5)
    for h in range(MEM_HEADS):
        sl = slice(h * MEM_DH, (h + 1) * MEM_DH)
        km = kvm_ref[:, h * MEM_DH:(h + 1) * MEM_DH]
        vm = kvm_ref[:, MEM_W + h * MEM_DH:MEM_W + (h + 1) * MEM_DH]
        s = lax.dot_general(qm_ref[:, sl], km, _NT, preferred_element_type=F32) * mscale
        m = jnp.max(s, axis=-1, keepdims=True)
        e = jnp.exp(s - m)
        p = e / jnp.sum(e, axis=-1, keepdims=True)
        o = jnp.dot(p.astype(BF16), vm, preferred_element_type=F32)
        y_ref[:, NSA_W + GM_W + h * MEM_DH:NSA_W + GM_W + (h + 1) * MEM_DH] = (
            o * _silu(zm_ref[:, sl])).astype(y_ref.dtype)


def _mix(oc, osel, ow, gates, zuvz, zm, qm, kvm, lng, lnb, wsp, bsp_t):
    t = oc.shape[0]
    row = lambda c: (lambda i: (i, c))
    full2 = lambda i: (0, 0)
    return pl.pallas_call(
        _mix_kernel,
        grid=(t // QB,),
        in_specs=[pl.BlockSpec((QB, NSA_W), row(0)),
                  pl.BlockSpec((QB, NSA_W), row(0)),
                  pl.BlockSpec((QB, NSA_W), row(0)),
                  pl.BlockSpec((QB, LANES), row(0)),
                  pl.BlockSpec((QB, NSA_W), row(0)),
                  pl.BlockSpec((QB, GM_W), row(2)),
                  pl.BlockSpec((QB, GM_W), row(3)),
                  pl.BlockSpec((QB, GM_W), row(4)),
                  pl.BlockSpec((QB, MEM_W), row(0)),
                  pl.BlockSpec((QB, MEM_W), row(0)),
                  pl.BlockSpec((MEM_LEN, 2 * MEM_W), full2),
                  pl.BlockSpec((1, GM_W), full2),
                  pl.BlockSpec((1, GM_W), full2),
                  pl.BlockSpec((GM_GROUPS, CHUNK, CHUNK), lambda i: (0, 0, 0)),
                  pl.BlockSpec((CHUNK, GM_GROUPS), full2)],
        out_specs=pl.BlockSpec((QB, D_MODEL), row(0)),
        out_shape=jax.ShapeDtypeStruct((t, D_MODEL), BF16),
        compiler_params=_params(1),
        name="mix_heads",
    )(oc, osel, ow, gates, zuvz, zuvz, zuvz, zuvz, zm, qm, kvm, lng, lnb, wsp, bsp_t)


def _ln_kernel(alpha, x_ref, h_ref, g_ref, b_ref, o_ref):
    z = alpha * x_ref[...] + h_ref[...]
    mu = jnp.mean(z, axis=-1, keepdims=True)
    var = jnp.mean(jnp.square(z - mu), axis=-1, keepdims=True)
    o_ref[...] = (z - mu) * lax.rsqrt(var + LN_EPS) * g_ref[...] + b_ref[...]


def _residual_ln(x2, h, g, b, alpha, tm=256):
    t, d = x2.shape
    tm = min(tm, t)
    return pl.pallas_call(
        functools.partial(_ln_kernel, np.float32(alpha)),
        grid=(t // tm,),
        in_specs=[pl.BlockSpec((tm, d), lambda i: (i, 0)),
                  pl.BlockSpec((tm, d), lambda i: (i, 0)),
                  pl.BlockSpec((1, d), lambda i: (0, 0)),
                  pl.BlockSpec((1, d), lambda i: (0, 0))],
        out_specs=pl.BlockSpec((tm, d), lambda i: (i, 0)),
        out_shape=jax.ShapeDtypeStruct((t, d), F32),
        compiler_params=_params(1),
        name="residual_ln",
    )(x2, h, g, b)


def _pool_matrix(nsbp, nch):
    j = np.arange(nsbp)[:, None]
    c = np.arange(nch)[None, :]
    ratio = SLC_BLOCK // CMP_STRIDE
    return ((c >= ratio * j - 1) & (c <= ratio * j + ratio - 1)).astype(np.float32)


def _tile_map(nsbp):
    j = np.arange(nsbp)[:, None]
    n = np.arange(LANES)[None, :]
    return (j // BLK_PER_KT == n).astype(np.float32)


def _block_pick():
    p = np.arange(OH_PERIOD)[:, None, None]
    j = np.arange(LANES)[None, :, None]
    c = np.arange(LANES)[None, None, :]
    return ((c < BLK_PER_KT) & (j == BLK_PER_KT * p + c)).astype(np.float32)


def _key_ext(n_keys, with_blocks):
    l = np.arange(n_keys)[:, None]
    c = np.arange(LANES)[None, :]
    kx = np.zeros((n_keys, LANES), np.float32)
    if with_blocks:
        kx += (c < BLK_PER_KT) & (l // SLC_BLOCK == c)
    kx += np.where((c >= EXT_LO0) & (c < EXT_LO0 + 3), l % 256, 0)
    kx += np.where((c >= EXT_HI0) & (c < EXT_HI0 + 3), l // 256, 0)
    return kx


def _ones_lane0(n_keys):
    vx = np.zeros((n_keys, LANES), np.float32)
    vx[:, 0] = 1.0
    return vx


def _window_band():
    d = np.arange(WINDOW // QB + 1)[:, None, None] * QB
    dist = d + np.arange(QB)[None, :, None] - np.arange(WSPAN)[None, None, :]
    return np.where((dist >= 0) & (dist < WINDOW), 0.0, NEG).astype(np.float32)


def _slope_lanes(cslopes):
    to_bf16 = lambda v: v.astype(BF16).astype(np.float32)
    s1 = to_bf16(cslopes)
    r1 = cslopes - s1
    s2 = to_bf16(r1)
    s3 = to_bf16(r1 - s2)
    pieces = np.stack([s1, s2, s3, 256.0 * s1, 256.0 * s2, 256.0 * s3], axis=-1)
    return np.pad(pieces, ((0, 0), (0, 0), (EXT_LO0, LANES - EXT_LO0 - 6))).astype(np.float32)


def kernel(x, mem, w_in, w_cmp_k1, w_cmp_k2, w_cmp_v1, w_cmp_v2, pe_cmp_k, pe_cmp_v,
           gm_ln_g, gm_ln_b, w_spatial, b_spatial, w_mem_kv, w_out, ln_g, ln_b):
    b, t, _ = x.shape
    assert b == 1 and t % KT == 0 and t >= WSPAN
    alpha = 2.0 ** 0.25
    nch = t // CMP_STRIDE
    nsb = t // SLC_BLOCK
    nsbp = -(-nsb // LANES) * LANES

    gate0 = NSA_W + 6 * KV_W
    w_t = w_in.T.astype(BF16)
    tail0 = gate0 + N_GATES
    x2 = x[0]
    xb = x2.astype(BF16)
    tm, tn = PROJ_TM, PROJ_TN
    proj = functools.partial(_matmul, xb, w_t, trans_b=True)
    qp = proj(BF16, tm, tn, "proj_q", 0, NSA_W, scale=QSCALE)
    kcvc = proj(F32, tm, tn, "proj_kvc", NSA_W, 2 * KV_W)
    kv = proj(BF16, tm, tn, "proj_kv", NSA_W + 2 * KV_W, 4 * KV_W)
    gates = proj(F32, tm, LANES, "proj_gates", gate0, LANES)
    zuvz = proj(F32, tm, tn, "proj_zuvz", tail0, NSA_W + 3 * GM_W)
    qm = proj(BF16, tm, tn, "proj_qm", tail0 + NSA_W + 3 * GM_W, MEM_W)
    zm = proj(F32, tm, tn, "proj_zm", tail0 + NSA_W + 3 * GM_W + MEM_W, MEM_W)
    kvm = _matmul(mem[0].astype(BF16), w_mem_kv.astype(BF16), BF16, MEM_LEN, tn, "mem_kv")

    pe = jnp.stack([pe_cmp_k, pe_cmp_v])
    w1 = jnp.stack([w_cmp_k1, w_cmp_v1]).astype(BF16)
    w2 = jnp.stack([w_cmp_k2, w_cmp_v2]).astype(BF16)
    kvc = _compress(kcvc, pe, w1, w2)

    slopes_np = np.exp2(-8.0 * np.arange(1, N_HEADS + 1, dtype=np.float32) / N_HEADS
                        ).astype(np.float32).reshape(N_KV, HG)
    cslopes_np = (slopes_np * np.float32(np.log2(np.e))).astype(np.float32)
    cslopes = jnp.asarray(cslopes_np)
    sconst = jnp.asarray(_slope_lanes(cslopes_np))
    pool = jnp.asarray(_pool_matrix(nsbp, nch), BF16)

    tmap = jnp.asarray(_tile_map(nsbp), BF16)
    o_c, mb, tf = _cmp_attention(cslopes, qp, kvc[0], kvc[1], pool, tmap)
    tflags = tf[:, :, 0, :t // KT]
    tile_list, tile_cnt = _tile_lists(tflags, t)
    o_s = _sel_attention(tile_list, tile_cnt, cslopes, qp, kv, mb, jnp.asarray(_block_pick(), BF16),
                         jnp.asarray(_key_ext(KT, True), BF16), jnp.asarray(_ones_lane0(KT), BF16), sconst)
    o_w = _win_attention(qp, kv, jnp.asarray(_window_band(), F32),
                         jnp.asarray(_key_ext(WSPAN, False), BF16), jnp.asarray(_ones_lane0(WSPAN), BF16),
                         sconst)

    y = _mix(o_c, o_s, o_w, gates, zuvz, zm, qm, kvm,
             gm_ln_g.reshape(1, GM_W), gm_ln_b.reshape(1, GM_W), w_spatial, b_spatial.T)
    h = _matmul(y, w_out.astype(BF16), F32, PROJ_TM, PROJ_TN, "out_proj")
    out = _residual_ln(x2, h, ln_g.reshape(1, D_MODEL), ln_b.reshape(1, D_MODEL), alpha)
    return out[None]
```

```python
import functools

import numpy as np
import jax
import jax.numpy as jnp
from jax import lax
from jax.experimental import pallas as pl
from jax.experimental.pallas import tpu as pltpu

D_MODEL = 4096
DH = 128
N_HEADS = 16
HG = 4
N_KV = 4
NSA_W = N_HEADS * DH
KV_W = N_KV * DH
CMP_LEN = 32
CMP_STRIDE = 16
SLC_BLOCK = 64
TOP_N = 16
WINDOW = 512
QB = 128
CHUNK = 128
GM_W = 1024
GM_GROUPS = 4
GM_DG = GM_W // GM_GROUPS
MEM_W = 1024
MEM_LEN = 256
MEM_HEADS = 4
MEM_DH = MEM_W // MEM_HEADS
LN_EPS = 1e-5
N_GATES = N_HEADS * 3

LANES = 128
KT = 512
BLK_PER_KT = KT // SLC_BLOCK
OH_PERIOD = LANES // BLK_PER_KT
NEG = -2.0 ** 100
PROJ_TM = 1024
PROJ_TN = 1024
VMEM_LIMIT = 48 * 1024 * 1024

F32 = jnp.float32
BF16 = jnp.bfloat16
_NT = (((1,), (1,)), ((), ()))


def _params(n_axes):
    return pltpu.CompilerParams(dimension_semantics=("arbitrary",) * n_axes,
                                vmem_limit_bytes=VMEM_LIMIT)


def _mm_kernel(scale, trans_b, a_ref, b_ref, o_ref):
    if trans_b:
        acc = lax.dot_general(a_ref[...], b_ref[...], _NT, preferred_element_type=F32)
    else:
        acc = jnp.dot(a_ref[...], b_ref[...], preferred_element_type=F32)
    if scale is not None:
        acc = acc * scale
    o_ref[...] = acc.astype(o_ref.dtype)


def _matmul(a, b, out_dtype, tm, tn, name, col0=0, ncols=None, scale=None, trans_b=False):
    m, k = a.shape
    n_all = b.shape[0] if trans_b else b.shape[1]
    ncols = n_all - col0 if ncols is None else ncols
    tm = min(tm, m)
    assert ncols % tn == 0 and m % tm == 0
    if trans_b:
        assert col0 % 16 == 0
        b_spec = pl.BlockSpec((pl.Element(tn), pl.Element(k)),
                              lambda j, i: (pl.multiple_of(col0 + j * tn, 16), 0))
    else:
        assert col0 % tn == 0
        b_spec = pl.BlockSpec((k, tn), lambda j, i: (0, col0 // tn + j))
    return pl.pallas_call(
        functools.partial(_mm_kernel, scale, trans_b),
        grid=(ncols // tn, m // tm),
        in_specs=[pl.BlockSpec((tm, k), lambda j, i: (i, 0)), b_spec],
        out_specs=pl.BlockSpec((tm, tn), lambda j, i: (i, j)),
        out_shape=jax.ShapeDtypeStruct((m, ncols), out_dtype),
        compiler_params=_params(2),
        name=name,
    )(a, b)


def _gelu_tanh(x):
    c = np.float32(np.sqrt(2.0 / np.pi))
    return x * (0.5 * (1.0 + jnp.tanh(c * (x + 0.044715 * (x * x * x)))))


def _compress_kernel(x_ref, pe_ref, w1_ref, w2_ref, o_ref):
    nch = o_ref.shape[2]
    a = jnp.zeros((nch, DH), F32)
    b = jnp.zeros((nch, DH), F32)
    for r in range(CMP_STRIDE):
        xr = x_ref[pl.ds(r, nch, stride=CMP_STRIDE), :]
        lo = (xr + pe_ref[0, r:r + 1, :]).astype(BF16)
        hi = (xr + pe_ref[0, CMP_STRIDE + r:CMP_STRIDE + r + 1, :]).astype(BF16)
        a = a + jnp.dot(lo, w1_ref[0, r * DH:(r + 1) * DH, :], preferred_element_type=F32)
        b = b + jnp.dot(hi, w1_ref[0, (CMP_STRIDE + r) * DH:(CMP_STRIDE + r + 1) * DH, :],
                        preferred_element_type=F32)
    pre = a + pltpu.roll(b, nch - 1, axis=0)
    h = _gelu_tanh(pre).astype(BF16)
    o_ref[0, 0] = jnp.dot(h, w2_ref[0], preferred_element_type=F32).astype(o_ref.dtype)


def _compress(kcvc, pe, w1, w2):
    t = kcvc.shape[0]
    nch = t // CMP_STRIDE
    return pl.pallas_call(
        _compress_kernel,
        grid=(2, N_KV),
        in_specs=[pl.BlockSpec((t, DH), lambda s, j: (0, s * N_KV + j)),
                  pl.BlockSpec((1, CMP_LEN, DH), lambda s, j: (s, 0, 0)),
                  pl.BlockSpec((1, CMP_LEN * DH, DH), lambda s, j: (s, 0, 0)),
                  pl.BlockSpec((1, DH, DH), lambda s, j: (s, 0, 0))],
        out_specs=pl.BlockSpec((1, 1, nch, DH), lambda s, j: (s, j, 0, 0)),
        out_shape=jax.ShapeDtypeStruct((2, N_KV, nch, DH), BF16),
        compiler_params=_params(2),
        name="compress_kv",
    )(kcvc, pe, w1, w2)


def _stack_heads(q):
    return jnp.concatenate([q[:, h * DH:(h + 1) * DH] for h in range(HG)], axis=0)


def _store_heads(o_ref, o, blk=0, rows=QB):
    for h in range(HG):
        o_ref[blk * rows:(blk + 1) * rows, h * DH:(h + 1) * DH] = o[h * rows:(h + 1) * rows, :]


CMP_QBLOCKS = 4
CMP_COL_STEP = 256
FORCED_BLOCKS = 3
QSCALE = np.float32(DH ** -0.5 * np.log2(np.e))


def _cmp_scores(g, i, w, cs_ref, q, kc_ref, vc_ref, oc_ref, blk):
    q0 = i * QB
    rows_out = slice(blk * QB, (blk + 1) * QB)
    qs = _stack_heads(q)
    s = lax.dot_general(qs, kc_ref[0, 0:w, :], _NT, preferred_element_type=F32)
    cpos = lax.broadcasted_iota(jnp.int32, (1, w), 1) * CMP_STRIDE + (CMP_LEN - 1) - q0
    row = lax.broadcasted_iota(jnp.int32, (QB, 1), 0)
    maskb = jnp.where(cpos <= row, 0.0, NEG)
    cposf = cpos.astype(F32)
    row_ok = jnp.where(q0 + row >= CMP_LEN - 1, 1.0, 0.0)

    vc = vc_ref[0, 0:w, :]
    imp = jnp.zeros((QB, w), F32)
    for h in range(HG):
        t = s[h * QB:(h + 1) * QB] + cs_ref[g, h] * cposf + maskb
        m = jnp.max(t, axis=1, keepdims=True)
        p = jnp.exp2(t - m)
        l = jnp.sum(p, axis=1, keepdims=True)
        pn = p * (row_ok / jnp.maximum(l, 1e-30))
        imp = imp + pn
        oc_ref[rows_out, h * DH:(h + 1) * DH] = jnp.dot(pn.astype(BF16), vc, preferred_element_type=F32)
    return imp


def _cmp_select(i0, w, r, imp, pool_ref, tmap_ref, mb_ref, tf_ref):
    nq = imp.shape[0]
    nsbp = mb_ref.shape[1]
    n_sel = min(TOP_N, r)
    pool = pool_ref[0:r, 0:w]
    p1 = imp.astype(BF16)
    r1 = imp - p1.astype(F32)
    p2 = r1.astype(BF16)
    p3 = (r1 - p2.astype(F32)).astype(BF16)
    imps = (lax.dot_general(pool, p1, _NT, preferred_element_type=F32)
            + lax.dot_general(pool, p2, _NT, preferred_element_type=F32)
            + lax.dot_general(pool, p3, _NT, preferred_element_type=F32))

    j = lax.broadcasted_iota(jnp.int32, (r, nq), 0)
    tq = i0 * QB + lax.broadcasted_iota(jnp.int32, (r, nq), 1)
    cur = lax.shift_right_arithmetic(tq, SLC_BLOCK.bit_length() - 1)
    forced = jnp.where(j == 0, 1, jnp.where(j == cur, 1, jnp.where(j == cur - 1, 1, 0)))
    sc = jnp.where(j > cur, -1.0, jnp.where(forced == 1, -2.0, imps))
    for _ in range(n_sel - FORCED_BLOCKS):
        mx = jnp.max(sc, axis=0, keepdims=True)
        idx = jnp.min(jnp.where(sc == mx, j, r), axis=0, keepdims=True)
        sc = jnp.where(j == idx, -2.0, sc)
    bias = jnp.where(j <= cur, jnp.where(sc == -2.0, 0.0, NEG), NEG)
    for blk in range(nq // QB):
        rows_out = slice(blk * QB, (blk + 1) * QB)
        bias_t = bias[:, rows_out].T
        mb_ref[rows_out, 0:r] = bias_t.astype(BF16)
        if r < nsbp:
            mb_ref[rows_out, r:nsbp] = jnp.full((QB, nsbp - r), NEG, BF16)
        anyq = jnp.max(jnp.where(bias_t == 0.0, 1.0, 0.0), axis=0, keepdims=True)
        hit = jnp.dot(jnp.broadcast_to(anyq, (8, r)).astype(BF16), tmap_ref[0:r, :],
                      preferred_element_type=F32)
        tf_ref[0, blk] = jnp.where(hit > 0.0, 1, 0).astype(jnp.int32)


def _cmp_kernel(cs_ref, q_ref, kc_ref, vc_ref, pool_ref, tmap_ref, oc_ref, mb_ref, tf_ref):
    g = pl.program_id(0)
    i0 = pl.program_id(1) * CMP_QBLOCKS
    nch = kc_ref.shape[1]
    nsbp = pool_ref.shape[0]
    step = min(CMP_COL_STEP, nch)
    need = (i0 + CMP_QBLOCKS - 1) * (QB // CMP_STRIDE) + (QB - CMP_LEN) // CMP_STRIDE
    variant = lax.shift_right_logical(need, step.bit_length() - 1)
    for v in range(nch // step):
        w = (v + 1) * step
        r = min(nsbp, -(-(w // (SLC_BLOCK // CMP_STRIDE)) // LANES) * LANES)

        @pl.when(variant == v)
        def _(w=w, r=r):
            imps = [_cmp_scores(g, i0 + blk, w, cs_ref, q_ref[blk * QB:(blk + 1) * QB, :], kc_ref, vc_ref,
                                oc_ref, blk) for blk in range(CMP_QBLOCKS)]
            _cmp_select(i0, w, r, jnp.concatenate(imps, axis=0), pool_ref, tmap_ref, mb_ref, tf_ref)


def _cmp_attention(cslopes, qf, kc, vc, pool, tmap):
    t = qf.shape[0]
    nch = kc.shape[1]
    nsbp = pool.shape[0]
    rows = CMP_QBLOCKS * QB
    return pl.pallas_call(
        _cmp_kernel,
        grid=(N_KV, t // rows),
        in_specs=[pl.BlockSpec(memory_space=pltpu.SMEM),
                  pl.BlockSpec((rows, HG * DH), lambda g, i: (i, g)),
                  pl.BlockSpec((1, nch, DH), lambda g, i: (g, 0, 0)),
                  pl.BlockSpec((1, nch, DH), lambda g, i: (g, 0, 0)),
                  pl.BlockSpec((nsbp, nch), lambda g, i: (0, 0)),
                  pl.BlockSpec((nsbp, LANES), lambda g, i: (0, 0))],
        out_specs=[pl.BlockSpec((rows, HG * DH), lambda g, i: (i, g)),
                   pl.BlockSpec((rows, nsbp), lambda g, i: (i, g)),
                   pl.BlockSpec((1, CMP_QBLOCKS, 8, LANES), lambda g, i: (g, i, 0, 0))],
        out_shape=[jax.ShapeDtypeStruct((t, NSA_W), F32),
                   jax.ShapeDtypeStruct((t, N_KV * nsbp), BF16),
                   jax.ShapeDtypeStruct((N_KV, t // QB, 8, LANES), jnp.int32)],
        compiler_params=_params(2),
        name="cmp_attn_select",
    )(cslopes, qf, kc, vc, pool, tmap)


SEL_QB = 256
EXT_LO0 = BLK_PER_KT
EXT_HI0 = BLK_PER_KT + 3


def _aug_queries(qs, ext, sconst_ref, rows=QB):
    parts = []
    for h in range(HG):
        e = (ext + sconst_ref[0, h:h + 1, :]).astype(BF16)
        parts.append(jnp.concatenate([qs[h * rows:(h + 1) * rows], e], axis=1))
    return jnp.concatenate(parts, axis=0)


def _sel_kernel(lst_ref, cnt_ref, cs_ref, q_ref, ks_ref, vs_ref, mb_ref, pick_ref, kx_ref, vx_ref,
                sconst_ref, o_ref, m_scr, acc_scr, u_scr, p_scr, al_scr):
    g = pl.program_id(0)
    i = pl.program_id(1)
    q0 = i * SEL_QB
    nhalf = mb_ref.shape[1] // LANES
    n_prev = cnt_ref[g, i]

    qs = _stack_heads(q_ref[...])
    m_scr[...] = jnp.full(m_scr.shape, NEG, F32)
    acc_scr[...] = jnp.zeros(acc_scr.shape, F32)
    p_scr[1] = jnp.zeros(p_scr.shape[1:], BF16)
    al_scr[1] = jnp.ones(al_scr.shape[1:], F32)

    def score_stage(n, slot):
        tn = lst_ref[g, i, n]
        k0 = pl.multiple_of(tn * KT, KT)
        kaug = jnp.concatenate([ks_ref[pl.ds(k0, KT), :], kx_ref[...]], axis=1)
        half = lax.shift_right_logical(tn, OH_PERIOD.bit_length() - 1)
        mbh = mb_ref[:, 0:LANES]
        for hf in range(1, nhalf):
            mbh = jnp.where(half == hf, mb_ref[:, hf * LANES:(hf + 1) * LANES], mbh)
        ext = jnp.dot(mbh, pick_ref[tn & (OH_PERIOD - 1)], preferred_element_type=F32)
        u_scr[slot] = lax.dot_general(_aug_queries(qs, ext, sconst_ref, SEL_QB), kaug, _NT,
                                      preferred_element_type=F32)

    def softmax_stage(n, slot, causal):
        k0 = lst_ref[g, i, n] * KT
        dq = (q0 - k0).astype(F32)
        if causal:
            visible = (lax.broadcasted_iota(jnp.int32, (SEL_QB, KT), 0)
                       - lax.broadcasted_iota(jnp.int32, (SEL_QB, KT), 1) + (q0 - k0)) >= 0
        for h in range(HG):
            rows = pl.ds(h * SEL_QB, SEL_QB)
            uh = u_scr[slot, rows, :]
            if causal:
                uh = jnp.where(visible, uh, NEG)
            c = cs_ref[g, h] * dq
            m_prev = m_scr[rows, :]
            m_next = jnp.maximum(m_prev, jnp.max(uh, axis=1, keepdims=True) - c)
            p_scr[slot, rows, :] = jnp.exp2(uh - (m_next + c)).astype(BF16)
            al_scr[slot, rows, :] = jnp.exp2(m_prev - m_next)
            m_scr[rows, :] = m_next

    def pv_stage(n, slot):
        k0 = pl.multiple_of(lst_ref[g, i, jnp.maximum(n, 0)] * KT, KT)
        vaug = jnp.concatenate([vs_ref[pl.ds(k0, KT), :], vx_ref[...]], axis=1)
        pv = jnp.dot(p_scr[slot], vaug, preferred_element_type=F32)
        acc_scr[...] = al_scr[slot] * acc_scr[...] + pv

    def finish():
        acc = acc_scr[...]
        _store_heads(o_ref, acc[:, 0:DH] / acc[:, DH:DH + 1], rows=SEL_QB)

    score_stage(0, 0)

    def pair(j, carry):
        n = 2 * j
        score_stage(n + 1, 1)
        softmax_stage(n, 0, False)
        pv_stage(n - 1, 1)
        score_stage(n + 2, 0)
        softmax_stage(n + 1, 1, False)
        pv_stage(n, 0)
        return carry

    lax.fori_loop(0, lax.shift_right_logical(n_prev, 1), pair, 0)
    n_even = n_prev & ~1

    @pl.when(n_prev == n_even)
    def _():
        softmax_stage(n_even, 0, True)
        pv_stage(n_even - 1, 1)
        pv_stage(n_even, 0)
        finish()

    @pl.when(n_prev != n_even)
    def _():
        score_stage(n_even + 1, 1)
        softmax_stage(n_even, 0, False)
        pv_stage(n_even - 1, 1)
        softmax_stage(n_even + 1, 1, True)
        pv_stage(n_even, 0)
        pv_stage(n_even + 1, 1)
        finish()


def _sel_attention(tile_list, tile_cnt, cslopes, qf, projb, mb, pick, kx, vx, sconst):
    t = qf.shape[0]
    nsbp = mb.shape[1] // N_KV
    grid_spec = pltpu.PrefetchScalarGridSpec(
        num_scalar_prefetch=2,
        grid=(N_KV, t // SEL_QB),
        in_specs=[pl.BlockSpec(memory_space=pltpu.SMEM),
                  pl.BlockSpec((SEL_QB, HG * DH), lambda g, i, *_: (i, g)),
                  pl.BlockSpec((t, DH), lambda g, i, *_: (0, g)),
                  pl.BlockSpec((t, DH), lambda g, i, *_: (0, N_KV + g)),
                  pl.BlockSpec((SEL_QB, nsbp), lambda g, i, *_: (i, g)),
                  pl.BlockSpec((OH_PERIOD, LANES, LANES), lambda g, i, *_: (0, 0, 0)),
                  pl.BlockSpec((KT, LANES), lambda g, i, *_: (0, 0)),
                  pl.BlockSpec((KT, LANES), lambda g, i, *_: (0, 0)),
                  pl.BlockSpec((1, HG, LANES), lambda g, i, *_: (g, 0, 0))],
        out_specs=pl.BlockSpec((SEL_QB, HG * DH), lambda g, i, *_: (i, g)),
        scratch_shapes=[pltpu.VMEM((HG * SEL_QB, 1), F32),
                        pltpu.VMEM((HG * SEL_QB, 2 * LANES), F32),
                        pltpu.VMEM((2, HG * SEL_QB, KT), F32),
                        pltpu.VMEM((2, HG * SEL_QB, KT), BF16),
                        pltpu.VMEM((2, HG * SEL_QB, 1), F32)])
    return pl.pallas_call(
        _sel_kernel,
        grid_spec=grid_spec,
        out_shape=jax.ShapeDtypeStruct((t, NSA_W), F32),
        compiler_params=_params(2),
        name="sel_attn",
    )(tile_list, tile_cnt, cslopes, qf, projb, projb, mb, pick, kx, vx, sconst)


def _tile_lists(tflags, t):
    nt = t // KT
    nq = t // SEL_QB
    tflags = jnp.max(tflags.reshape(N_KV, nq, SEL_QB // QB, nt), axis=2)
    last = (jnp.arange(nq, dtype=jnp.int32) * SEL_QB + SEL_QB - 1) // KT
    ids = jnp.arange(nt, dtype=jnp.int32)
    active = (tflags != 0) & (ids[None, None, :] < last[None, :, None])
    cnt = jnp.sum(active, axis=-1).astype(jnp.int32)
    order = jnp.argsort(jnp.where(active, ids, nt + ids), axis=-1).astype(jnp.int32)
    lst = jnp.where(ids[None, None, :] == cnt[..., None], last[None, :, None], order)
    return lst, cnt


WSPAN = WINDOW + QB


WIN_QBLOCKS = 2


def _win_kernel(q_ref, kw_ref, vw_ref, band_ref, kx_ref, vx_ref, sconst_ref, o_ref):
    for blk in range(WIN_QBLOCKS):
        q0 = (pl.program_id(1) * WIN_QBLOCKS + blk) * QB
        start = pl.multiple_of(jnp.maximum(q0 - WINDOW, 0), QB)

        qs = _stack_heads(q_ref[blk * QB:(blk + 1) * QB, :])
        kaug = jnp.concatenate([kw_ref[pl.ds(start, WSPAN), :], kx_ref[...]], axis=1)
        vaug = jnp.concatenate([vw_ref[pl.ds(start, WSPAN), :], vx_ref[...]], axis=1)
        u = lax.dot_general(_aug_queries(qs, jnp.zeros((QB, LANES), F32), sconst_ref), kaug, _NT,
                            preferred_element_type=F32)
        band = band_ref[lax.shift_right_logical(q0 - start, QB.bit_length() - 1)]
        ps = []
        for h in range(HG):
            uh = u[h * QB:(h + 1) * QB] + band
            m = jnp.max(uh, axis=1, keepdims=True)
            ps.append(jnp.exp2(uh - m).astype(BF16))
        pv = jnp.dot(jnp.concatenate(ps, axis=0), vaug, preferred_element_type=F32)
        _store_heads(o_ref, pv[:, 0:DH] / pv[:, DH:DH + 1], blk)


def _win_attention(qf, projb, band, kxw, vxw, sconst):
    t = qf.shape[0]
    rows = WIN_QBLOCKS * QB
    return pl.pallas_call(
        _win_kernel,
        grid=(N_KV, t // rows),
        in_specs=[pl.BlockSpec((rows, HG * DH), lambda g, i: (i, g)),
                  pl.BlockSpec((t, DH), lambda g, i: (0, 2 * N_KV + g)),
                  pl.BlockSpec((t, DH), lambda g, i: (0, 3 * N_KV + g)),
                  pl.BlockSpec((WINDOW // QB + 1, QB, WSPAN), lambda g, i: (0, 0, 0)),
                  pl.BlockSpec((WSPAN, LANES), lambda g, i: (0, 0)),
                  pl.BlockSpec((WSPAN, LANES), lambda g, i: (0, 0)),
                  pl.BlockSpec((1, HG, LANES), lambda g, i: (g, 0, 0))],
        out_specs=pl.BlockSpec((rows, HG * DH), lambda g, i: (i, g)),
        out_shape=jax.ShapeDtypeStruct((t, NSA_W), F32),
        compiler_params=_params(2),
        name="win_attn",
    )(qf, projb, projb, band, kxw, vxw, sconst)


def _silu(z):
    return z * jax.nn.sigmoid(z)


def _mix_kernel(oc_ref, os_ref, ow_ref, gates_ref, zn_ref, u_ref, v_ref, zg_ref, zm_ref,
                qm_ref, kvm_ref, lng_ref, lnb_ref, wsp_ref, bsp_ref, y_ref):
    sig = jax.nn.sigmoid(gates_ref[...])
    for h in range(N_HEADS):
        sl = slice(h * DH, (h + 1) * DH)
        comb = (sig[:, 3 * h:3 * h + 1] * oc_ref[:, sl]
                + sig[:, 3 * h + 1:3 * h + 2] * os_ref[:, sl]
                + sig[:, 3 * h + 2:3 * h + 3] * ow_ref[:, sl])
        y_ref[:, sl] = (comb * _silu(zn_ref[:, sl])).astype(y_ref.dtype)

    vv = v_ref[...]
    mu = jnp.mean(vv, axis=-1, keepdims=True)
    var = jnp.mean(jnp.square(vv - mu), axis=-1, keepdims=True)
    vn = ((vv - mu) * lax.rsqrt(var + LN_EPS) * lng_ref[...] + lnb_ref[...]).astype(BF16)
    causal = (lax.broadcasted_iota(jnp.int32, (CHUNK, CHUNK), 0)
              >= lax.broadcasted_iota(jnp.int32, (CHUNK, CHUNK), 1))
    for gg in range(GM_GROUPS):
        sl = slice(gg * GM_DG, (gg + 1) * GM_DG)
        ws = jnp.where(causal, wsp_ref[gg], 0.0).astype(BF16)
        sv = jnp.dot(ws, vn[:, sl], preferred_element_type=F32) + bsp_ref[:, gg:gg + 1]
        y_ref[:, NSA_W + gg * GM_DG:NSA_W + (gg + 1) * GM_DG] = (
            u_ref[:, sl] * sv * _silu(zg_ref[:, sl])).astype(y_ref.dtype)

    mscale = np.float32(MEM_DH ** -0.5)
    for h in range(MEM_HEADS):
        sl = slice(h * MEM_DH, (h + 1) * MEM_DH)
        km = kvm_ref[:, h * MEM_DH:(h + 1) * MEM_DH]
        vm = kvm_ref[:, MEM_W + h * MEM_DH:MEM_W + (h + 1) * MEM_DH]
        s = lax.dot_general(qm_ref[:, sl], km, _NT, preferred_element_type=F32) * mscale
        m = jnp.max(s, axis=-1, keepdims=True)
        e = jnp.exp(s - m)
        p = e / jnp.sum(e, axis=-1, keepdims=True)
        o = jnp.dot(p.astype(BF16), vm, preferred_element_type=F32)
        y_ref[:, NSA_W + GM_W + h * MEM_DH:NSA_W + GM_W + (h + 1) * MEM_DH] = (
            o * _silu(zm_ref[:, sl])).astype(y_ref.dtype)


def _mix(oc, osel, ow, gates, zuvz, zm, qm, kvm, lng, lnb, wsp, bsp_t):
    t = oc.shape[0]
    row = lambda c: (lambda i: (i, c))
    full2 = lambda i: (0, 0)
    return pl.pallas_call(
        _mix_kernel,
        grid=(t // QB,),
        in_specs=[pl.BlockSpec((QB, NSA_W), row(0)),
                  pl.BlockSpec((QB, NSA_W), row(0)),
                  pl.BlockSpec((QB, NSA_W), row(0)),
                  pl.BlockSpec((QB, LANES), row(0)),
                  pl.BlockSpec((QB, NSA_W), row(0)),
                  pl.BlockSpec((QB, GM_W), row(2)),
                  pl.BlockSpec((QB, GM_W), row(3)),
                  pl.BlockSpec((QB, GM_W), row(4)),
                  pl.BlockSpec((QB, MEM_W), row(0)),
                  pl.BlockSpec((QB, MEM_W), row(0)),
                  pl.BlockSpec((MEM_LEN, 2 * MEM_W), full2),
                  pl.BlockSpec((1, GM_W), full2),
                  pl.BlockSpec((1, GM_W), full2),
                  pl.BlockSpec((GM_GROUPS, CHUNK, CHUNK), lambda i: (0, 0, 0)),
                  pl.BlockSpec((CHUNK, GM_GROUPS), full2)],
        out_specs=pl.BlockSpec((QB, D_MODEL), row(0)),
        out_shape=jax.ShapeDtypeStruct((t, D_MODEL), BF16),
        compiler_params=_params(1),
        name="mix_heads",
    )(oc, osel, ow, gates, zuvz, zuvz, zuvz, zuvz, zm, qm, kvm, lng, lnb, wsp, bsp_t)


def _ln_kernel(alpha, x_ref, h_ref, g_ref, b_ref, o_ref):
    z = alpha * x_ref[...] + h_ref[...]
    mu = jnp.mean(z, axis=-1, keepdims=True)
    var = jnp.mean(jnp.square(z - mu), axis=-1, keepdims=True)
    o_ref[...] = (z - mu) * lax.rsqrt(var + LN_EPS) * g_ref[...] + b_ref[...]


def _residual_ln(x2, h, g, b, alpha, tm=256):
    t, d = x2.shape
    tm = min(tm, t)
    return pl.pallas_call(
        functools.partial(_ln_kernel, np.float32(alpha)),
        grid=(t // tm,),
        in_specs=[pl.BlockSpec((tm, d), lambda i: (i, 0)),
                  pl.BlockSpec((tm, d), lambda i: (i, 0)),
                  pl.BlockSpec((1, d), lambda i: (0, 0)),
                  pl.BlockSpec((1, d), lambda i: (0, 0))],
        out_specs=pl.BlockSpec((tm, d), lambda i: (i, 0)),
        out_shape=jax.ShapeDtypeStruct((t, d), F32),
        compiler_params=_params(1),
        name="residual_ln",
    )(x2, h, g, b)


def _pool_matrix(nsbp, nch):
    j = np.arange(nsbp)[:, None]
    c = np.arange(nch)[None, :]
    ratio = SLC_BLOCK // CMP_STRIDE
    return ((c >= ratio * j - 1) & (c <= ratio * j + ratio - 1)).astype(np.float32)


def _tile_map(nsbp):
    j = np.arange(nsbp)[:, None]
    n = np.arange(LANES)[None, :]
    return (j // BLK_PER_KT == n).astype(np.float32)


def _block_pick():
    p = np.arange(OH_PERIOD)[:, None, None]
    j = np.arange(LANES)[None, :, None]
    c = np.arange(LANES)[None, None, :]
    return ((c < BLK_PER_KT) & (j == BLK_PER_KT * p + c)).astype(np.float32)


def _key_ext(n_keys, with_blocks):
    l = np.arange(n_keys)[:, None]
    c = np.arange(LANES)[None, :]
    kx = np.zeros((n_keys, LANES), np.float32)
    if with_blocks:
        kx += (c < BLK_PER_KT) & (l // SLC_BLOCK == c)
    kx += np.where((c >= EXT_LO0) & (c < EXT_LO0 + 3), l % 256, 0)
    kx += np.where((c >= EXT_HI0) & (c < EXT_HI0 + 3), l // 256, 0)
    return kx


def _ones_lane0(n_keys):
    vx = np.zeros((n_keys, LANES), np.float32)
    vx[:, 0] = 1.0
    return vx


def _window_band():
    d = np.arange(WINDOW // QB + 1)[:, None, None] * QB
    dist = d + np.arange(QB)[None, :, None] - np.arange(WSPAN)[None, None, :]
    return np.where((dist >= 0) & (dist < WINDOW), 0.0, NEG).astype(np.float32)


def _slope_lanes(cslopes):
    to_bf16 = lambda v: v.astype(BF16).astype(np.float32)
    s1 = to_bf16(cslopes)
    r1 = cslopes - s1
    s2 = to_bf16(r1)
    s3 = to_bf16(r1 - s2)
    pieces = np.stack([s1, s2, s3, 256.0 * s1, 256.0 * s2, 256.0 * s3], axis=-1)
    return np.pad(pieces, ((0, 0), (0, 0), (EXT_LO0, LANES - EXT_LO0 - 6))).astype(np.float32)


def kernel(x, mem, w_in, w_cmp_k1, w_cmp_k2, w_cmp_v1, w_cmp_v2, pe_cmp_k, pe_cmp_v,
           gm_ln_g, gm_ln_b, w_spatial, b_spatial, w_mem_kv, w_out, ln_g, ln_b):
    b, t, _ = x.shape
    assert b == 1 and t % KT == 0 and t >= WSPAN
    alpha = 2.0 ** 0.25
    nch = t // CMP_STRIDE
    nsb = t // SLC_BLOCK
    nsbp = -(-nsb // LANES) * LANES

    gate0 = NSA_W + 6 * KV_W
    w_t = w_in.T.astype(BF16)
    tail0 = gate0 + N_GATES
    x2 = x[0]
    xb = x2.astype(BF16)
    tm, tn = PROJ_TM, PROJ_TN
    proj = functools.partial(_matmul, xb, w_t, trans_b=True)
    qp = proj(BF16, tm, tn, "proj_q", 0, NSA_W, scale=QSCALE)
    kcvc = proj(F32, tm, tn, "proj_kvc", NSA_W, 2 * KV_W)
    kv = proj(BF16, tm, tn, "proj_kv", NSA_W + 2 * KV_W, 4 * KV_W)
    gates = proj(F32, tm, LANES, "proj_gates", gate0, LANES)
    zuvz = proj(F32, tm, tn, "proj_zuvz", tail0, NSA_W + 3 * GM_W)
    qm = proj(BF16, tm, tn, "proj_qm", tail0 + NSA_W + 3 * GM_W, MEM_W)
    zm = proj(F32, tm, tn, "proj_zm", tail0 + NSA_W + 3 * GM_W + MEM_W, MEM_W)
    kvm = _matmul(mem[0].astype(BF16), w_mem_kv.astype(BF16), BF16, MEM_LEN, tn, "mem_kv")

    pe = jnp.stack([pe_cmp_k, pe_cmp_v])
    w1 = jnp.stack([w_cmp_k1, w_cmp_v1]).astype(BF16)
    w2 = jnp.stack([w_cmp_k2, w_cmp_v2]).astype(BF16)
    kvc = _compress(kcvc, pe, w1, w2)

    slopes_np = np.exp2(-8.0 * np.arange(1, N_HEADS + 1, dtype=np.float32) / N_HEADS
                        ).astype(np.float32).reshape(N_KV, HG)
    cslopes_np = (slopes_np * np.float32(np.log2(np.e))).astype(np.float32)
    cslopes = jnp.asarray(cslopes_np)
    sconst = jnp.asarray(_slope_lanes(cslopes_np))
    pool = jnp.asarray(_pool_matrix(nsbp, nch), BF16)

    tmap = jnp.asarray(_tile_map(nsbp), BF16)
    o_c, mb, tf = _cmp_attention(cslopes, qp, kvc[0], kvc[1], pool, tmap)
    tflags = tf[:, :, 0, :t // KT]
    tile_list, tile_cnt = _tile_lists(tflags, t)
    o_s = _sel_attention(tile_list, tile_cnt, cslopes, qp, kv, mb, jnp.asarray(_block_pick(), BF16),
                         jnp.asarray(_key_ext(KT, True), BF16), jnp.asarray(_ones_lane0(KT), BF16), sconst)
    o_w = _win_attention(qp, kv, jnp.asarray(_window_band(), F32),
                         jnp.asarray(_key_ext(WSPAN, False), BF16), jnp.asarray(_ones_lane0(WSPAN), BF16),
                         sconst)

    y = _mix(o_c, o_s, o_w, gates, zuvz, zm, qm, kvm,
             gm_ln_g.reshape(1, GM_W), gm_ln_b.reshape(1, GM_W), w_spatial, b_spatial.T)
    h = _matmul(y, w_out.astype(BF16), F32, PROJ_TM, PROJ_TN, "out_proj")
    out = _residual_ln(x2, h, ln_g.reshape(1, D_MODEL), ln_b.reshape(1, D_MODEL), alpha)
    return out[None]
```

```python
import functools

import numpy as np
import jax
import jax.numpy as jnp
from jax import lax
from jax.experimental import pallas as pl
from jax.experimental.pallas import tpu as pltpu

D_MODEL = 4096
DH = 128
N_HEADS = 16
HG = 4
N_KV = 4
NSA_W = N_HEADS * DH
KV_W = N_KV * DH
CMP_LEN = 32
CMP_STRIDE = 16
SLC_BLOCK = 64
TOP_N = 16
WINDOW = 512
QB = 128
CHUNK = 128
GM_W = 1024
GM_GROUPS = 4
GM_DG = GM_W // GM_GROUPS
MEM_W = 1024
MEM_LEN = 256
MEM_HEADS = 4
MEM_DH = MEM_W // MEM_HEADS
LN_EPS = 1e-5
N_GATES = N_HEADS * 3

LANES = 128
KT = 512
BLK_PER_KT = KT // SLC_BLOCK
OH_PERIOD = LANES // BLK_PER_KT
NEG = -2.0 ** 100
PROJ_TM = 1024
PROJ_TN = 1024
VMEM_LIMIT = 48 * 1024 * 1024

F32 = jnp.float32
BF16 = jnp.bfloat16
_NT = (((1,), (1,)), ((), ()))


def _params(n_axes):
    return pltpu.CompilerParams(dimension_semantics=("arbitrary",) * n_axes,
                                vmem_limit_bytes=VMEM_LIMIT)


def _mm_kernel(scale, trans_b, a_ref, b_ref, o_ref):
    if trans_b:
        acc = lax.dot_general(a_ref[...], b_ref[...], _NT, preferred_element_type=F32)
    else:
        acc = jnp.dot(a_ref[...], b_ref[...], preferred_element_type=F32)
    if scale is not None:
        acc = acc * scale
    o_ref[...] = acc.astype(o_ref.dtype)


def _matmul(a, b, out_dtype, tm, tn, name, col0=0, ncols=None, scale=None, trans_b=False):
    m, k = a.shape
    n_all = b.shape[0] if trans_b else b.shape[1]
    ncols = n_all - col0 if ncols is None else ncols
    tm = min(tm, m)
    assert ncols % tn == 0 and m % tm == 0
    if trans_b:
        assert col0 % 16 == 0
        b_spec = pl.BlockSpec((pl.Element(tn), pl.Element(k)),
                              lambda j, i: (pl.multiple_of(col0 + j * tn, 16), 0))
    else:
        assert col0 % tn == 0
        b_spec = pl.BlockSpec((k, tn), lambda j, i: (0, col0 // tn + j))
    return pl.pallas_call(
        functools.partial(_mm_kernel, scale, trans_b),
        grid=(ncols // tn, m // tm),
        in_specs=[pl.BlockSpec((tm, k), lambda j, i: (i, 0)), b_spec],
        out_specs=pl.BlockSpec((tm, tn), lambda j, i: (i, j)),
        out_shape=jax.ShapeDtypeStruct((m, ncols), out_dtype),
        compiler_params=_params(2),
        name=name,
    )(a, b)


def _gelu_tanh(x):
    c = np.float32(np.sqrt(2.0 / np.pi))
    return x * (0.5 * (1.0 + jnp.tanh(c * (x + 0.044715 * (x * x * x)))))


def _compress_kernel(x_ref, pe_ref, w1_ref, w2_ref, o_ref):
    nch = o_ref.shape[2]
    a = jnp.zeros((nch, DH), F32)
    b = jnp.zeros((nch, DH), F32)
    for r in range(CMP_STRIDE):
        xr = x_ref[pl.ds(r, nch, stride=CMP_STRIDE), :]
        lo = (xr + pe_ref[0, r:r + 1, :]).astype(BF16)
        hi = (xr + pe_ref[0, CMP_STRIDE + r:CMP_STRIDE + r + 1, :]).astype(BF16)
        a = a + jnp.dot(lo, w1_ref[0, r * DH:(r + 1) * DH, :], preferred_element_type=F32)
        b = b + jnp.dot(hi, w1_ref[0, (CMP_STRIDE + r) * DH:(CMP_STRIDE + r + 1) * DH, :],
                        preferred_element_type=F32)
    pre = a + pltpu.roll(b, nch - 1, axis=0)
    h = _gelu_tanh(pre).astype(BF16)
    o_ref[0, 0] = jnp.dot(h, w2_ref[0], preferred_element_type=F32).astype(o_ref.dtype)


def _compress(kcvc, pe, w1, w2):
    t = kcvc.shape[0]
    nch = t // CMP_STRIDE
    return pl.pallas_call(
        _compress_kernel,
        grid=(2, N_KV),
        in_specs=[pl.BlockSpec((t, DH), lambda s, j: (0, s * N_KV + j)),
                  pl.BlockSpec((1, CMP_LEN, DH), lambda s, j: (s, 0, 0)),
                  pl.BlockSpec((1, CMP_LEN * DH, DH), lambda s, j: (s, 0, 0)),
                  pl.BlockSpec((1, DH, DH), lambda s, j: (s, 0, 0))],
        out_specs=pl.BlockSpec((1, 1, nch, DH), lambda s, j: (s, j, 0, 0)),
        out_shape=jax.ShapeDtypeStruct((2, N_KV, nch, DH), BF16),
        compiler_params=_params(2),
        name="compress_kv",
    )(kcvc, pe, w1, w2)


def _stack_heads(q):
    return jnp.concatenate([q[:, h * DH:(h + 1) * DH] for h in range(HG)], axis=0)


def _store_heads(o_ref, o, blk=0, rows=QB):
    for h in range(HG):
        o_ref[blk * rows:(blk + 1) * rows, h * DH:(h + 1) * DH] = o[h * rows:(h + 1) * rows, :]


CMP_QBLOCKS = 4
CMP_COL_STEP = 256
FORCED_BLOCKS = 3
QSCALE = np.float32(DH ** -0.5 * np.log2(np.e))


def _cmp_scores(g, i0, w, cs_ref, q_ref, kc_ref, vc_ref, oc_ref):
    blocks = range(CMP_QBLOCKS)
    kc = kc_ref[0, 0:w, :]
    vc = vc_ref[0, 0:w, :]
    row = lax.broadcasted_iota(jnp.int32, (QB, 1), 0)
    ss, maskbs, cposfs, row_oks = [], [], [], []
    for blk in blocks:
        q0 = (i0 + blk) * QB
        qs = _stack_heads(q_ref[blk * QB:(blk + 1) * QB, :])
        ss.append(lax.dot_general(qs, kc, _NT, preferred_element_type=F32))
        cpos = lax.broadcasted_iota(jnp.int32, (1, w), 1) * CMP_STRIDE + (CMP_LEN - 1) - q0
        maskbs.append(jnp.where(cpos <= row, 0.0, NEG))
        cposfs.append(cpos.astype(F32))
        row_oks.append(jnp.where(q0 + row >= CMP_LEN - 1, 1.0, 0.0))

    imps = [jnp.zeros((QB, w), F32) for _ in blocks]
    for h in range(HG):
        for blk in blocks:
            t = ss[blk][h * QB:(h + 1) * QB] + cs_ref[g, h] * cposfs[blk] + maskbs[blk]
            m = jnp.max(t, axis=1, keepdims=True)
            p = jnp.exp2(t - m)
            l = jnp.sum(p, axis=1, keepdims=True)
            pn = p * (row_oks[blk] / jnp.maximum(l, 1e-30))
            imps[blk] = imps[blk] + pn
            oc_ref[blk * QB:(blk + 1) * QB, h * DH:(h + 1) * DH] = jnp.dot(
                pn.astype(BF16), vc, preferred_element_type=F32)
    return jnp.concatenate(imps, axis=0)


def _cmp_select(i0, w, r, imp, pool_ref, tmap_ref, mb_ref, tf_ref):
    nq = imp.shape[0]
    nsbp = mb_ref.shape[1]
    n_sel = min(TOP_N, r)
    pool = pool_ref[0:r, 0:w]
    p1 = imp.astype(BF16)
    r1 = imp - p1.astype(F32)
    p2 = r1.astype(BF16)
    p3 = (r1 - p2.astype(F32)).astype(BF16)
    imps = (lax.dot_general(pool, p1, _NT, preferred_element_type=F32)
            + lax.dot_general(pool, p2, _NT, preferred_element_type=F32)
            + lax.dot_general(pool, p3, _NT, preferred_element_type=F32))

    j = lax.broadcasted_iota(jnp.int32, (r, nq), 0)
    tq = i0 * QB + lax.broadcasted_iota(jnp.int32, (r, nq), 1)
    cur = lax.shift_right_arithmetic(tq, SLC_BLOCK.bit_length() - 1)
    forced = jnp.where(j == 0, 1, jnp.where(j == cur, 1, jnp.where(j == cur - 1, 1, 0)))
    sc = jnp.where(j > cur, -1.0, jnp.where(forced == 1, -2.0, imps))
    for _ in range(n_sel - FORCED_BLOCKS):
        mx = jnp.max(sc, axis=0, keepdims=True)
        idx = jnp.min(jnp.where(sc == mx, j, r), axis=0, keepdims=True)
        sc = jnp.where(j == idx, -2.0, sc)
    bias = jnp.where(j <= cur, jnp.where(sc == -2.0, 0.0, NEG), NEG)
    for blk in range(nq // QB):
        rows_out = slice(blk * QB, (blk + 1) * QB)
        bias_t = bias[:, rows_out].T
        mb_ref[rows_out, 0:r] = bias_t.astype(BF16)
        if r < nsbp:
            mb_ref[rows_out, r:nsbp] = jnp.full((QB, nsbp - r), NEG, BF16)
        anyq = jnp.max(jnp.where(bias_t == 0.0, 1.0, 0.0), axis=0, keepdims=True)
        hit = jnp.dot(jnp.broadcast_to(anyq, (8, r)).astype(BF16), tmap_ref[0:r, :],
                      preferred_element_type=F32)
        tf_ref[0, blk] = jnp.where(hit > 0.0, 1, 0).astype(jnp.int32)


def _cmp_kernel(cs_ref, q_ref, kc_ref, vc_ref, pool_ref, tmap_ref, oc_ref, mb_ref, tf_ref):
    g = pl.program_id(0)
    i0 = pl.program_id(1) * CMP_QBLOCKS
    nch = kc_ref.shape[1]
    nsbp = pool_ref.shape[0]
    step = min(CMP_COL_STEP, nch)
    need = (i0 + CMP_QBLOCKS - 1) * (QB // CMP_STRIDE) + (QB - CMP_LEN) // CMP_STRIDE
    variant = lax.shift_right_logical(need, step.bit_length() - 1)
    for v in range(nch // step):
        w = (v + 1) * step
        r = min(nsbp, -(-(w // (SLC_BLOCK // CMP_STRIDE)) // LANES) * LANES)

        @pl.when(variant == v)
        def _(w=w, r=r):
            imp = _cmp_scores(g, i0, w, cs_ref, q_ref, kc_ref, vc_ref, oc_ref)
            _cmp_select(i0, w, r, imp, pool_ref, tmap_ref, mb_ref, tf_ref)


def _cmp_attention(cslopes, qf, kc, vc, pool, tmap):
    t = qf.shape[0]
    nch = kc.shape[1]
    nsbp = pool.shape[0]
    rows = CMP_QBLOCKS * QB
    return pl.pallas_call(
        _cmp_kernel,
        grid=(N_KV, t // rows),
        in_specs=[pl.BlockSpec(memory_space=pltpu.SMEM),
                  pl.BlockSpec((rows, HG * DH), lambda g, i: (i, g)),
                  pl.BlockSpec((1, nch, DH), lambda g, i: (g, 0, 0)),
                  pl.BlockSpec((1, nch, DH), lambda g, i: (g, 0, 0)),
                  pl.BlockSpec((nsbp, nch), lambda g, i: (0, 0)),
                  pl.BlockSpec((nsbp, LANES), lambda g, i: (0, 0))],
        out_specs=[pl.BlockSpec((rows, HG * DH), lambda g, i: (i, g)),
                   pl.BlockSpec((rows, nsbp), lambda g, i: (i, g)),
                   pl.BlockSpec((1, CMP_QBLOCKS, 8, LANES), lambda g, i: (g, i, 0, 0))],
        out_shape=[jax.ShapeDtypeStruct((t, NSA_W), F32),
                   jax.ShapeDtypeStruct((t, N_KV * nsbp), BF16),
                   jax.ShapeDtypeStruct((N_KV, t // QB, 8, LANES), jnp.int32)],
        compiler_params=_params(2),
        name="cmp_attn_select",
    )(cslopes, qf, kc, vc, pool, tmap)


SEL_QB = 256
EXT_LO0 = BLK_PER_KT
EXT_HI0 = BLK_PER_KT + 3


def _aug_queries(qs, ext, sconst_ref, rows=QB):
    parts = []
    for h in range(HG):
        e = (ext + sconst_ref[0, h:h + 1, :]).astype(BF16)
        parts.append(jnp.concatenate([qs[h * rows:(h + 1) * rows], e], axis=1))
    return jnp.concatenate(parts, axis=0)


def _sel_kernel(lst_ref, cnt_ref, cs_ref, q_ref, ks_ref, vs_ref, mb_ref, pick_ref, kx_ref, vx_ref,
                sconst_ref, o_ref, m_scr, acc_scr, u_scr, p_scr, al_scr):
    g = pl.program_id(0)
    i = pl.program_id(1)
    q0 = i * SEL_QB
    nhalf = mb_ref.shape[1] // LANES
    n_prev = cnt_ref[g, i]

    qs = _stack_heads(q_ref[...])
    m_scr[...] = jnp.full(m_scr.shape, NEG, F32)
    acc_scr[...] = jnp.zeros(acc_scr.shape, F32)
    p_scr[1] = jnp.zeros(p_scr.shape[1:], BF16)
    al_scr[1] = jnp.ones(al_scr.shape[1:], F32)

    def score_stage(n, slot):
        tn = lst_ref[g, i, n]
        k0 = pl.multiple_of(tn * KT, KT)
        kaug = jnp.concatenate([ks_ref[pl.ds(k0, KT), :], kx_ref[...]], axis=1)
        half = lax.shift_right_logical(tn, OH_PERIOD.bit_length() - 1)
        mbh = mb_ref[:, 0:LANES]
        for hf in range(1, nhalf):
            mbh = jnp.where(half == hf, mb_ref[:, hf * LANES:(hf + 1) * LANES], mbh)
        ext = jnp.dot(mbh, pick_ref[tn & (OH_PERIOD - 1)], preferred_element_type=F32)
        u_scr[slot] = lax.dot_general(_aug_queries(qs, ext, sconst_ref, SEL_QB), kaug, _NT,
                                      preferred_element_type=F32)

    def softmax_stage(n, slot, causal):
        k0 = lst_ref[g, i, n] * KT
        dq = (q0 - k0).astype(F32)
        if causal:
            visible = (lax.broadcasted_iota(jnp.int32, (SEL_QB, KT), 0)
                       - lax.broadcasted_iota(jnp.int32, (SEL_QB, KT), 1) + (q0 - k0)) >= 0
        for h in range(HG):
            rows = pl.ds(h * SEL_QB, SEL_QB)
            uh = u_scr[slot, rows, :]
            if causal:
                uh = jnp.where(visible, uh, NEG)
            c = cs_ref[g, h] * dq
            m_prev = m_scr[rows, :]
            m_next = jnp.maximum(m_prev, jnp.max(uh, axis=1, keepdims=True) - c)
            p_scr[slot, rows, :] = jnp.exp2(uh - (m_next + c)).astype(BF16)
            al_scr[slot, rows, :] = jnp.exp2(m_prev - m_next)
            m_scr[rows, :] = m_next

    def pv_stage(n, slot):
        k0 = pl.multiple_of(lst_ref[g, i, jnp.maximum(n, 0)] * KT, KT)
        vaug = jnp.concatenate([vs_ref[pl.ds(k0, KT), :], vx_ref[...]], axis=1)
        pv = jnp.dot(p_scr[slot], vaug, preferred_element_type=F32)
        acc_scr[...] = al_scr[slot] * acc_scr[...] + pv

    def finish():
        acc = acc_scr[...]
        _store_heads(o_ref, acc[:, 0:DH] / acc[:, DH:DH + 1], rows=SEL_QB)

    score_stage(0, 0)

    def pair(j, carry):
        n = 2 * j
        score_stage(n + 1, 1)
        softmax_stage(n, 0, False)
        pv_stage(n - 1, 1)
        score_stage(n + 2, 0)
        softmax_stage(n + 1, 1, False)
        pv_stage(n, 0)
        return carry

    lax.fori_loop(0, lax.shift_right_logical(n_prev, 1), pair, 0)
    n_even = n_prev & ~1

    @pl.when(n_prev == n_even)
    def _():
        softmax_stage(n_even, 0, True)
        pv_stage(n_even - 1, 1)
        pv_stage(n_even, 0)
        finish()

    @pl.when(n_prev != n_even)
    def _():
        score_stage(n_even + 1, 1)
        softmax_stage(n_even, 0, False)
        pv_stage(n_even - 1, 1)
        softmax_stage(n_even + 1, 1, True)
        pv_stage(n_even, 0)
        pv_stage(n_even + 1, 1)
        finish()


def _sel_attention(tile_list, tile_cnt, cslopes, qf, projb, mb, pick, kx, vx, sconst):
    t = qf.shape[0]
    nsbp = mb.shape[1] // N_KV
    grid_spec = pltpu.PrefetchScalarGridSpec(
        num_scalar_prefetch=2,
        grid=(N_KV, t // SEL_QB),
        in_specs=[pl.BlockSpec(memory_space=pltpu.SMEM),
                  pl.BlockSpec((SEL_QB, HG * DH), lambda g, i, *_: (i, g)),
                  pl.BlockSpec((t, DH), lambda g, i, *_: (0, g)),
                  pl.BlockSpec((t, DH), lambda g, i, *_: (0, N_KV + g)),
                  pl.BlockSpec((SEL_QB, nsbp), lambda g, i, *_: (i, g)),
                  pl.BlockSpec((OH_PERIOD, LANES, LANES), lambda g, i, *_: (0, 0, 0)),
                  pl.BlockSpec((KT, LANES), lambda g, i, *_: (0, 0)),
                  pl.BlockSpec((KT, LANES), lambda g, i, *_: (0, 0)),
                  pl.BlockSpec((1, HG, LANES), lambda g, i, *_: (g, 0, 0))],
        out_specs=pl.BlockSpec((SEL_QB, HG * DH), lambda g, i, *_: (i, g)),
        scratch_shapes=[pltpu.VMEM((HG * SEL_QB, 1), F32),
                        pltpu.VMEM((HG * SEL_QB, 2 * LANES), F32),
                        pltpu.VMEM((2, HG * SEL_QB, KT), F32),
                        pltpu.VMEM((2, HG * SEL_QB, KT), BF16),
                        pltpu.VMEM((2, HG * SEL_QB, 1), F32)])
    return pl.pallas_call(
        _sel_kernel,
        grid_spec=grid_spec,
        out_shape=jax.ShapeDtypeStruct((t, NSA_W), F32),
        compiler_params=_params(2),
        name="sel_attn",
    )(tile_list, tile_cnt, cslopes, qf, projb, projb, mb, pick, kx, vx, sconst)


def _tile_lists(tflags, t):
    nt = t // KT
    nq = t // SEL_QB
    tflags = jnp.max(tflags.reshape(N_KV, nq, SEL_QB // QB, nt), axis=2)
    last = (jnp.arange(nq, dtype=jnp.int32) * SEL_QB + SEL_QB - 1) // KT
    ids = jnp.arange(nt, dtype=jnp.int32)
    active = (tflags != 0) & (ids[None, None, :] < last[None, :, None])
    cnt = jnp.sum(active, axis=-1).astype(jnp.int32)
    order = jnp.argsort(jnp.where(active, ids, nt + ids), axis=-1).astype(jnp.int32)
    lst = jnp.where(ids[None, None, :] == cnt[..., None], last[None, :, None], order)
    return lst, cnt


WSPAN = WINDOW + QB


WIN_QBLOCKS = 4


def _win_kernel(q_ref, kw_ref, vw_ref, band_ref, kx_ref, vx_ref, sconst_ref, o_ref):
    blocks = range(WIN_QBLOCKS)
    q0s = [(pl.program_id(1) * WIN_QBLOCKS + blk) * QB for blk in blocks]
    starts = [pl.multiple_of(jnp.maximum(q0 - WINDOW, 0), QB) for q0 in q0s]
    us = []
    for blk in blocks:
        qs = _stack_heads(q_ref[blk * QB:(blk + 1) * QB, :])
        kaug = jnp.concatenate([kw_ref[pl.ds(starts[blk], WSPAN), :], kx_ref[...]], axis=1)
        us.append(lax.dot_general(_aug_queries(qs, jnp.zeros((QB, LANES), F32), sconst_ref), kaug, _NT,
                                  preferred_element_type=F32))
    pss = []
    for blk in blocks:
        band = band_ref[lax.shift_right_logical(q0s[blk] - starts[blk], QB.bit_length() - 1)]
        ps = []
        for h in range(HG):
            uh = us[blk][h * QB:(h + 1) * QB] + band
            m = jnp.max(uh, axis=1, keepdims=True)
            ps.append(jnp.exp2(uh - m).astype(BF16))
        pss.append(jnp.concatenate(ps, axis=0))
    for blk in blocks:
        vaug = jnp.concatenate([vw_ref[pl.ds(starts[blk], WSPAN), :], vx_ref[...]], axis=1)
        pv = jnp.dot(pss[blk], vaug, preferred_element_type=F32)
        _store_heads(o_ref, pv[:, 0:DH] / pv[:, DH:DH + 1], blk)


def _win_attention(qf, projb, band, kxw, vxw, sconst):
    t = qf.shape[0]
    rows = WIN_QBLOCKS * QB
    return pl.pallas_call(
        _win_kernel,
        grid=(N_KV, t // rows),
        in_specs=[pl.BlockSpec((rows, HG * DH), lambda g, i: (i, g)),
                  pl.BlockSpec((t, DH), lambda g, i: (0, 2 * N_KV + g)),
                  pl.BlockSpec((t, DH), lambda g, i: (0, 3 * N_KV + g)),
                  pl.BlockSpec((WINDOW // QB + 1, QB, WSPAN), lambda g, i: (0, 0, 0)),
                  pl.BlockSpec((WSPAN, LANES), lambda g, i: (0, 0)),
                  pl.BlockSpec((WSPAN, LANES), lambda g, i: (0, 0)),
                  pl.BlockSpec((1, HG, LANES), lambda g, i: (g, 0, 0))],
        out_specs=pl.BlockSpec((rows, HG * DH), lambda g, i: (i, g)),
        out_shape=jax.ShapeDtypeStruct((t, NSA_W), F32),
        compiler_params=_params(2),
        name="win_attn",
    )(qf, projb, projb, band, kxw, vxw, sconst)


def _silu(z):
    return z * jax.nn.sigmoid(z)


def _mix_kernel(oc_ref, os_ref, ow_ref, gates_ref, zn_ref, u_ref, v_ref, zg_ref, zm_ref,
                qm_ref, kvm_ref, lng_ref, lnb_ref, wsp_ref, bsp_ref, y_ref):
    sig = jax.nn.sigmoid(gates_ref[...])
    for h in range(N_HEADS):
        sl = slice(h * DH, (h + 1) * DH)
        comb = (sig[:, 3 * h:3 * h + 1] * oc_ref[:, sl]
                + sig[:, 3 * h + 1:3 * h + 2] * os_ref[:, sl]
                + sig[:, 3 * h + 2:3 * h + 3] * ow_ref[:, sl])
        y_ref[:, sl] = (comb * _silu(zn_ref[:, sl])).astype(y_ref.dtype)

    vv = v_ref[...]
    mu = jnp.mean(vv, axis=-1, keepdims=True)
    var = jnp.mean(jnp.square(vv - mu), axis=-1, keepdims=True)
    vn = ((vv - mu) * lax.rsqrt(var + LN_EPS) * lng_ref[...] + lnb_ref[...]).astype(BF16)
    causal = (lax.broadcasted_iota(jnp.int32, (CHUNK, CHUNK), 0)
              >= lax.broadcasted_iota(jnp.int32, (CHUNK, CHUNK), 1))
    for gg in range(GM_GROUPS):
        sl = slice(gg * GM_DG, (gg + 1) * GM_DG)
        ws = jnp.where(causal, wsp_ref[gg], 0.0).astype(BF16)
        sv = jnp.dot(ws, vn[:, sl], preferred_element_type=F32) + bsp_ref[:, gg:gg + 1]
        y_ref[:, NSA_W + gg * GM_DG:NSA_W + (gg + 1) * GM_DG] = (
            u_ref[:, sl] * sv * _silu(zg_ref[:, sl])).astype(y_ref.dtype)

    mscale = np.float32(MEM_DH ** -0.5)
    for h in range(MEM_HEADS):
        sl = slice(h * MEM_DH, (h + 1) * MEM_DH)
        km = kvm_ref[:, h * MEM_DH:(h + 1) * MEM_DH]
        vm = kvm_ref[:, MEM_W + h * MEM_DH:MEM_W + (h + 1) * MEM_DH]
        s = lax.dot_general(qm_ref[:, sl], km, _NT, preferred_element_type=F32) * mscale
        m = jnp.max(s, axis=-1, keepdims=True)
        e = jnp.exp(s - m)
        p = e / jnp.sum(e, axis=-1, keepdims=True)
        o = jnp.dot(p.astype(BF16), vm, preferred_element_type=F32)
        y_ref[:, NSA_W + GM_W + h * MEM_DH:NSA_W + GM_W + (h + 1) * MEM_DH] = (
            o * _silu(zm_ref[:, sl])).astype(y_ref.dtype)


def _mix(oc, osel, ow, gates, zuvz, zm, qm, kvm, lng, lnb, wsp, bsp_t):
    t = oc.shape[0]
    row = lambda c: (lambda i: (i, c))
    full2 = lambda i: (0, 0)
    return pl.pallas_call(
        _mix_kernel,
        grid=(t // QB,),
        in_specs=[pl.BlockSpec((QB, NSA_W), row(0)),
                  pl.BlockSpec((QB, NSA_W), row(0)),
                  pl.BlockSpec((QB, NSA_W), row(0)),
                  pl.BlockSpec((QB, LANES), row(0)),
                  pl.BlockSpec((QB, NSA_W), row(0)),
                  pl.BlockSpec((QB, GM_W), row(2)),
                  pl.BlockSpec((QB, GM_W), row(3)),
                  pl.BlockSpec((QB, GM_W), row(4)),
                  pl.BlockSpec((QB, MEM_W), row(0)),
                  pl.BlockSpec((QB, MEM_W), row(0)),
                  pl.BlockSpec((MEM_LEN, 2 * MEM_W), full2),
                  pl.BlockSpec((1, GM_W), full2),
                  pl.BlockSpec((1, GM_W), full2),
                  pl.BlockSpec((GM_GROUPS, CHUNK, CHUNK), lambda i: (0, 0, 0)),
                  pl.BlockSpec((CHUNK, GM_GROUPS), full2)],
        out_specs=pl.BlockSpec((QB, D_MODEL), row(0)),
        out_shape=jax.ShapeDtypeStruct((t, D_MODEL), BF16),
        compiler_params=_params(1),
        name="mix_heads",
    )(oc, osel, ow, gates, zuvz, zuvz, zuvz, zuvz, zm, qm, kvm, lng, lnb, wsp, bsp_t)


def _ln_kernel(alpha, x_ref, h_ref, g_ref, b_ref, o_ref):
    z = alpha * x_ref[...] + h_ref[...]
    mu = jnp.mean(z, axis=-1, keepdims=True)
    var = jnp.mean(jnp.square(z - mu), axis=-1, keepdims=True)
    o_ref[...] = (z - mu) * lax.rsqrt(var + LN_EPS) * g_ref[...] + b_ref[...]


def _residual_ln(x2, h, g, b, alpha, tm=256):
    t, d = x2.shape
    tm = min(tm, t)
    return pl.pallas_call(
        functools.partial(_ln_kernel, np.float32(alpha)),
        grid=(t // tm,),
        in_specs=[pl.BlockSpec((tm, d), lambda i: (i, 0)),
                  pl.BlockSpec((tm, d), lambda i: (i, 0)),
                  pl.BlockSpec((1, d), lambda i: (0, 0)),
                  pl.BlockSpec((1, d), lambda i: (0, 0))],
        out_specs=pl.BlockSpec((tm, d), lambda i: (i, 0)),
        out_shape=jax.ShapeDtypeStruct((t, d), F32),
        compiler_params=_params(1),
        name="residual_ln",
    )(x2, h, g, b)


def _pool_matrix(nsbp, nch):
    j = np.arange(nsbp)[:, None]
    c = np.arange(nch)[None, :]
    ratio = SLC_BLOCK // CMP_STRIDE
    return ((c >= ratio * j - 1) & (c <= ratio * j + ratio - 1)).astype(np.float32)


def _tile_map(nsbp):
    j = np.arange(nsbp)[:, None]
    n = np.arange(LANES)[None, :]
    return (j // BLK_PER_KT == n).astype(np.float32)


def _block_pick():
    p = np.arange(OH_PERIOD)[:, None, None]
    j = np.arange(LANES)[None, :, None]
    c = np.arange(LANES)[None, None, :]
    return ((c < BLK_PER_KT) & (j == BLK_PER_KT * p + c)).astype(np.float32)


def _key_ext(n_keys, with_blocks):
    l = np.arange(n_keys)[:, None]
    c = np.arange(LANES)[None, :]
    kx = np.zeros((n_keys, LANES), np.float32)
    if with_blocks:
        kx += (c < BLK_PER_KT) & (l // SLC_BLOCK == c)
    kx += np.where((c >= EXT_LO0) & (c < EXT_LO0 + 3), l % 256, 0)
    kx += np.where((c >= EXT_HI0) & (c < EXT_HI0 + 3), l // 256, 0)
    return kx


def _ones_lane0(n_keys):
    vx = np.zeros((n_keys, LANES), np.float32)
    vx[:, 0] = 1.0
    return vx


def _window_band():
    d = np.arange(WINDOW // QB + 1)[:, None, None] * QB
    dist = d + np.arange(QB)[None, :, None] - np.arange(WSPAN)[None, None, :]
    return np.where((dist >= 0) & (dist < WINDOW), 0.0, NEG).astype(np.float32)


def _slope_lanes(cslopes):
    to_bf16 = lambda v: v.astype(BF16).astype(np.float32)
    s1 = to_bf16(cslopes)
    r1 = cslopes - s1
    s2 = to_bf16(r1)
    s3 = to_bf16(r1 - s2)
    pieces = np.stack([s1, s2, s3, 256.0 * s1, 256.0 * s2, 256.0 * s3], axis=-1)
    return np.pad(pieces, ((0, 0), (0, 0), (EXT_LO0, LANES - EXT_LO0 - 6))).astype(np.float32)


def kernel(x, mem, w_in, w_cmp_k1, w_cmp_k2, w_cmp_v1, w_cmp_v2, pe_cmp_k, pe_cmp_v,
           gm_ln_g, gm_ln_b, w_spatial, b_spatial, w_mem_kv, w_out, ln_g, ln_b):
    b, t, _ = x.shape
    assert b == 1 and t % KT == 0 and t >= WSPAN
    alpha = 2.0 ** 0.25
    nch = t // CMP_STRIDE
    nsb = t // SLC_BLOCK
    nsbp = -(-nsb // LANES) * LANES

    gate0 = NSA_W + 6 * KV_W
    w_t = w_in.T.astype(BF16)
    tail0 = gate0 + N_GATES
    x2 = x[0]
    xb = x2.astype(BF16)
    tm, tn = PROJ_TM, PROJ_TN
    proj = functools.partial(_matmul, xb, w_t, trans_b=True)
    qp = proj(BF16, tm, tn, "proj_q", 0, NSA_W, scale=QSCALE)
    kcvc = proj(F32, tm, tn, "proj_kvc", NSA_W, 2 * KV_W)
    kv = proj(BF16, tm, tn, "proj_kv", NSA_W + 2 * KV_W, 4 * KV_W)
    gates = proj(F32, tm, LANES, "proj_gates", gate0, LANES)
    zuvz = proj(F32, tm, tn, "proj_zuvz", tail0, NSA_W + 3 * GM_W)
    qm = proj(BF16, tm, tn, "proj_qm", tail0 + NSA_W + 3 * GM_W, MEM_W)
    zm = proj(F32, tm, tn, "proj_zm", tail0 + NSA_W + 3 * GM_W + MEM_W, MEM_W)
    kvm = _matmul(mem[0].astype(BF16), w_mem_kv.astype(BF16), BF16, MEM_LEN, tn, "mem_kv")

    pe = jnp.stack([pe_cmp_k, pe_cmp_v])
    w1 = jnp.stack([w_cmp_k1, w_cmp_v1]).astype(BF16)
    w2 = jnp.stack([w_cmp_k2, w_cmp_v2]).astype(BF16)
    kvc = _compress(kcvc, pe, w1, w2)

    slopes_np = np.exp2(-8.0 * np.arange(1, N_HEADS + 1, dtype=np.float32) / N_HEADS
                        ).astype(np.float32).reshape(N_KV, HG)
    cslopes_np = (slopes_np * np.float32(np.log2(np.e))).astype(np.float32)
    cslopes = jnp.asarray(cslopes_np)
    sconst = jnp.asarray(_slope_lanes(cslopes_np))
    pool = jnp.asarray(_pool_matrix(nsbp, nch), BF16)

    tmap = jnp.asarray(_tile_map(nsbp), BF16)
    o_c, mb, tf = _cmp_attention(cslopes, qp, kvc[0], kvc[1], pool, tmap)
    tflags = tf[:, :, 0, :t // KT]
    tile_list, tile_cnt = _tile_lists(tflags, t)
    o_s = _sel_attention(tile_list, tile_cnt, cslopes, qp, kv, mb, jnp.asarray(_block_pick(), BF16),
                         jnp.asarray(_key_ext(KT, True), BF16), jnp.asarray(_ones_lane0(KT), BF16), sconst)
    o_w = _win_attention(qp, kv, jnp.asarray(_window_band(), F32),
                         jnp.asarray(_key_ext(WSPAN, False), BF16), jnp.asarray(_ones_lane0(WSPAN), BF16),
                         sconst)

    y = _mix(o_c, o_s, o_w, gates, zuvz, zm, qm, kvm,
             gm_ln_g.reshape(1, GM_W), gm_ln_b.reshape(1, GM_W), w_spatial, b_spatial.T)
    h = _matmul(y, w_out.astype(BF16), F32, PROJ_TM, PROJ_TN, "out_proj")
    out = _residual_ln(x2, h, ln_g.reshape(1, D_MODEL), ln_b.reshape(1, D_MODEL), alpha)
    return out[None]
```

```python
import functools

import numpy as np
import jax
import jax.numpy as jnp
from jax import lax
from jax.experimental import pallas as pl
from jax.experimental.pallas import tpu as pltpu

D_MODEL = 4096
DH = 128
N_HEADS = 16
HG = 4
N_KV = 4
NSA_W = N_HEADS * DH
KV_W = N_KV * DH
CMP_LEN = 32
CMP_STRIDE = 16
SLC_BLOCK = 64
TOP_N = 16
WINDOW = 512
QB = 128
CHUNK = 128
GM_W = 1024
GM_GROUPS = 4
GM_DG = GM_W // GM_GROUPS
MEM_W = 1024
MEM_LEN = 256
MEM_HEADS = 4
MEM_DH = MEM_W // MEM_HEADS
LN_EPS = 1e-5
N_GATES = N_HEADS * 3

LANES = 128
KT = 512
BLK_PER_KT = KT // SLC_BLOCK
OH_PERIOD = LANES // BLK_PER_KT
NEG = -2.0 ** 100
PROJ_TM = 1024
PROJ_TN = 1024
VMEM_LIMIT = 48 * 1024 * 1024

F32 = jnp.float32
BF16 = jnp.bfloat16
_NT = (((1,), (1,)), ((), ()))


def _params(n_axes):
    return pltpu.CompilerParams(dimension_semantics=("arbitrary",) * n_axes,
                                vmem_limit_bytes=VMEM_LIMIT)


def _mm_kernel(scale, trans_b, a_ref, b_ref, o_ref):
    if trans_b:
        acc = lax.dot_general(a_ref[...], b_ref[...], _NT, preferred_element_type=F32)
    else:
        acc = jnp.dot(a_ref[...], b_ref[...], preferred_element_type=F32)
    if scale is not None:
        acc = acc * scale
    o_ref[...] = acc.astype(o_ref.dtype)


def _matmul(a, b, out_dtype, tm, tn, name, col0=0, ncols=None, scale=None, trans_b=False):
    m, k = a.shape
    n_all = b.shape[0] if trans_b else b.shape[1]
    ncols = n_all - col0 if ncols is None else ncols
    tm = min(tm, m)
    assert ncols % tn == 0 and m % tm == 0
    if trans_b:
        assert col0 % 16 == 0
        b_spec = pl.BlockSpec((pl.Element(tn), pl.Element(k)),
                              lambda j, i: (pl.multiple_of(col0 + j * tn, 16), 0))
    else:
        assert col0 % tn == 0
        b_spec = pl.BlockSpec((k, tn), lambda j, i: (0, col0 // tn + j))
    return pl.pallas_call(
        functools.partial(_mm_kernel, scale, trans_b),
        grid=(ncols // tn, m // tm),
        in_specs=[pl.BlockSpec((tm, k), lambda j, i: (i, 0)), b_spec],
        out_specs=pl.BlockSpec((tm, tn), lambda j, i: (i, j)),
        out_shape=jax.ShapeDtypeStruct((m, ncols), out_dtype),
        compiler_params=_params(2),
        name=name,
    )(a, b)


def _gelu_tanh(x):
    c = np.float32(np.sqrt(2.0 / np.pi))
    return x * (0.5 * (1.0 + jnp.tanh(c * (x + 0.044715 * (x * x * x)))))


def _compress_kernel(x_ref, pe_ref, w1_ref, w2_ref, o_ref):
    nch = o_ref.shape[2]
    a = jnp.zeros((nch, DH), F32)
    b = jnp.zeros((nch, DH), F32)
    for r in range(CMP_STRIDE):
        xr = x_ref[pl.ds(r, nch, stride=CMP_STRIDE), :]
        lo = (xr + pe_ref[0, r:r + 1, :]).astype(BF16)
        hi = (xr + pe_ref[0, CMP_STRIDE + r:CMP_STRIDE + r + 1, :]).astype(BF16)
        a = a + jnp.dot(lo, w1_ref[0, r * DH:(r + 1) * DH, :], preferred_element_type=F32)
        b = b + jnp.dot(hi, w1_ref[0, (CMP_STRIDE + r) * DH:(CMP_STRIDE + r + 1) * DH, :],
                        preferred_element_type=F32)
    pre = a + pltpu.roll(b, nch - 1, axis=0)
    h = _gelu_tanh(pre).astype(BF16)
    o_ref[0, 0] = jnp.dot(h, w2_ref[0], preferred_element_type=F32).astype(o_ref.dtype)


def _compress(kcvc, pe, w1, w2):
    t = kcvc.shape[0]
    nch = t // CMP_STRIDE
    return pl.pallas_call(
        _compress_kernel,
        grid=(2, N_KV),
        in_specs=[pl.BlockSpec((t, DH), lambda s, j: (0, s * N_KV + j)),
                  pl.BlockSpec((1, CMP_LEN, DH), lambda s, j: (s, 0, 0)),
                  pl.BlockSpec((1, CMP_LEN * DH, DH), lambda s, j: (s, 0, 0)),
                  pl.BlockSpec((1, DH, DH), lambda s, j: (s, 0, 0))],
        out_specs=pl.BlockSpec((1, 1, nch, DH), lambda s, j: (s, j, 0, 0)),
        out_shape=jax.ShapeDtypeStruct((2, N_KV, nch, DH), BF16),
        compiler_params=_params(2),
        name="compress_kv",
    )(kcvc, pe, w1, w2)


def _stack_heads(q):
    return jnp.concatenate([q[:, h * DH:(h + 1) * DH] for h in range(HG)], axis=0)


def _store_heads(o_ref, o, blk=0, rows=QB):
    for h in range(HG):
        o_ref[blk * rows:(blk + 1) * rows, h * DH:(h + 1) * DH] = o[h * rows:(h + 1) * rows, :]


CMP_QBLOCKS = 4
CMP_COL_STEP = 256
FORCED_BLOCKS = 3
QSCALE = np.float32(DH ** -0.5 * np.log2(np.e))


def _cmp_scores(g, i0, w, cs_ref, q_ref, kc_ref, vc_ref, oc_ref):
    blocks = range(CMP_QBLOCKS)
    kc = kc_ref[0, 0:w, :]
    vc = vc_ref[0, 0:w, :]
    row = lax.broadcasted_iota(jnp.int32, (QB, 1), 0)
    ss, maskbs, cposfs, row_oks = [], [], [], []
    for blk in blocks:
        q0 = (i0 + blk) * QB
        qs = _stack_heads(q_ref[blk * QB:(blk + 1) * QB, :])
        ss.append(lax.dot_general(qs, kc, _NT, preferred_element_type=F32))
        cpos = lax.broadcasted_iota(jnp.int32, (1, w), 1) * CMP_STRIDE + (CMP_LEN - 1) - q0
        maskbs.append(jnp.where(cpos <= row, 0.0, NEG))
        cposfs.append(cpos.astype(F32))
        row_oks.append(jnp.where(q0 + row >= CMP_LEN - 1, 1.0, 0.0))

    imps = [jnp.zeros((QB, w), F32) for _ in blocks]
    for h in range(HG):
        for blk in blocks:
            t = ss[blk][h * QB:(h + 1) * QB] + cs_ref[g, h] * cposfs[blk] + maskbs[blk]
            m = jnp.max(t, axis=1, keepdims=True)
            p = jnp.exp2(t - m)
            l = jnp.sum(p, axis=1, keepdims=True)
            pn = p * (row_oks[blk] / jnp.maximum(l, 1e-30))
            imps[blk] = imps[blk] + pn
            oc_ref[blk * QB:(blk + 1) * QB, h * DH:(h + 1) * DH] = jnp.dot(
                pn.astype(BF16), vc, preferred_element_type=F32)
    return jnp.concatenate(imps, axis=0)


def _cmp_select(i0, w, r, imp, pool_ref, tmap_ref, mb_ref, tf_ref):
    nq = imp.shape[0]
    nsbp = mb_ref.shape[1]
    n_sel = min(TOP_N, r)
    pool = pool_ref[0:r, 0:w]
    p1 = imp.astype(BF16)
    r1 = imp - p1.astype(F32)
    p2 = r1.astype(BF16)
    p3 = (r1 - p2.astype(F32)).astype(BF16)
    imps = (lax.dot_general(pool, p1, _NT, preferred_element_type=F32)
            + lax.dot_general(pool, p2, _NT, preferred_element_type=F32)
            + lax.dot_general(pool, p3, _NT, preferred_element_type=F32))

    j = lax.broadcasted_iota(jnp.int32, (r, nq), 0)
    tq = i0 * QB + lax.broadcasted_iota(jnp.int32, (r, nq), 1)
    cur = lax.shift_right_arithmetic(tq, SLC_BLOCK.bit_length() - 1)
    forced = jnp.where(j == 0, 1, jnp.where(j == cur, 1, jnp.where(j == cur - 1, 1, 0)))
    sc = jnp.where(j > cur, -1.0, jnp.where(forced == 1, -2.0, imps))
    for _ in range(n_sel - FORCED_BLOCKS):
        mx = jnp.max(sc, axis=0, keepdims=True)
        idx = jnp.min(jnp.where(sc == mx, j, r), axis=0, keepdims=True)
        sc = jnp.where(j == idx, -2.0, sc)
    bias = jnp.where(j <= cur, jnp.where(sc == -2.0, 0.0, NEG), NEG)
    for blk in range(nq // QB):
        rows_out = slice(blk * QB, (blk + 1) * QB)
        bias_t = bias[:, rows_out].T
        mb_ref[rows_out, 0:r] = bias_t.astype(BF16)
        if r < nsbp:
            mb_ref[rows_out, r:nsbp] = jnp.full((QB, nsbp - r), NEG, BF16)
        anyq = jnp.max(jnp.where(bias_t == 0.0, 1.0, 0.0), axis=0, keepdims=True)
        hit = jnp.dot(jnp.broadcast_to(anyq, (8, r)).astype(BF16), tmap_ref[0:r, :],
                      preferred_element_type=F32)
        tf_ref[0, blk] = jnp.where(hit > 0.0, 1, 0).astype(jnp.int32)


def _cmp_kernel(cs_ref, q_ref, kc_ref, vc_ref, pool_ref, tmap_ref, oc_ref, mb_ref, tf_ref):
    g = pl.program_id(0)
    i0 = pl.program_id(1) * CMP_QBLOCKS
    nch = kc_ref.shape[1]
    nsbp = pool_ref.shape[0]
    step = min(CMP_COL_STEP, nch)
    need = (i0 + CMP_QBLOCKS - 1) * (QB // CMP_STRIDE) + (QB - CMP_LEN) // CMP_STRIDE
    variant = lax.shift_right_logical(need, step.bit_length() - 1)
    for v in range(nch // step):
        w = (v + 1) * step
        r = min(nsbp, -(-(w // (SLC_BLOCK // CMP_STRIDE)) // LANES) * LANES)

        @pl.when(variant == v)
        def _(w=w, r=r):
            imp = _cmp_scores(g, i0, w, cs_ref, q_ref, kc_ref, vc_ref, oc_ref)
            _cmp_select(i0, w, r, imp, pool_ref, tmap_ref, mb_ref, tf_ref)


def _cmp_attention(cslopes, qf, kc, vc, pool, tmap):
    t = qf.shape[0]
    nch = kc.shape[1]
    nsbp = pool.shape[0]
    rows = CMP_QBLOCKS * QB
    return pl.pallas_call(
        _cmp_kernel,
        grid=(N_KV, t // rows),
        in_specs=[pl.BlockSpec(memory_space=pltpu.SMEM),
                  pl.BlockSpec((rows, HG * DH), lambda g, i: (i, g)),
                  pl.BlockSpec((1, nch, DH), lambda g, i: (g, 0, 0)),
                  pl.BlockSpec((1, nch, DH), lambda g, i: (g, 0, 0)),
                  pl.BlockSpec((nsbp, nch), lambda g, i: (0, 0)),
                  pl.BlockSpec((nsbp, LANES), lambda g, i: (0, 0))],
        out_specs=[pl.BlockSpec((rows, HG * DH), lambda g, i: (i, g)),
                   pl.BlockSpec((rows, nsbp), lambda g, i: (i, g)),
                   pl.BlockSpec((1, CMP_QBLOCKS, 8, LANES), lambda g, i: (g, i, 0, 0))],
        out_shape=[jax.ShapeDtypeStruct((t, NSA_W), F32),
                   jax.ShapeDtypeStruct((t, N_KV * nsbp), BF16),
                   jax.ShapeDtypeStruct((N_KV, t // QB, 8, LANES), jnp.int32)],
        compiler_params=_params(2),
        name="cmp_attn_select",
    )(cslopes, qf, kc, vc, pool, tmap)


SEL_QB = 256
EXT_LO0 = BLK_PER_KT
EXT_HI0 = BLK_PER_KT + 3


def _aug_queries(qs, ext, sconst_ref, rows=QB):
    parts = []
    for h in range(HG):
        e = (ext + sconst_ref[0, h:h + 1, :]).astype(BF16)
        parts.append(jnp.concatenate([qs[h * rows:(h + 1) * rows], e], axis=1))
    return jnp.concatenate(parts, axis=0)


def _sel_kernel(lst_ref, cnt_ref, cs_ref, q_ref, ks_ref, vs_ref, mb_ref, pick_ref, kx_ref, vx_ref,
                sconst_ref, o_ref, m_scr, acc_scr, u_scr, p_scr, al_scr):
    g = pl.program_id(0)
    i = pl.program_id(1)
    q0 = i * SEL_QB
    nhalf = mb_ref.shape[1] // LANES
    n_prev = cnt_ref[g, i]

    qs = _stack_heads(q_ref[...])
    m_scr[...] = jnp.full(m_scr.shape, NEG, F32)
    acc_scr[...] = jnp.zeros(acc_scr.shape, F32)
    p_scr[1] = jnp.zeros(p_scr.shape[1:], BF16)
    al_scr[1] = jnp.ones(al_scr.shape[1:], F32)

    def score_stage(n, slot):
        tn = lst_ref[g, i, n]
        k0 = pl.multiple_of(tn * KT, KT)
        kaug = jnp.concatenate([ks_ref[pl.ds(k0, KT), :], kx_ref[...]], axis=1)
        half = lax.shift_right_logical(tn, OH_PERIOD.bit_length() - 1)
        mbh = mb_ref[:, 0:LANES]
        for hf in range(1, nhalf):
            mbh = jnp.where(half == hf, mb_ref[:, hf * LANES:(hf + 1) * LANES], mbh)
        ext = jnp.dot(mbh, pick_ref[tn & (OH_PERIOD - 1)], preferred_element_type=F32)
        u_scr[slot] = lax.dot_general(_aug_queries(qs, ext, sconst_ref, SEL_QB), kaug, _NT,
                                      preferred_element_type=F32)

    def softmax_stage(n, slot, causal):
        k0 = lst_ref[g, i, n] * KT
        dq = (q0 - k0).astype(F32)
        if causal:
            visible = (lax.broadcasted_iota(jnp.int32, (SEL_QB, KT), 0)
                       - lax.broadcasted_iota(jnp.int32, (SEL_QB, KT), 1) + (q0 - k0)) >= 0
        for h in range(HG):
            rows = pl.ds(h * SEL_QB, SEL_QB)
            uh = u_scr[slot, rows, :]
            if causal:
                uh = jnp.where(visible, uh, NEG)
            c = cs_ref[g, h] * dq
            m_prev = m_scr[rows, :]
            m_next = jnp.maximum(m_prev, jnp.max(uh, axis=1, keepdims=True) - c)
            p_scr[slot, rows, :] = jnp.exp2(uh - (m_next + c)).astype(BF16)
            al_scr[slot, rows, :] = jnp.exp2(m_prev - m_next)
            m_scr[rows, :] = m_next

    def pv_stage(n, slot):
        k0 = pl.multiple_of(lst_ref[g, i, jnp.maximum(n, 0)] * KT, KT)
        vaug = jnp.concatenate([vs_ref[pl.ds(k0, KT), :], vx_ref[...]], axis=1)
        pv = jnp.dot(p_scr[slot], vaug, preferred_element_type=F32)
        acc_scr[...] = al_scr[slot] * acc_scr[...] + pv

    def finish():
        acc = acc_scr[...]
        _store_heads(o_ref, acc[:, 0:DH] / acc[:, DH:DH + 1], rows=SEL_QB)

    score_stage(0, 0)

    def pair(j, carry):
        n = 2 * j
        score_stage(n + 1, 1)
        softmax_stage(n, 0, False)
        pv_stage(n - 1, 1)
        score_stage(n + 2, 0)
        softmax_stage(n + 1, 1, False)
        pv_stage(n, 0)
        return carry

    lax.fori_loop(0, lax.shift_right_logical(n_prev, 1), pair, 0)
    n_even = n_prev & ~1

    @pl.when(n_prev == n_even)
    def _():
        softmax_stage(n_even, 0, True)
        pv_stage(n_even - 1, 1)
        pv_stage(n_even, 0)
        finish()

    @pl.when(n_prev != n_even)
    def _():
        score_stage(n_even + 1, 1)
        pv_stage(n_even - 1, 1)
        softmax_stage(n_even, 0, False)
        softmax_stage(n_even + 1, 1, True)
        pv_stage(n_even, 0)
        pv_stage(n_even + 1, 1)
        finish()


def _sel_attention(tile_list, tile_cnt, cslopes, qf, projb, mb, pick, kx, vx, sconst):
    t = qf.shape[0]
    nsbp = mb.shape[1] // N_KV
    grid_spec = pltpu.PrefetchScalarGridSpec(
        num_scalar_prefetch=2,
        grid=(N_KV, t // SEL_QB),
        in_specs=[pl.BlockSpec(memory_space=pltpu.SMEM),
                  pl.BlockSpec((SEL_QB, HG * DH), lambda g, i, *_: (i, g)),
                  pl.BlockSpec((t, DH), lambda g, i, *_: (0, g)),
                  pl.BlockSpec((t, DH), lambda g, i, *_: (0, N_KV + g)),
                  pl.BlockSpec((SEL_QB, nsbp), lambda g, i, *_: (i, g)),
                  pl.BlockSpec((OH_PERIOD, LANES, LANES), lambda g, i, *_: (0, 0, 0)),
                  pl.BlockSpec((KT, LANES), lambda g, i, *_: (0, 0)),
                  pl.BlockSpec((KT, LANES), lambda g, i, *_: (0, 0)),
                  pl.BlockSpec((1, HG, LANES), lambda g, i, *_: (g, 0, 0))],
        out_specs=pl.BlockSpec((SEL_QB, HG * DH), lambda g, i, *_: (i, g)),
        scratch_shapes=[pltpu.VMEM((HG * SEL_QB, 1), F32),
                        pltpu.VMEM((HG * SEL_QB, 2 * LANES), F32),
                        pltpu.VMEM((2, HG * SEL_QB, KT), F32),
                        pltpu.VMEM((2, HG * SEL_QB, KT), BF16),
                        pltpu.VMEM((2, HG * SEL_QB, 1), F32)])
    return pl.pallas_call(
        _sel_kernel,
        grid_spec=grid_spec,
        out_shape=jax.ShapeDtypeStruct((t, NSA_W), F32),
        compiler_params=_params(2),
        name="sel_attn",
    )(tile_list, tile_cnt, cslopes, qf, projb, projb, mb, pick, kx, vx, sconst)


def _tile_lists(tflags, t):
    nt = t // KT
    nq = t // SEL_QB
    tflags = jnp.max(tflags.reshape(N_KV, nq, SEL_QB // QB, nt), axis=2)
    last = (jnp.arange(nq, dtype=jnp.int32) * SEL_QB + SEL_QB - 1) // KT
    ids = jnp.arange(nt, dtype=jnp.int32)
    active = (tflags != 0) & (ids[None, None, :] < last[None, :, None])
    cnt = jnp.sum(active, axis=-1).astype(jnp.int32)
    order = jnp.argsort(jnp.where(active, ids, nt + ids), axis=-1).astype(jnp.int32)
    lst = jnp.where(ids[None, None, :] == cnt[..., None], last[None, :, None], order)
    return lst, cnt


WSPAN = WINDOW + QB


WIN_QBLOCKS = 4


def _win_kernel(q_ref, kw_ref, vw_ref, band_ref, kx_ref, vx_ref, sconst_ref, o_ref):
    blocks = range(WIN_QBLOCKS)
    q0s = [(pl.program_id(1) * WIN_QBLOCKS + blk) * QB for blk in blocks]
    starts = [pl.multiple_of(jnp.maximum(q0 - WINDOW, 0), QB) for q0 in q0s]
    us = []
    for blk in blocks:
        qs = _stack_heads(q_ref[blk * QB:(blk + 1) * QB, :])
        kaug = jnp.concatenate([kw_ref[pl.ds(starts[blk], WSPAN), :], kx_ref[...]], axis=1)
        us.append(lax.dot_general(_aug_queries(qs, jnp.zeros((QB, LANES), F32), sconst_ref), kaug, _NT,
                                  preferred_element_type=F32))
    pss = []
    for blk in blocks:
        band = band_ref[lax.shift_right_logical(q0s[blk] - starts[blk], QB.bit_length() - 1)]
        ps = []
        for h in range(HG):
            uh = us[blk][h * QB:(h + 1) * QB] + band
            m = jnp.max(uh, axis=1, keepdims=True)
            ps.append(jnp.exp2(uh - m).astype(BF16))
        pss.append(jnp.concatenate(ps, axis=0))
    for blk in blocks:
        vaug = jnp.concatenate([vw_ref[pl.ds(starts[blk], WSPAN), :], vx_ref[...]], axis=1)
        pv = jnp.dot(pss[blk], vaug, preferred_element_type=F32)
        _store_heads(o_ref, pv[:, 0:DH] / pv[:, DH:DH + 1], blk)


def _win_attention(qf, projb, band, kxw, vxw, sconst):
    t = qf.shape[0]
    rows = WIN_QBLOCKS * QB
    return pl.pallas_call(
        _win_kernel,
        grid=(N_KV, t // rows),
        in_specs=[pl.BlockSpec((rows, HG * DH), lambda g, i: (i, g)),
                  pl.BlockSpec((t, DH), lambda g, i: (0, 2 * N_KV + g)),
                  pl.BlockSpec((t, DH), lambda g, i: (0, 3 * N_KV + g)),
                  pl.BlockSpec((WINDOW // QB + 1, QB, WSPAN), lambda g, i: (0, 0, 0)),
                  pl.BlockSpec((WSPAN, LANES), lambda g, i: (0, 0)),
                  pl.BlockSpec((WSPAN, LANES), lambda g, i: (0, 0)),
                  pl.BlockSpec((1, HG, LANES), lambda g, i: (g, 0, 0))],
        out_specs=pl.BlockSpec((rows, HG * DH), lambda g, i: (i, g)),
        out_shape=jax.ShapeDtypeStruct((t, NSA_W), F32),
        compiler_params=_params(2),
        name="win_attn",
    )(qf, projb, projb, band, kxw, vxw, sconst)


def _silu(z):
    return z * jax.nn.sigmoid(z)


def _mix_kernel(oc_ref, os_ref, ow_ref, gates_ref, zn_ref, u_ref, v_ref, zg_ref, zm_ref,
                qm_ref, kvm_ref, lng_ref, lnb_ref, wsp_ref, bsp_ref, y_ref):
    mscale = np.float32(MEM_DH ** -0.5)
    msl = [slice(h * MEM_DH, (h + 1) * MEM_DH) for h in range(MEM_HEADS)]

    scores = [lax.dot_general(qm_ref[:, msl[h]], kvm_ref[:, msl[h]], _NT, preferred_element_type=F32) * mscale
              for h in range(MEM_HEADS)]

    vv = v_ref[...]
    mu = jnp.mean(vv, axis=-1, keepdims=True)
    var = jnp.mean(jnp.square(vv - mu), axis=-1, keepdims=True)
    vn = ((vv - mu) * lax.rsqrt(var + LN_EPS) * lng_ref[...] + lnb_ref[...]).astype(BF16)
    causal = (lax.broadcasted_iota(jnp.int32, (CHUNK, CHUNK), 0)
              >= lax.broadcasted_iota(jnp.int32, (CHUNK, CHUNK), 1))
    svs = []
    for gg in range(GM_GROUPS):
        ws = jnp.where(causal, wsp_ref[gg], 0.0).astype(BF16)
        svs.append(jnp.dot(ws, vn[:, gg * GM_DG:(gg + 1) * GM_DG], preferred_element_type=F32)
                   + bsp_ref[:, gg:gg + 1])

    probs = []
    for h in range(MEM_HEADS):
        m = jnp.max(scores[h], axis=-1, keepdims=True)
        e = jnp.exp(scores[h] - m)
        probs.append((e / jnp.sum(e, axis=-1, keepdims=True)).astype(BF16))

    sig = jax.nn.sigmoid(gates_ref[...])
    for h in range(N_HEADS):
        sl = slice(h * DH, (h + 1) * DH)
        comb = (sig[:, 3 * h:3 * h + 1] * oc_ref[:, sl]
                + sig[:, 3 * h + 1:3 * h + 2] * os_ref[:, sl]
                + sig[:, 3 * h + 2:3 * h + 3] * ow_ref[:, sl])
        y_ref[:, sl] = (comb * _silu(zn_ref[:, sl])).astype(y_ref.dtype)

    for h in range(MEM_HEADS):
        vm = kvm_ref[:, MEM_W + h * MEM_DH:MEM_W + (h + 1) * MEM_DH]
        o = jnp.dot(probs[h], vm, preferred_element_type=F32)
        y_ref[:, NSA_W + GM_W + h * MEM_DH:NSA_W + GM_W + (h + 1) * MEM_DH] = (
            o * _silu(zm_ref[:, msl[h]])).astype(y_ref.dtype)

    for gg in range(GM_GROUPS):
        sl = slice(gg * GM_DG, (gg + 1) * GM_DG)
        y_ref[:, NSA_W + gg * GM_DG:NSA_W + (gg + 1) * GM_DG] = (
            u_ref[:, sl] * svs[gg] * _silu(zg_ref[:, sl])).astype(y_ref.dtype)


def _mix(oc, osel, ow, gates, zuvz, zm, qm, kvm, lng, lnb, wsp, bsp_t):
    t = oc.shape[0]
    row = lambda c: (lambda i: (i, c))
    full2 = lambda i: (0, 0)
    return pl.pallas_call(
        _mix_kernel,
        grid=(t // QB,),
        in_specs=[pl.BlockSpec((QB, NSA_W), row(0)),
                  pl.BlockSpec((QB, NSA_W), row(0)),
                  pl.BlockSpec((QB, NSA_W), row(0)),
                  pl.BlockSpec((QB, LANES), row(0)),
                  pl.BlockSpec((QB, NSA_W), row(0)),
                  pl.BlockSpec((QB, GM_W), row(2)),
                  pl.BlockSpec((QB, GM_W), row(3)),
                  pl.BlockSpec((QB, GM_W), row(4)),
                  pl.BlockSpec((QB, MEM_W), row(0)),
                  pl.BlockSpec((QB, MEM_W), row(0)),
                  pl.BlockSpec((MEM_LEN, 2 * MEM_W), full2),
                  pl.BlockSpec((1, GM_W), full2),
                  pl.BlockSpec((1, GM_W), full2),
                  pl.BlockSpec((GM_GROUPS, CHUNK, CHUNK), lambda i: (0, 0, 0)),
                  pl.BlockSpec((CHUNK, GM_GROUPS), full2)],
        out_specs=pl.BlockSpec((QB, D_MODEL), row(0)),
        out_shape=jax.ShapeDtypeStruct((t, D_MODEL), BF16),
        compiler_params=_params(1),
        name="mix_heads",
    )(oc, osel, ow, gates, zuvz, zuvz, zuvz, zuvz, zm, qm, kvm, lng, lnb, wsp, bsp_t)


def _ln_kernel(alpha, x_ref, h_ref, g_ref, b_ref, o_ref):
    z = alpha * x_ref[...] + h_ref[...]
    mu = jnp.mean(z, axis=-1, keepdims=True)
    var = jnp.mean(jnp.square(z - mu), axis=-1, keepdims=True)
    o_ref[...] = (z - mu) * lax.rsqrt(var + LN_EPS) * g_ref[...] + b_ref[...]


def _residual_ln(x2, h, g, b, alpha, tm=256):
    t, d = x2.shape
    tm = min(tm, t)
    return pl.pallas_call(
        functools.partial(_ln_kernel, np.float32(alpha)),
        grid=(t // tm,),
        in_specs=[pl.BlockSpec((tm, d), lambda i: (i, 0)),
                  pl.BlockSpec((tm, d), lambda i: (i, 0)),
                  pl.BlockSpec((1, d), lambda i: (0, 0)),
                  pl.BlockSpec((1, d), lambda i: (0, 0))],
        out_specs=pl.BlockSpec((tm, d), lambda i: (i, 0)),
        out_shape=jax.ShapeDtypeStruct((t, d), F32),
        compiler_params=_params(1),
        name="residual_ln",
    )(x2, h, g, b)


def _pool_matrix(nsbp, nch):
    j = np.arange(nsbp)[:, None]
    c = np.arange(nch)[None, :]
    ratio = SLC_BLOCK // CMP_STRIDE
    return ((c >= ratio * j - 1) & (c <= ratio * j + ratio - 1)).astype(np.float32)


def _tile_map(nsbp):
    j = np.arange(nsbp)[:, None]
    n = np.arange(LANES)[None, :]
    return (j // BLK_PER_KT == n).astype(np.float32)


def _block_pick():
    p = np.arange(OH_PERIOD)[:, None, None]
    j = np.arange(LANES)[None, :, None]
    c = np.arange(LANES)[None, None, :]
    return ((c < BLK_PER_KT) & (j == BLK_PER_KT * p + c)).astype(np.float32)


def _key_ext(n_keys, with_blocks):
    l = np.arange(n_keys)[:, None]
    c = np.arange(LANES)[None, :]
    kx = np.zeros((n_keys, LANES), np.float32)
    if with_blocks:
        kx += (c < BLK_PER_KT) & (l // SLC_BLOCK == c)
    kx += np.where((c >= EXT_LO0) & (c < EXT_LO0 + 3), l % 256, 0)
    kx += np.where((c >= EXT_HI0) & (c < EXT_HI0 + 3), l // 256, 0)
    return kx


def _ones_lane0(n_keys):
    vx = np.zeros((n_keys, LANES), np.float32)
    vx[:, 0] = 1.0
    return vx


def _window_band():
    d = np.arange(WINDOW // QB + 1)[:, None, None] * QB
    dist = d + np.arange(QB)[None, :, None] - np.arange(WSPAN)[None, None, :]
    return np.where((dist >= 0) & (dist < WINDOW), 0.0, NEG).astype(np.float32)


def _slope_lanes(cslopes):
    to_bf16 = lambda v: v.astype(BF16).astype(np.float32)
    s1 = to_bf16(cslopes)
    r1 = cslopes - s1
    s2 = to_bf16(r1)
    s3 = to_bf16(r1 - s2)
    pieces = np.stack([s1, s2, s3, 256.0 * s1, 256.0 * s2, 256.0 * s3], axis=-1)
    return np.pad(pieces, ((0, 0), (0, 0), (EXT_LO0, LANES - EXT_LO0 - 6))).astype(np.float32)


def kernel(x, mem, w_in, w_cmp_k1, w_cmp_k2, w_cmp_v1, w_cmp_v2, pe_cmp_k, pe_cmp_v,
           gm_ln_g, gm_ln_b, w_spatial, b_spatial, w_mem_kv, w_out, ln_g, ln_b):
    b, t, _ = x.shape
    assert b == 1 and t % KT == 0 and t >= WSPAN
    alpha = 2.0 ** 0.25
    nch = t // CMP_STRIDE
    nsb = t // SLC_BLOCK
    nsbp = -(-nsb // LANES) * LANES

    gate0 = NSA_W + 6 * KV_W
    w_t = w_in.T.astype(BF16)
    tail0 = gate0 + N_GATES
    x2 = x[0]
    xb = x2.astype(BF16)
    tm, tn = PROJ_TM, PROJ_TN
    proj = functools.partial(_matmul, xb, w_t, trans_b=True)
    qp = proj(BF16, tm, tn, "proj_q", 0, NSA_W, scale=QSCALE)
    kcvc = proj(F32, tm, tn, "proj_kvc", NSA_W, 2 * KV_W)
    kv = proj(BF16, tm, tn, "proj_kv", NSA_W + 2 * KV_W, 4 * KV_W)
    gates = proj(F32, tm, LANES, "proj_gates", gate0, LANES)
    zuvz = proj(F32, tm, tn, "proj_zuvz", tail0, NSA_W + 3 * GM_W)
    qm = proj(BF16, tm, tn, "proj_qm", tail0 + NSA_W + 3 * GM_W, MEM_W)
    zm = proj(F32, tm, tn, "proj_zm", tail0 + NSA_W + 3 * GM_W + MEM_W, MEM_W)
    kvm = _matmul(mem[0].astype(BF16), w_mem_kv.astype(BF16), BF16, MEM_LEN, tn, "mem_kv")

    pe = jnp.stack([pe_cmp_k, pe_cmp_v])
    w1 = jnp.stack([w_cmp_k1, w_cmp_v1]).astype(BF16)
    w2 = jnp.stack([w_cmp_k2, w_cmp_v2]).astype(BF16)
    kvc = _compress(kcvc, pe, w1, w2)

    slopes_np = np.exp2(-8.0 * np.arange(1, N_HEADS + 1, dtype=np.float32) / N_HEADS
                        ).astype(np.float32).reshape(N_KV, HG)
    cslopes_np = (slopes_np * np.float32(np.log2(np.e))).astype(np.float32)
    cslopes = jnp.asarray(cslopes_np)
    sconst = jnp.asarray(_slope_lanes(cslopes_np))
    pool = jnp.asarray(_pool_matrix(nsbp, nch), BF16)

    tmap = jnp.asarray(_tile_map(nsbp), BF16)
    o_c, mb, tf = _cmp_attention(cslopes, qp, kvc[0], kvc[1], pool, tmap)
    tflags = tf[:, :, 0, :t // KT]
    tile_list, tile_cnt = _tile_lists(tflags, t)
    o_s = _sel_attention(tile_list, tile_cnt, cslopes, qp, kv, mb, jnp.asarray(_block_pick(), BF16),
                         jnp.asarray(_key_ext(KT, True), BF16), jnp.asarray(_ones_lane0(KT), BF16), sconst)
    o_w = _win_attention(qp, kv, jnp.asarray(_window_band(), F32),
                         jnp.asarray(_key_ext(WSPAN, False), BF16), jnp.asarray(_ones_lane0(WSPAN), BF16),
                         sconst)

    y = _mix(o_c, o_s, o_w, gates, zuvz, zm, qm, kvm,
             gm_ln_g.reshape(1, GM_W), gm_ln_b.reshape(1, GM_W), w_spatial, b_spatial.T)
    h = _matmul(y, w_out.astype(BF16), F32, PROJ_TM, PROJ_TN, "out_proj")
    out = _residual_ln(x2, h, ln_g.reshape(1, D_MODEL), ln_b.reshape(1, D_MODEL), alpha)
    return out[None]
```

```python
import functools

import numpy as np
import jax
import jax.numpy as jnp
from jax import lax
from jax.experimental import pallas as pl
from jax.experimental.pallas import tpu as pltpu

D_MODEL = 4096
DH = 128
N_HEADS = 16
HG = 4
N_KV = 4
NSA_W = N_HEADS * DH
KV_W = N_KV * DH
CMP_LEN = 32
CMP_STRIDE = 16
SLC_BLOCK = 64
TOP_N = 16
WINDOW = 512
QB = 128
CHUNK = 128
GM_W = 1024
GM_GROUPS = 4
GM_DG = GM_W // GM_GROUPS
MEM_W = 1024
MEM_LEN = 256
MEM_HEADS = 4
MEM_DH = MEM_W // MEM_HEADS
LN_EPS = 1e-5
N_GATES = N_HEADS * 3

LANES = 128
KT = 512
BLK_PER_KT = KT // SLC_BLOCK
OH_PERIOD = LANES // BLK_PER_KT
NEG = -2.0 ** 100
PROJ_TM = 1024
PROJ_TN = 1024
VMEM_LIMIT = 48 * 1024 * 1024

F32 = jnp.float32
BF16 = jnp.bfloat16
_NT = (((1,), (1,)), ((), ()))


def _params(n_axes):
    return pltpu.CompilerParams(dimension_semantics=("arbitrary",) * n_axes,
                                vmem_limit_bytes=VMEM_LIMIT)


def _mm_kernel(scale, trans_b, res_scale, a_ref, b_ref, *rest):
    o_ref = rest[-1]
    if trans_b:
        acc = lax.dot_general(a_ref[...], b_ref[...], _NT, preferred_element_type=F32)
    else:
        acc = jnp.dot(a_ref[...], b_ref[...], preferred_element_type=F32)
    if scale is not None:
        acc = acc * scale
    if res_scale is not None:
        acc = res_scale * rest[0][...] + acc
    o_ref[...] = acc.astype(o_ref.dtype)


def _matmul(a, b, out_dtype, tm, tn, name, col0=0, ncols=None, scale=None, trans_b=False,
            residual=None, res_scale=None):
    m, k = a.shape
    n_all = b.shape[0] if trans_b else b.shape[1]
    ncols = n_all - col0 if ncols is None else ncols
    tm = min(tm, m)
    assert ncols % tn == 0 and m % tm == 0
    if trans_b:
        assert col0 % 16 == 0
        b_spec = pl.BlockSpec((pl.Element(tn), pl.Element(k)),
                              lambda j, i: (pl.multiple_of(col0 + j * tn, 16), 0))
    else:
        assert col0 % tn == 0
        b_spec = pl.BlockSpec((k, tn), lambda j, i: (0, col0 // tn + j))
    in_specs = [pl.BlockSpec((tm, k), lambda j, i: (i, 0)), b_spec]
    operands = [a, b]
    if residual is not None:
        in_specs.append(pl.BlockSpec((tm, tn), lambda j, i: (i, j)))
        operands.append(residual)
    return pl.pallas_call(
        functools.partial(_mm_kernel, scale, trans_b, res_scale if residual is not None else None),
        grid=(ncols // tn, m // tm),
        in_specs=in_specs,
        out_specs=pl.BlockSpec((tm, tn), lambda j, i: (i, j)),
        out_shape=jax.ShapeDtypeStruct((m, ncols), out_dtype),
        compiler_params=_params(2),
        name=name,
    )(*operands)


def _gelu_tanh(x):
    c = np.float32(np.sqrt(2.0 / np.pi))
    return x * (0.5 * (1.0 + jnp.tanh(c * (x + 0.044715 * (x * x * x)))))


def _compress_kernel(x_ref, pe_ref, w1_ref, w2_ref, o_ref):
    nch = o_ref.shape[2]
    a = jnp.zeros((nch, DH), F32)
    b = jnp.zeros((nch, DH), F32)
    for r in range(CMP_STRIDE):
        xr = x_ref[pl.ds(r, nch, stride=CMP_STRIDE), :]
        lo = (xr + pe_ref[0, r:r + 1, :]).astype(BF16)
        hi = (xr + pe_ref[0, CMP_STRIDE + r:CMP_STRIDE + r + 1, :]).astype(BF16)
        a = a + jnp.dot(lo, w1_ref[0, r * DH:(r + 1) * DH, :], preferred_element_type=F32)
        b = b + jnp.dot(hi, w1_ref[0, (CMP_STRIDE + r) * DH:(CMP_STRIDE + r + 1) * DH, :],
                        preferred_element_type=F32)
    pre = a + pltpu.roll(b, nch - 1, axis=0)
    h = _gelu_tanh(pre).astype(BF16)
    o_ref[0, 0] = jnp.dot(h, w2_ref[0], preferred_element_type=F32).astype(o_ref.dtype)


def _compress(kcvc, pe, w1, w2):
    t = kcvc.shape[0]
    nch = t // CMP_STRIDE
    return pl.pallas_call(
        _compress_kernel,
        grid=(2, N_KV),
        in_specs=[pl.BlockSpec((t, DH), lambda s, j: (0, s * N_KV + j)),
                  pl.BlockSpec((1, CMP_LEN, DH), lambda s, j: (s, 0, 0)),
                  pl.BlockSpec((1, CMP_LEN * DH, DH), lambda s, j: (s, 0, 0)),
                  pl.BlockSpec((1, DH, DH), lambda s, j: (s, 0, 0))],
        out_specs=pl.BlockSpec((1, 1, nch, DH), lambda s, j: (s, j, 0, 0)),
        out_shape=jax.ShapeDtypeStruct((2, N_KV, nch, DH), BF16),
        compiler_params=_params(2),
        name="compress_kv",
    )(kcvc, pe, w1, w2)


def _stack_heads(q):
    return jnp.concatenate([q[:, h * DH:(h + 1) * DH] for h in range(HG)], axis=0)


def _store_heads(o_ref, o, blk=0, rows=QB):
    for h in range(HG):
        o_ref[blk * rows:(blk + 1) * rows, h * DH:(h + 1) * DH] = o[h * rows:(h + 1) * rows, :]


CMP_QBLOCKS = 4
CMP_COL_STEP = 256
FORCED_BLOCKS = 3
QSCALE = np.float32(DH ** -0.5 * np.log2(np.e))


def _cmp_scores(g, i0, w, cs_ref, q_ref, kc_ref, vc_ref, oc_ref):
    blocks = range(CMP_QBLOCKS)
    kc = kc_ref[0, 0:w, :]
    vc = vc_ref[0, 0:w, :]
    row = lax.broadcasted_iota(jnp.int32, (QB, 1), 0)
    ss, maskbs, cposfs, row_oks = [], [], [], []
    for blk in blocks:
        q0 = (i0 + blk) * QB
        qs = _stack_heads(q_ref[blk * QB:(blk + 1) * QB, :])
        ss.append(lax.dot_general(qs, kc, _NT, preferred_element_type=F32))
        cpos = lax.broadcasted_iota(jnp.int32, (1, w), 1) * CMP_STRIDE + (CMP_LEN - 1) - q0
        maskbs.append(jnp.where(cpos <= row, 0.0, NEG))
        cposfs.append(cpos.astype(F32))
        row_oks.append(jnp.where(q0 + row >= CMP_LEN - 1, 1.0, 0.0))

    imps = [jnp.zeros((QB, w), F32) for _ in blocks]
    for h in range(HG):
        for blk in blocks:
            t = ss[blk][h * QB:(h + 1) * QB] + cs_ref[g, h] * cposfs[blk] + maskbs[blk]
            m = jnp.max(t, axis=1, keepdims=True)
            p = jnp.exp2(t - m)
            l = jnp.sum(p, axis=1, keepdims=True)
            pn = p * (row_oks[blk] / jnp.maximum(l, 1e-30))
            imps[blk] = imps[blk] + pn
            oc_ref[blk * QB:(blk + 1) * QB, h * DH:(h + 1) * DH] = jnp.dot(
                pn.astype(BF16), vc, preferred_element_type=F32)
    return jnp.concatenate(imps, axis=0)


def _cmp_select(i0, w, r, imp, pool_ref, tmap_ref, mb_ref, tf_ref):
    nq = imp.shape[0]
    nsbp = mb_ref.shape[1]
    n_sel = min(TOP_N, r)
    pool = pool_ref[0:r, 0:w]
    p1 = imp.astype(BF16)
    r1 = imp - p1.astype(F32)
    p2 = r1.astype(BF16)
    p3 = (r1 - p2.astype(F32)).astype(BF16)
    imps = (lax.dot_general(pool, p1, _NT, preferred_element_type=F32)
            + lax.dot_general(pool, p2, _NT, preferred_element_type=F32)
            + lax.dot_general(pool, p3, _NT, preferred_element_type=F32))

    j = lax.broadcasted_iota(jnp.int32, (r, nq), 0)
    tq = i0 * QB + lax.broadcasted_iota(jnp.int32, (r, nq), 1)
    cur = lax.shift_right_arithmetic(tq, SLC_BLOCK.bit_length() - 1)
    forced = jnp.where(j == 0, 1, jnp.where(j == cur, 1, jnp.where(j == cur - 1, 1, 0)))
    sc = jnp.where(j > cur, -1.0, jnp.where(forced == 1, -2.0, imps))
    for _ in range(n_sel - FORCED_BLOCKS):
        mx = jnp.max(sc, axis=0, keepdims=True)
        idx = jnp.min(jnp.where(sc == mx, j, r), axis=0, keepdims=True)
        sc = jnp.where(j == idx, -2.0, sc)
    bias = jnp.where(j <= cur, jnp.where(sc == -2.0, 0.0, NEG), NEG)
    for blk in range(nq // QB):
        rows_out = slice(blk * QB, (blk + 1) * QB)
        bias_t = bias[:, rows_out].T
        mb_ref[rows_out, 0:r] = bias_t.astype(BF16)
        if r < nsbp:
            mb_ref[rows_out, r:nsbp] = jnp.full((QB, nsbp - r), NEG, BF16)
        anyq = jnp.max(jnp.where(bias_t == 0.0, 1.0, 0.0), axis=0, keepdims=True)
        hit = jnp.dot(jnp.broadcast_to(anyq, (8, r)).astype(BF16), tmap_ref[0:r, :],
                      preferred_element_type=F32)
        tf_ref[0, blk] = jnp.where(hit > 0.0, 1, 0).astype(jnp.int32)


def _cmp_kernel(cs_ref, q_ref, kc_ref, vc_ref, pool_ref, tmap_ref, oc_ref, mb_ref, tf_ref):
    g = pl.program_id(0)
    i0 = pl.program_id(1) * CMP_QBLOCKS
    nch = kc_ref.shape[1]
    nsbp = pool_ref.shape[0]
    step = min(CMP_COL_STEP, nch)
    need = (i0 + CMP_QBLOCKS - 1) * (QB // CMP_STRIDE) + (QB - CMP_LEN) // CMP_STRIDE
    variant = lax.shift_right_logical(need, step.bit_length() - 1)
    for v in range(nch // step):
        w = (v + 1) * step
        r = min(nsbp, -(-(w // (SLC_BLOCK // CMP_STRIDE)) // LANES) * LANES)

        @pl.when(variant == v)
        def _(w=w, r=r):
            imp = _cmp_scores(g, i0, w, cs_ref, q_ref, kc_ref, vc_ref, oc_ref)
            _cmp_select(i0, w, r, imp, pool_ref, tmap_ref, mb_ref, tf_ref)


def _cmp_attention(cslopes, qf, kc, vc, pool, tmap):
    t = qf.shape[0]
    nch = kc.shape[1]
    nsbp = pool.shape[0]
    rows = CMP_QBLOCKS * QB
    return pl.pallas_call(
        _cmp_kernel,
        grid=(N_KV, t // rows),
        in_specs=[pl.BlockSpec(memory_space=pltpu.SMEM),
                  pl.BlockSpec((rows, HG * DH), lambda g, i: (i, g)),
                  pl.BlockSpec((1, nch, DH), lambda g, i: (g, 0, 0)),
                  pl.BlockSpec((1, nch, DH), lambda g, i: (g, 0, 0)),
                  pl.BlockSpec((nsbp, nch), lambda g, i: (0, 0)),
                  pl.BlockSpec((nsbp, LANES), lambda g, i: (0, 0))],
        out_specs=[pl.BlockSpec((rows, HG * DH), lambda g, i: (i, g)),
                   pl.BlockSpec((rows, nsbp), lambda g, i: (i, g)),
                   pl.BlockSpec((1, CMP_QBLOCKS, 8, LANES), lambda g, i: (g, i, 0, 0))],
        out_shape=[jax.ShapeDtypeStruct((t, NSA_W), F32),
                   jax.ShapeDtypeStruct((t, N_KV * nsbp), BF16),
                   jax.ShapeDtypeStruct((N_KV, t // QB, 8, LANES), jnp.int32)],
        compiler_params=_params(2),
        name="cmp_attn_select",
    )(cslopes, qf, kc, vc, pool, tmap)


SEL_QB = 256
EXT_LO0 = BLK_PER_KT
EXT_HI0 = BLK_PER_KT + 3


def _aug_queries(qs, ext, sconst_ref, rows=QB):
    parts = []
    for h in range(HG):
        e = (ext + sconst_ref[0, h:h + 1, :]).astype(BF16)
        parts.append(jnp.concatenate([qs[h * rows:(h + 1) * rows], e], axis=1))
    return jnp.concatenate(parts, axis=0)


def _sel_kernel(lst_ref, cnt_ref, cs_ref, q_ref, ks_ref, vs_ref, mb_ref, pick_ref, kx_ref, vx_ref,
                sconst_ref, o_ref, m_scr, acc_scr, u_scr, p_scr, al_scr):
    g = pl.program_id(0)
    i = pl.program_id(1)
    q0 = i * SEL_QB
    nhalf = mb_ref.shape[1] // LANES
    n_prev = cnt_ref[g, i]

    qs = _stack_heads(q_ref[...])
    m_scr[...] = jnp.full(m_scr.shape, NEG, F32)
    acc_scr[...] = jnp.zeros(acc_scr.shape, F32)
    p_scr[1] = jnp.zeros(p_scr.shape[1:], BF16)
    al_scr[1] = jnp.ones(al_scr.shape[1:], F32)

    def score_stage(n, slot):
        tn = lst_ref[g, i, n]
        k0 = pl.multiple_of(tn * KT, KT)
        kaug = jnp.concatenate([ks_ref[pl.ds(k0, KT), :], kx_ref[...]], axis=1)
        half = lax.shift_right_logical(tn, OH_PERIOD.bit_length() - 1)
        mbh = mb_ref[:, 0:LANES]
        for hf in range(1, nhalf):
            mbh = jnp.where(half == hf, mb_ref[:, hf * LANES:(hf + 1) * LANES], mbh)
        ext = jnp.dot(mbh, pick_ref[tn & (OH_PERIOD - 1)], preferred_element_type=F32)
        u_scr[slot] = lax.dot_general(_aug_queries(qs, ext, sconst_ref, SEL_QB), kaug, _NT,
                                      preferred_element_type=F32)

    def softmax_stage(n, slot, causal):
        k0 = lst_ref[g, i, n] * KT
        dq = (q0 - k0).astype(F32)
        if causal:
            visible = (lax.broadcasted_iota(jnp.int32, (SEL_QB, KT), 0)
                       - lax.broadcasted_iota(jnp.int32, (SEL_QB, KT), 1) + (q0 - k0)) >= 0
        for h in range(HG):
            rows = pl.ds(h * SEL_QB, SEL_QB)
            uh = u_scr[slot, rows, :]
            if causal:
                uh = jnp.where(visible, uh, NEG)
            c = cs_ref[g, h] * dq
            m_prev = m_scr[rows, :]
            m_next = jnp.maximum(m_prev, jnp.max(uh, axis=1, keepdims=True) - c)
            p_scr[slot, rows, :] = jnp.exp2(uh - (m_next + c)).astype(BF16)
            al_scr[slot, rows, :] = jnp.exp2(m_prev - m_next)
            m_scr[rows, :] = m_next

    def pv_stage(n, slot):
        k0 = pl.multiple_of(lst_ref[g, i, jnp.maximum(n, 0)] * KT, KT)
        vaug = jnp.concatenate([vs_ref[pl.ds(k0, KT), :], vx_ref[...]], axis=1)
        pv = jnp.dot(p_scr[slot], vaug, preferred_element_type=F32)
        acc_scr[...] = al_scr[slot] * acc_scr[...] + pv

    def finish():
        acc = acc_scr[...]
        _store_heads(o_ref, acc[:, 0:DH] / acc[:, DH:DH + 1], rows=SEL_QB)

    score_stage(0, 0)

    def pair(j, carry):
        n = 2 * j
        score_stage(n + 1, 1)
        softmax_stage(n, 0, False)
        pv_stage(n - 1, 1)
        score_stage(n + 2, 0)
        softmax_stage(n + 1, 1, False)
        pv_stage(n, 0)
        return carry

    lax.fori_loop(0, lax.shift_right_logical(n_prev, 1), pair, 0)
    n_even = n_prev & ~1

    @pl.when(n_prev == n_even)
    def _():
        softmax_stage(n_even, 0, True)
        pv_stage(n_even - 1, 1)
        pv_stage(n_even, 0)
        finish()

    @pl.when(n_prev != n_even)
    def _():
        score_stage(n_even + 1, 1)
        pv_stage(n_even - 1, 1)
        softmax_stage(n_even, 0, False)
        softmax_stage(n_even + 1, 1, True)
        pv_stage(n_even, 0)
        pv_stage(n_even + 1, 1)
        finish()


def _sel_attention(tile_list, tile_cnt, cslopes, qf, projb, mb, pick, kx, vx, sconst):
    t = qf.shape[0]
    nsbp = mb.shape[1] // N_KV
    grid_spec = pltpu.PrefetchScalarGridSpec(
        num_scalar_prefetch=2,
        grid=(N_KV, t // SEL_QB),
        in_specs=[pl.BlockSpec(memory_space=pltpu.SMEM),
                  pl.BlockSpec((SEL_QB, HG * DH), lambda g, i, *_: (i, g)),
                  pl.BlockSpec((t, DH), lambda g, i, *_: (0, g)),
                  pl.BlockSpec((t, DH), lambda g, i, *_: (0, N_KV + g)),
                  pl.BlockSpec((SEL_QB, nsbp), lambda g, i, *_: (i, g)),
                  pl.BlockSpec((OH_PERIOD, LANES, LANES), lambda g, i, *_: (0, 0, 0)),
                  pl.BlockSpec((KT, LANES), lambda g, i, *_: (0, 0)),
                  pl.BlockSpec((KT, LANES), lambda g, i, *_: (0, 0)),
                  pl.BlockSpec((1, HG, LANES), lambda g, i, *_: (g, 0, 0))],
        out_specs=pl.BlockSpec((SEL_QB, HG * DH), lambda g, i, *_: (i, g)),
        scratch_shapes=[pltpu.VMEM((HG * SEL_QB, 1), F32),
                        pltpu.VMEM((HG * SEL_QB, 2 * LANES), F32),
                        pltpu.VMEM((2, HG * SEL_QB, KT), F32),
                        pltpu.VMEM((2, HG * SEL_QB, KT), BF16),
                        pltpu.VMEM((2, HG * SEL_QB, 1), F32)])
    return pl.pallas_call(
        _sel_kernel,
        grid_spec=grid_spec,
        out_shape=jax.ShapeDtypeStruct((t, NSA_W), F32),
        compiler_params=_params(2),
        name="sel_attn",
    )(tile_list, tile_cnt, cslopes, qf, projb, projb, mb, pick, kx, vx, sconst)


def _tile_lists(tflags, t):
    nt = t // KT
    nq = t // SEL_QB
    tflags = jnp.max(tflags.reshape(N_KV, nq, SEL_QB // QB, nt), axis=2)
    last = (jnp.arange(nq, dtype=jnp.int32) * SEL_QB + SEL_QB - 1) // KT
    ids = jnp.arange(nt, dtype=jnp.int32)
    active = (tflags != 0) & (ids[None, None, :] < last[None, :, None])
    cnt = jnp.sum(active, axis=-1).astype(jnp.int32)
    order = jnp.argsort(jnp.where(active, ids, nt + ids), axis=-1).astype(jnp.int32)
    lst = jnp.where(ids[None, None, :] == cnt[..., None], last[None, :, None], order)
    return lst, cnt


WSPAN = WINDOW + QB


WIN_QBLOCKS = 8


def _win_kernel(q_ref, kw_ref, vw_ref, band_ref, kx_ref, vx_ref, sconst_ref, o_ref):
    blocks = range(WIN_QBLOCKS)
    q0s = [(pl.program_id(1) * WIN_QBLOCKS + blk) * QB for blk in blocks]
    starts = [pl.multiple_of(jnp.maximum(q0 - WINDOW, 0), QB) for q0 in q0s]
    us = []
    for blk in blocks:
        qs = _stack_heads(q_ref[blk * QB:(blk + 1) * QB, :])
        kaug = jnp.concatenate([kw_ref[pl.ds(starts[blk], WSPAN), :], kx_ref[...]], axis=1)
        us.append(lax.dot_general(_aug_queries(qs, jnp.zeros((QB, LANES), F32), sconst_ref), kaug, _NT,
                                  preferred_element_type=F32))
    pss = []
    for blk in blocks:
        band = band_ref[lax.shift_right_logical(q0s[blk] - starts[blk], QB.bit_length() - 1)]
        ps = []
        for h in range(HG):
            uh = us[blk][h * QB:(h + 1) * QB] + band
            m = jnp.max(uh, axis=1, keepdims=True)
            ps.append(jnp.exp2(uh - m).astype(BF16))
        pss.append(jnp.concatenate(ps, axis=0))
    for blk in blocks:
        vaug = jnp.concatenate([vw_ref[pl.ds(starts[blk], WSPAN), :], vx_ref[...]], axis=1)
        pv = jnp.dot(pss[blk], vaug, preferred_element_type=F32)
        _store_heads(o_ref, pv[:, 0:DH] / pv[:, DH:DH + 1], blk)


def _win_attention(qf, projb, band, kxw, vxw, sconst):
    t = qf.shape[0]
    rows = WIN_QBLOCKS * QB
    return pl.pallas_call(
        _win_kernel,
        grid=(N_KV, t // rows),
        in_specs=[pl.BlockSpec((rows, HG * DH), lambda g, i: (i, g)),
                  pl.BlockSpec((t, DH), lambda g, i: (0, 2 * N_KV + g)),
                  pl.BlockSpec((t, DH), lambda g, i: (0, 3 * N_KV + g)),
                  pl.BlockSpec((WINDOW // QB + 1, QB, WSPAN), lambda g, i: (0, 0, 0)),
                  pl.BlockSpec((WSPAN, LANES), lambda g, i: (0, 0)),
                  pl.BlockSpec((WSPAN, LANES), lambda g, i: (0, 0)),
                  pl.BlockSpec((1, HG, LANES), lambda g, i: (g, 0, 0))],
        out_specs=pl.BlockSpec((rows, HG * DH), lambda g, i: (i, g)),
        out_shape=jax.ShapeDtypeStruct((t, NSA_W), F32),
        compiler_params=_params(2),
        name="win_attn",
    )(qf, projb, projb, band, kxw, vxw, sconst)


def _silu(z):
    return z * jax.nn.sigmoid(z)


def _mix_kernel(oc_ref, os_ref, ow_ref, gates_ref, zn_ref, u_ref, v_ref, zg_ref, zm_ref,
                qm_ref, kvm_ref, lng_ref, lnb_ref, wsp_ref, bsp_ref, y_ref):
    mscale = np.float32(MEM_DH ** -0.5)
    msl = [slice(h * MEM_DH, (h + 1) * MEM_DH) for h in range(MEM_HEADS)]

    scores = [lax.dot_general(qm_ref[:, msl[h]], kvm_ref[:, msl[h]], _NT, preferred_element_type=F32) * mscale
              for h in range(MEM_HEADS)]

    vv = v_ref[...]
    mu = jnp.mean(vv, axis=-1, keepdims=True)
    var = jnp.mean(jnp.square(vv - mu), axis=-1, keepdims=True)
    vn = ((vv - mu) * lax.rsqrt(var + LN_EPS) * lng_ref[...] + lnb_ref[...]).astype(BF16)
    causal = (lax.broadcasted_iota(jnp.int32, (CHUNK, CHUNK), 0)
              >= lax.broadcasted_iota(jnp.int32, (CHUNK, CHUNK), 1))
    svs = []
    for gg in range(GM_GROUPS):
        ws = jnp.where(causal, wsp_ref[gg], 0.0).astype(BF16)
        svs.append(jnp.dot(ws, vn[:, gg * GM_DG:(gg + 1) * GM_DG], preferred_element_type=F32)
                   + bsp_ref[:, gg:gg + 1])

    probs = []
    for h in range(MEM_HEADS):
        m = jnp.max(scores[h], axis=-1, keepdims=True)
        e = jnp.exp(scores[h] - m)
        probs.append((e / jnp.sum(e, axis=-1, keepdims=True)).astype(BF16))

    sig = jax.nn.sigmoid(gates_ref[...])
    for h in range(N_HEADS):
        sl = slice(h * DH, (h + 1) * DH)
        comb = (sig[:, 3 * h:3 * h + 1] * oc_ref[:, sl]
                + sig[:, 3 * h + 1:3 * h + 2] * os_ref[:, sl]
                + sig[:, 3 * h + 2:3 * h + 3] * ow_ref[:, sl])
        y_ref[:, sl] = (comb * _silu(zn_ref[:, sl])).astype(y_ref.dtype)

    for h in range(MEM_HEADS):
        vm = kvm_ref[:, MEM_W + h * MEM_DH:MEM_W + (h + 1) * MEM_DH]
        o = jnp.dot(probs[h], vm, preferred_element_type=F32)
        y_ref[:, NSA_W + GM_W + h * MEM_DH:NSA_W + GM_W + (h + 1) * MEM_DH] = (
            o * _silu(zm_ref[:, msl[h]])).astype(y_ref.dtype)

    for gg in range(GM_GROUPS):
        sl = slice(gg * GM_DG, (gg + 1) * GM_DG)
        y_ref[:, NSA_W + gg * GM_DG:NSA_W + (gg + 1) * GM_DG] = (
            u_ref[:, sl] * svs[gg] * _silu(zg_ref[:, sl])).astype(y_ref.dtype)


def _mix(oc, osel, ow, gates, zuvz, zm, qm, kvm, lng, lnb, wsp, bsp_t):
    t = oc.shape[0]
    row = lambda c: (lambda i: (i, c))
    full2 = lambda i: (0, 0)
    return pl.pallas_call(
        _mix_kernel,
        grid=(t // QB,),
        in_specs=[pl.BlockSpec((QB, NSA_W), row(0)),
                  pl.BlockSpec((QB, NSA_W), row(0)),
                  pl.BlockSpec((QB, NSA_W), row(0)),
                  pl.BlockSpec((QB, LANES), row(0)),
                  pl.BlockSpec((QB, NSA_W), row(0)),
                  pl.BlockSpec((QB, GM_W), row(2)),
                  pl.BlockSpec((QB, GM_W), row(3)),
                  pl.BlockSpec((QB, GM_W), row(4)),
                  pl.BlockSpec((QB, MEM_W), row(0)),
                  pl.BlockSpec((QB, MEM_W), row(0)),
                  pl.BlockSpec((MEM_LEN, 2 * MEM_W), full2),
                  pl.BlockSpec((1, GM_W), full2),
                  pl.BlockSpec((1, GM_W), full2),
                  pl.BlockSpec((GM_GROUPS, CHUNK, CHUNK), lambda i: (0, 0, 0)),
                  pl.BlockSpec((CHUNK, GM_GROUPS), full2)],
        out_specs=pl.BlockSpec((QB, D_MODEL), row(0)),
        out_shape=jax.ShapeDtypeStruct((t, D_MODEL), BF16),
        compiler_params=_params(1),
        name="mix_heads",
    )(oc, osel, ow, gates, zuvz, zuvz, zuvz, zuvz, zm, qm, kvm, lng, lnb, wsp, bsp_t)


def _ln_kernel(z_ref, g_ref, b_ref, o_ref):
    z = z_ref[...]
    mu = jnp.mean(z, axis=-1, keepdims=True)
    var = jnp.mean(jnp.square(z - mu), axis=-1, keepdims=True)
    o_ref[...] = (z - mu) * lax.rsqrt(var + LN_EPS) * g_ref[...] + b_ref[...]


def _layernorm(z, g, b, tm=256):
    t, d = z.shape
    tm = min(tm, t)
    return pl.pallas_call(
        _ln_kernel,
        grid=(t // tm,),
        in_specs=[pl.BlockSpec((tm, d), lambda i: (i, 0)),
                  pl.BlockSpec((1, d), lambda i: (0, 0)),
                  pl.BlockSpec((1, d), lambda i: (0, 0))],
        out_specs=pl.BlockSpec((tm, d), lambda i: (i, 0)),
        out_shape=jax.ShapeDtypeStruct((t, d), F32),
        compiler_params=_params(1),
        name="residual_ln",
    )(z, g, b)


def _pool_matrix(nsbp, nch):
    j = np.arange(nsbp)[:, None]
    c = np.arange(nch)[None, :]
    ratio = SLC_BLOCK // CMP_STRIDE
    return ((c >= ratio * j - 1) & (c <= ratio * j + ratio - 1)).astype(np.float32)


def _tile_map(nsbp):
    j = np.arange(nsbp)[:, None]
    n = np.arange(LANES)[None, :]
    return (j // BLK_PER_KT == n).astype(np.float32)


def _block_pick():
    p = np.arange(OH_PERIOD)[:, None, None]
    j = np.arange(LANES)[None, :, None]
    c = np.arange(LANES)[None, None, :]
    return ((c < BLK_PER_KT) & (j == BLK_PER_KT * p + c)).astype(np.float32)


def _key_ext(n_keys, with_blocks):
    l = np.arange(n_keys)[:, None]
    c = np.arange(LANES)[None, :]
    kx = np.zeros((n_keys, LANES), np.float32)
    if with_blocks:
        kx += (c < BLK_PER_KT) & (l // SLC_BLOCK == c)
    kx += np.where((c >= EXT_LO0) & (c < EXT_LO0 + 3), l % 256, 0)
    kx += np.where((c >= EXT_HI0) & (c < EXT_HI0 + 3), l // 256, 0)
    return kx


def _ones_lane0(n_keys):
    vx = np.zeros((n_keys, LANES), np.float32)
    vx[:, 0] = 1.0
    return vx


def _window_band():
    d = np.arange(WINDOW // QB + 1)[:, None, None] * QB
    dist = d + np.arange(QB)[None, :, None] - np.arange(WSPAN)[None, None, :]
    return np.where((dist >= 0) & (dist < WINDOW), 0.0, NEG).astype(np.float32)


def _slope_lanes(cslopes):
    to_bf16 = lambda v: v.astype(BF16).astype(np.float32)
    s1 = to_bf16(cslopes)
    r1 = cslopes - s1
    s2 = to_bf16(r1)
    s3 = to_bf16(r1 - s2)
    pieces = np.stack([s1, s2, s3, 256.0 * s1, 256.0 * s2, 256.0 * s3], axis=-1)
    return np.pad(pieces, ((0, 0), (0, 0), (EXT_LO0, LANES - EXT_LO0 - 6))).astype(np.float32)


def kernel(x, mem, w_in, w_cmp_k1, w_cmp_k2, w_cmp_v1, w_cmp_v2, pe_cmp_k, pe_cmp_v,
           gm_ln_g, gm_ln_b, w_spatial, b_spatial, w_mem_kv, w_out, ln_g, ln_b):
    b, t, _ = x.shape
    assert b == 1 and t % KT == 0 and t >= WSPAN
    alpha = 2.0 ** 0.25
    nch = t // CMP_STRIDE
    nsb = t // SLC_BLOCK
    nsbp = -(-nsb // LANES) * LANES

    gate0 = NSA_W + 6 * KV_W
    w_t = w_in.T.astype(BF16)
    tail0 = gate0 + N_GATES
    x2 = x[0]
    xb = x2.astype(BF16)
    tm, tn = PROJ_TM, PROJ_TN
    proj = functools.partial(_matmul, xb, w_t, trans_b=True)
    qp = proj(BF16, tm, tn, "proj_q", 0, NSA_W, scale=QSCALE)
    kcvc = proj(F32, tm, tn, "proj_kvc", NSA_W, 2 * KV_W)
    kv = proj(BF16, tm, tn, "proj_kv", NSA_W + 2 * KV_W, 4 * KV_W)
    gates = proj(F32, tm, LANES, "proj_gates", gate0, LANES)
    zuvz = proj(F32, tm, tn, "proj_zuvz", tail0, NSA_W + 3 * GM_W)
    qm = proj(BF16, tm, tn, "proj_qm", tail0 + NSA_W + 3 * GM_W, MEM_W)
    zm = proj(F32, tm, tn, "proj_zm", tail0 + NSA_W + 3 * GM_W + MEM_W, MEM_W)
    kvm = _matmul(mem[0].astype(BF16), w_mem_kv.astype(BF16), BF16, MEM_LEN, tn, "mem_kv")

    pe = jnp.stack([pe_cmp_k, pe_cmp_v])
    w1 = jnp.stack([w_cmp_k1, w_cmp_v1]).astype(BF16)
    w2 = jnp.stack([w_cmp_k2, w_cmp_v2]).astype(BF16)
    kvc = _compress(kcvc, pe, w1, w2)

    slopes_np = np.exp2(-8.0 * np.arange(1, N_HEADS + 1, dtype=np.float32) / N_HEADS
                        ).astype(np.float32).reshape(N_KV, HG)
    cslopes_np = (slopes_np * np.float32(np.log2(np.e))).astype(np.float32)
    cslopes = jnp.asarray(cslopes_np)
    sconst = jnp.asarray(_slope_lanes(cslopes_np))
    pool = jnp.asarray(_pool_matrix(nsbp, nch), BF16)

    tmap = jnp.asarray(_tile_map(nsbp), BF16)
    o_c, mb, tf = _cmp_attention(cslopes, qp, kvc[0], kvc[1], pool, tmap)
    tflags = tf[:, :, 0, :t // KT]
    tile_list, tile_cnt = _tile_lists(tflags, t)
    o_s = _sel_attention(tile_list, tile_cnt, cslopes, qp, kv, mb, jnp.asarray(_block_pick(), BF16),
                         jnp.asarray(_key_ext(KT, True), BF16), jnp.asarray(_ones_lane0(KT), BF16), sconst)
    o_w = _win_attention(qp, kv, jnp.asarray(_window_band(), F32),
                         jnp.asarray(_key_ext(WSPAN, False), BF16), jnp.asarray(_ones_lane0(WSPAN), BF16),
                         sconst)

    y = _mix(o_c, o_s, o_w, gates, zuvz, zm, qm, kvm,
             gm_ln_g.reshape(1, GM_W), gm_ln_b.reshape(1, GM_W), w_spatial, b_spatial.T)
    z = _matmul(y, w_out.astype(BF16), F32, PROJ_TM, PROJ_TN // 2, "out_proj",
                residual=x2, res_scale=np.float32(alpha))
    out = _layernorm(z, ln_g.reshape(1, D_MODEL), ln_b.reshape(1, D_MODEL))
    return out[None]
```

```python
import functools

import numpy as np
import jax
import jax.numpy as jnp
from jax import lax
from jax.experimental import pallas as pl
from jax.experimental.pallas import tpu as pltpu

D_MODEL = 4096
DH = 128
N_HEADS = 16
HG = 4
N_KV = 4
NSA_W = N_HEADS * DH
KV_W = N_KV * DH
CMP_LEN = 32
CMP_STRIDE = 16
SLC_BLOCK = 64
TOP_N = 16
WINDOW = 512
QB = 128
CHUNK = 128
GM_W = 1024
GM_GROUPS = 4
GM_DG = GM_W // GM_GROUPS
MEM_W = 1024
MEM_LEN = 256
MEM_HEADS = 4
MEM_DH = MEM_W // MEM_HEADS
LN_EPS = 1e-5
N_GATES = N_HEADS * 3

LANES = 128
KT = 512
BLK_PER_KT = KT // SLC_BLOCK
OH_PERIOD = LANES // BLK_PER_KT
NEG = -2.0 ** 100
PROJ_TM = 1024
PROJ_TN = 1024
VMEM_LIMIT = 48 * 1024 * 1024

F32 = jnp.float32
BF16 = jnp.bfloat16
_NT = (((1,), (1,)), ((), ()))


def _params(n_axes):
    return pltpu.CompilerParams(dimension_semantics=("arbitrary",) * n_axes,
                                vmem_limit_bytes=VMEM_LIMIT)


def _mm_kernel(scale, trans_b, res_scale, a_ref, b_ref, *rest):
    o_ref = rest[-1]
    if trans_b:
        acc = lax.dot_general(a_ref[...], b_ref[...], _NT, preferred_element_type=F32)
    else:
        acc = jnp.dot(a_ref[...], b_ref[...], preferred_element_type=F32)
    if scale is not None:
        acc = acc * scale
    if res_scale is not None:
        acc = res_scale * rest[0][...] + acc
    o_ref[...] = acc.astype(o_ref.dtype)


def _matmul(a, b, out_dtype, tm, tn, name, col0=0, ncols=None, scale=None, trans_b=False,
            residual=None, res_scale=None):
    m, k = a.shape
    n_all = b.shape[0] if trans_b else b.shape[1]
    ncols = n_all - col0 if ncols is None else ncols
    tm = min(tm, m)
    assert ncols % tn == 0 and m % tm == 0
    if trans_b:
        assert col0 % 16 == 0
        b_spec = pl.BlockSpec((pl.Element(tn), pl.Element(k)),
                              lambda j, i: (pl.multiple_of(col0 + j * tn, 16), 0))
    else:
        assert col0 % tn == 0
        b_spec = pl.BlockSpec((k, tn), lambda j, i: (0, col0 // tn + j))
    in_specs = [pl.BlockSpec((tm, k), lambda j, i: (i, 0)), b_spec]
    operands = [a, b]
    if residual is not None:
        in_specs.append(pl.BlockSpec((tm, tn), lambda j, i: (i, j)))
        operands.append(residual)
    return pl.pallas_call(
        functools.partial(_mm_kernel, scale, trans_b, res_scale if residual is not None else None),
        grid=(ncols // tn, m // tm),
        in_specs=in_specs,
        out_specs=pl.BlockSpec((tm, tn), lambda j, i: (i, j)),
        out_shape=jax.ShapeDtypeStruct((m, ncols), out_dtype),
        compiler_params=_params(2),
        name=name,
    )(*operands)


def _gelu_tanh(x):
    c = np.float32(np.sqrt(2.0 / np.pi))
    return x * (0.5 * (1.0 + jnp.tanh(c * (x + 0.044715 * (x * x * x)))))


def _compress_kernel(x_ref, pe_ref, w1_ref, w2_ref, o_ref):
    nch = o_ref.shape[2]
    a = jnp.zeros((nch, DH), F32)
    b = jnp.zeros((nch, DH), F32)
    for r in range(CMP_STRIDE):
        xr = x_ref[pl.ds(r, nch, stride=CMP_STRIDE), :]
        lo = (xr + pe_ref[0, r:r + 1, :]).astype(BF16)
        hi = (xr + pe_ref[0, CMP_STRIDE + r:CMP_STRIDE + r + 1, :]).astype(BF16)
        a = a + jnp.dot(lo, w1_ref[0, r * DH:(r + 1) * DH, :], preferred_element_type=F32)
        b = b + jnp.dot(hi, w1_ref[0, (CMP_STRIDE + r) * DH:(CMP_STRIDE + r + 1) * DH, :],
                        preferred_element_type=F32)
    pre = a + pltpu.roll(b, nch - 1, axis=0)
    h = _gelu_tanh(pre).astype(BF16)
    o_ref[0, 0] = jnp.dot(h, w2_ref[0], preferred_element_type=F32).astype(o_ref.dtype)


def _compress(kcvc, pe, w1, w2):
    t = kcvc.shape[0]
    nch = t // CMP_STRIDE
    return pl.pallas_call(
        _compress_kernel,
        grid=(2, N_KV),
        in_specs=[pl.BlockSpec((t, DH), lambda s, j: (0, s * N_KV + j)),
                  pl.BlockSpec((1, CMP_LEN, DH), lambda s, j: (s, 0, 0)),
                  pl.BlockSpec((1, CMP_LEN * DH, DH), lambda s, j: (s, 0, 0)),
                  pl.BlockSpec((1, DH, DH), lambda s, j: (s, 0, 0))],
        out_specs=pl.BlockSpec((1, 1, nch, DH), lambda s, j: (s, j, 0, 0)),
        out_shape=jax.ShapeDtypeStruct((2, N_KV, nch, DH), BF16),
        compiler_params=_params(2),
        name="compress_kv",
    )(kcvc, pe, w1, w2)


def _stack_heads(q):
    return jnp.concatenate([q[:, h * DH:(h + 1) * DH] for h in range(HG)], axis=0)


def _store_heads(o_ref, o, blk=0, rows=QB):
    for h in range(HG):
        o_ref[blk * rows:(blk + 1) * rows, h * DH:(h + 1) * DH] = o[h * rows:(h + 1) * rows, :]


CMP_QBLOCKS = 4
CMP_COL_STEP = 256
FORCED_BLOCKS = 3
QSCALE = np.float32(DH ** -0.5 * np.log2(np.e))


def _cmp_scores(g, i0, w, cs_ref, q_ref, kc_ref, vc_ref, oc_ref):
    blocks = range(CMP_QBLOCKS)
    kc = kc_ref[0, 0:w, :]
    vc = vc_ref[0, 0:w, :]
    row = lax.broadcasted_iota(jnp.int32, (QB, 1), 0)
    ss, maskbs, cposfs, row_oks = [], [], [], []
    for blk in blocks:
        q0 = (i0 + blk) * QB
        qs = _stack_heads(q_ref[blk * QB:(blk + 1) * QB, :])
        ss.append(lax.dot_general(qs, kc, _NT, preferred_element_type=F32))
        cpos = lax.broadcasted_iota(jnp.int32, (1, w), 1) * CMP_STRIDE + (CMP_LEN - 1) - q0
        maskbs.append(jnp.where(cpos <= row, 0.0, NEG))
        cposfs.append(cpos.astype(F32))
        row_oks.append(jnp.where(q0 + row >= CMP_LEN - 1, 1.0, 0.0))

    imps = [jnp.zeros((QB, w), F32) for _ in blocks]
    for h in range(HG):
        for blk in blocks:
            t = ss[blk][h * QB:(h + 1) * QB] + cs_ref[g, h] * cposfs[blk] + maskbs[blk]
            m = jnp.max(t, axis=1, keepdims=True)
            p = jnp.exp2(t - m)
            l = jnp.sum(p, axis=1, keepdims=True)
            pn = p * (row_oks[blk] / jnp.maximum(l, 1e-30))
            imps[blk] = imps[blk] + pn
            oc_ref[blk * QB:(blk + 1) * QB, h * DH:(h + 1) * DH] = jnp.dot(
                pn.astype(BF16), vc, preferred_element_type=F32)
    return jnp.concatenate(imps, axis=0)


def _cmp_select(i0, w, r, imp, pool_ref, tmap_ref, mb_ref, tf_ref):
    nq = imp.shape[0]
    nsbp = mb_ref.shape[1]
    n_sel = min(TOP_N, r)
    pool = pool_ref[0:r, 0:w]
    p1 = imp.astype(BF16)
    r1 = imp - p1.astype(F32)
    p2 = r1.astype(BF16)
    p3 = (r1 - p2.astype(F32)).astype(BF16)
    imps = (lax.dot_general(pool, p1, _NT, preferred_element_type=F32)
            + lax.dot_general(pool, p2, _NT, preferred_element_type=F32)
            + lax.dot_general(pool, p3, _NT, preferred_element_type=F32))

    j = lax.broadcasted_iota(jnp.int32, (r, nq), 0)
    tq = i0 * QB + lax.broadcasted_iota(jnp.int32, (r, nq), 1)
    cur = lax.shift_right_arithmetic(tq, SLC_BLOCK.bit_length() - 1)
    forced = jnp.where(j == 0, 1, jnp.where(j == cur, 1, jnp.where(j == cur - 1, 1, 0)))
    sc = jnp.where(j > cur, -1.0, jnp.where(forced == 1, -2.0, imps))
    for _ in range(n_sel - FORCED_BLOCKS):
        mx = jnp.max(sc, axis=0, keepdims=True)
        idx = jnp.min(jnp.where(sc == mx, j, r), axis=0, keepdims=True)
        sc = jnp.where(j == idx, -2.0, sc)
    bias = jnp.where(j <= cur, jnp.where(sc == -2.0, 0.0, NEG), NEG)
    for blk in range(nq // QB):
        rows_out = slice(blk * QB, (blk + 1) * QB)
        bias_t = bias[:, rows_out].T
        mb_ref[rows_out, 0:r] = bias_t.astype(BF16)
        if r < nsbp:
            mb_ref[rows_out, r:nsbp] = jnp.full((QB, nsbp - r), NEG, BF16)
        anyq = jnp.max(jnp.where(bias_t == 0.0, 1.0, 0.0), axis=0, keepdims=True)
        hit = jnp.dot(jnp.broadcast_to(anyq, (8, r)).astype(BF16), tmap_ref[0:r, :],
                      preferred_element_type=F32)
        tf_ref[0, blk] = jnp.where(hit > 0.0, 1, 0).astype(jnp.int32)


def _cmp_kernel(cs_ref, q_ref, kc_ref, vc_ref, pool_ref, tmap_ref, oc_ref, mb_ref, tf_ref):
    g = pl.program_id(0)
    i0 = pl.program_id(1) * CMP_QBLOCKS
    nch = kc_ref.shape[1]
    nsbp = pool_ref.shape[0]
    step = min(CMP_COL_STEP, nch)
    need = (i0 + CMP_QBLOCKS - 1) * (QB // CMP_STRIDE) + (QB - CMP_LEN) // CMP_STRIDE
    variant = lax.shift_right_logical(need, step.bit_length() - 1)
    for v in range(nch // step):
        w = (v + 1) * step
        r = min(nsbp, -(-(w // (SLC_BLOCK // CMP_STRIDE)) // LANES) * LANES)

        @pl.when(variant == v)
        def _(w=w, r=r):
            imp = _cmp_scores(g, i0, w, cs_ref, q_ref, kc_ref, vc_ref, oc_ref)
            _cmp_select(i0, w, r, imp, pool_ref, tmap_ref, mb_ref, tf_ref)


def _cmp_attention(cslopes, qf, kc, vc, pool, tmap):
    t = qf.shape[0]
    nch = kc.shape[1]
    nsbp = pool.shape[0]
    rows = CMP_QBLOCKS * QB
    return pl.pallas_call(
        _cmp_kernel,
        grid=(N_KV, t // rows),
        in_specs=[pl.BlockSpec(memory_space=pltpu.SMEM),
                  pl.BlockSpec((rows, HG * DH), lambda g, i: (i, g)),
                  pl.BlockSpec((1, nch, DH), lambda g, i: (g, 0, 0)),
                  pl.BlockSpec((1, nch, DH), lambda g, i: (g, 0, 0)),
                  pl.BlockSpec((nsbp, nch), lambda g, i: (0, 0)),
                  pl.BlockSpec((nsbp, LANES), lambda g, i: (0, 0))],
        out_specs=[pl.BlockSpec((rows, HG * DH), lambda g, i: (i, g)),
                   pl.BlockSpec((rows, nsbp), lambda g, i: (i, g)),
                   pl.BlockSpec((1, CMP_QBLOCKS, 8, LANES), lambda g, i: (g, i, 0, 0))],
        out_shape=[jax.ShapeDtypeStruct((t, NSA_W), F32),
                   jax.ShapeDtypeStruct((t, N_KV * nsbp), BF16),
                   jax.ShapeDtypeStruct((N_KV, t // QB, 8, LANES), jnp.int32)],
        compiler_params=_params(2),
        name="cmp_attn_select",
    )(cslopes, qf, kc, vc, pool, tmap)


SEL_QB = 256
EXT_LO0 = BLK_PER_KT
EXT_HI0 = BLK_PER_KT + 3


def _aug_queries(qs, ext, sconst_ref, rows=QB):
    parts = []
    for h in range(HG):
        e = (ext + sconst_ref[0, h:h + 1, :]).astype(BF16)
        parts.append(jnp.concatenate([qs[h * rows:(h + 1) * rows], e], axis=1))
    return jnp.concatenate(parts, axis=0)


def _sel_kernel(lst_ref, cnt_ref, cs_ref, q_ref, ks_ref, vs_ref, mb_ref, pick_ref, kx_ref, vx_ref,
                sconst_ref, o_ref, m_scr, acc_scr, u_scr, p_scr, al_scr):
    g = pl.program_id(0)
    i = pl.program_id(1)
    q0 = i * SEL_QB
    nhalf = mb_ref.shape[1] // LANES
    n_prev = cnt_ref[g, i]

    qs = _stack_heads(q_ref[...])
    m_scr[...] = jnp.full(m_scr.shape, NEG, F32)
    acc_scr[...] = jnp.zeros(acc_scr.shape, F32)
    p_scr[1] = jnp.zeros(p_scr.shape[1:], BF16)
    al_scr[1] = jnp.ones(al_scr.shape[1:], F32)

    def score_stage(n, slot):
        tn = lst_ref[g, i, n]
        k0 = pl.multiple_of(tn * KT, KT)
        kaug = jnp.concatenate([ks_ref[pl.ds(k0, KT), :], kx_ref[...]], axis=1)
        half = lax.shift_right_logical(tn, OH_PERIOD.bit_length() - 1)
        mbh = mb_ref[:, 0:LANES]
        for hf in range(1, nhalf):
            mbh = jnp.where(half == hf, mb_ref[:, hf * LANES:(hf + 1) * LANES], mbh)
        ext = jnp.dot(mbh, pick_ref[tn & (OH_PERIOD - 1)], preferred_element_type=F32)
        u_scr[slot] = lax.dot_general(_aug_queries(qs, ext, sconst_ref, SEL_QB), kaug, _NT,
                                      preferred_element_type=F32)

    def softmax_stage(n, slot, causal):
        k0 = lst_ref[g, i, n] * KT
        dq = (q0 - k0).astype(F32)
        if causal:
            visible = (lax.broadcasted_iota(jnp.int32, (SEL_QB, KT), 0)
                       - lax.broadcasted_iota(jnp.int32, (SEL_QB, KT), 1) + (q0 - k0)) >= 0
        for h in range(HG):
            rows = pl.ds(h * SEL_QB, SEL_QB)
            uh = u_scr[slot, rows, :]
            if causal:
                uh = jnp.where(visible, uh, NEG)
            c = cs_ref[g, h] * dq
            m_prev = m_scr[rows, :]
            m_next = jnp.maximum(m_prev, jnp.max(uh, axis=1, keepdims=True) - c)
            p_scr[slot, rows, :] = jnp.exp2(uh - (m_next + c)).astype(BF16)
            al_scr[slot, rows, :] = jnp.exp2(m_prev - m_next)
            m_scr[rows, :] = m_next

    def pv_stage(n, slot):
        k0 = pl.multiple_of(lst_ref[g, i, jnp.maximum(n, 0)] * KT, KT)
        vaug = jnp.concatenate([vs_ref[pl.ds(k0, KT), :], vx_ref[...]], axis=1)
        pv = jnp.dot(p_scr[slot], vaug, preferred_element_type=F32)
        acc_scr[...] = al_scr[slot] * acc_scr[...] + pv

    def finish():
        acc = acc_scr[...]
        _store_heads(o_ref, acc[:, 0:DH] / acc[:, DH:DH + 1], rows=SEL_QB)

    score_stage(0, 0)

    def pair(j, carry):
        n = 2 * j
        score_stage(n + 1, 1)
        softmax_stage(n, 0, False)
        pv_stage(n - 1, 1)
        score_stage(n + 2, 0)
        softmax_stage(n + 1, 1, False)
        pv_stage(n, 0)
        return carry

    lax.fori_loop(0, lax.shift_right_logical(n_prev, 1), pair, 0)
    n_even = n_prev & ~1

    @pl.when(n_prev == n_even)
    def _():
        softmax_stage(n_even, 0, True)
        pv_stage(n_even - 1, 1)
        pv_stage(n_even, 0)
        finish()

    @pl.when(n_prev != n_even)
    def _():
        score_stage(n_even + 1, 1)
        pv_stage(n_even - 1, 1)
        softmax_stage(n_even, 0, False)
        softmax_stage(n_even + 1, 1, True)
        pv_stage(n_even, 0)
        pv_stage(n_even + 1, 1)
        finish()


def _sel_attention(tile_list, tile_cnt, cslopes, qf, projb, mb, pick, kx, vx, sconst):
    t = qf.shape[0]
    nsbp = mb.shape[1] // N_KV
    grid_spec = pltpu.PrefetchScalarGridSpec(
        num_scalar_prefetch=2,
        grid=(N_KV, t // SEL_QB),
        in_specs=[pl.BlockSpec(memory_space=pltpu.SMEM),
                  pl.BlockSpec((SEL_QB, HG * DH), lambda g, i, *_: (i, g)),
                  pl.BlockSpec((t, DH), lambda g, i, *_: (0, g)),
                  pl.BlockSpec((t, DH), lambda g, i, *_: (0, N_KV + g)),
                  pl.BlockSpec((SEL_QB, nsbp), lambda g, i, *_: (i, g)),
                  pl.BlockSpec((OH_PERIOD, LANES, LANES), lambda g, i, *_: (0, 0, 0)),
                  pl.BlockSpec((KT, LANES), lambda g, i, *_: (0, 0)),
                  pl.BlockSpec((KT, LANES), lambda g, i, *_: (0, 0)),
                  pl.BlockSpec((1, HG, LANES), lambda g, i, *_: (g, 0, 0))],
        out_specs=pl.BlockSpec((SEL_QB, HG * DH), lambda g, i, *_: (i, g)),
        scratch_shapes=[pltpu.VMEM((HG * SEL_QB, 1), F32),
                        pltpu.VMEM((HG * SEL_QB, 2 * LANES), F32),
                        pltpu.VMEM((2, HG * SEL_QB, KT), F32),
                        pltpu.VMEM((2, HG * SEL_QB, KT), BF16),
                        pltpu.VMEM((2, HG * SEL_QB, 1), F32)])
    return pl.pallas_call(
        _sel_kernel,
        grid_spec=grid_spec,
        out_shape=jax.ShapeDtypeStruct((t, NSA_W), F32),
        compiler_params=_params(2),
        name="sel_attn",
    )(tile_list, tile_cnt, cslopes, qf, projb, projb, mb, pick, kx, vx, sconst)


def _tile_lists(tflags, t):
    nt = t // KT
    nq = t // SEL_QB
    tflags = jnp.max(tflags.reshape(N_KV, nq, SEL_QB // QB, nt), axis=2)
    last = (jnp.arange(nq, dtype=jnp.int32) * SEL_QB + SEL_QB - 1) // KT
    ids = jnp.arange(nt, dtype=jnp.int32)
    active = (tflags != 0) & (ids[None, None, :] < last[None, :, None])
    cnt = jnp.sum(active, axis=-1).astype(jnp.int32)
    order = jnp.argsort(jnp.where(active, ids, nt + ids), axis=-1).astype(jnp.int32)
    lst = jnp.where(ids[None, None, :] == cnt[..., None], last[None, :, None], order)
    return lst, cnt


WSPAN = WINDOW + QB


WIN_QBLOCKS = 8


def _win_kernel(q_ref, kw_ref, vw_ref, band_ref, kx_ref, vx_ref, sconst_ref, o_ref):
    blocks = range(WIN_QBLOCKS)
    q0s = [(pl.program_id(1) * WIN_QBLOCKS + blk) * QB for blk in blocks]
    starts = [pl.multiple_of(jnp.maximum(q0 - WINDOW, 0), QB) for q0 in q0s]
    us = []
    for blk in blocks:
        qs = _stack_heads(q_ref[blk * QB:(blk + 1) * QB, :])
        kaug = jnp.concatenate([kw_ref[pl.ds(starts[blk], WSPAN), :], kx_ref[...]], axis=1)
        us.append(lax.dot_general(_aug_queries(qs, jnp.zeros((QB, LANES), F32), sconst_ref), kaug, _NT,
                                  preferred_element_type=F32))
    pss = []
    for blk in blocks:
        band = band_ref[lax.shift_right_logical(q0s[blk] - starts[blk], QB.bit_length() - 1)]
        ps = []
        for h in range(HG):
            uh = us[blk][h * QB:(h + 1) * QB] + band
            m = jnp.max(uh, axis=1, keepdims=True)
            ps.append(jnp.exp2(uh - m).astype(BF16))
        pss.append(jnp.concatenate(ps, axis=0))
    for blk in blocks:
        vaug = jnp.concatenate([vw_ref[pl.ds(starts[blk], WSPAN), :], vx_ref[...]], axis=1)
        pv = jnp.dot(pss[blk], vaug, preferred_element_type=F32)
        _store_heads(o_ref, pv[:, 0:DH] / pv[:, DH:DH + 1], blk)


def _win_attention(qf, projb, band, kxw, vxw, sconst):
    t = qf.shape[0]
    rows = WIN_QBLOCKS * QB
    return pl.pallas_call(
        _win_kernel,
        grid=(N_KV, t // rows),
        in_specs=[pl.BlockSpec((rows, HG * DH), lambda g, i: (i, g)),
                  pl.BlockSpec((t, DH), lambda g, i: (0, 2 * N_KV + g)),
                  pl.BlockSpec((t, DH), lambda g, i: (0, 3 * N_KV + g)),
                  pl.BlockSpec((WINDOW // QB + 1, QB, WSPAN), lambda g, i: (0, 0, 0)),
                  pl.BlockSpec((WSPAN, LANES), lambda g, i: (0, 0)),
                  pl.BlockSpec((WSPAN, LANES), lambda g, i: (0, 0)),
                  pl.BlockSpec((1, HG, LANES), lambda g, i: (g, 0, 0))],
        out_specs=pl.BlockSpec((rows, HG * DH), lambda g, i: (i, g)),
        out_shape=jax.ShapeDtypeStruct((t, NSA_W), F32),
        compiler_params=_params(2),
        name="win_attn",
    )(qf, projb, projb, band, kxw, vxw, sconst)


def _silu(z):
    return z * jax.nn.sigmoid(z)


def _mix_kernel(oc_ref, os_ref, ow_ref, gates_ref, zn_ref, u_ref, v_ref, zg_ref, zm_ref,
                qm_ref, kvm_ref, lng_ref, lnb_ref, wsp_ref, bsp_ref, y_ref):
    mscale = np.float32(MEM_DH ** -0.5)
    msl = [slice(h * MEM_DH, (h + 1) * MEM_DH) for h in range(MEM_HEADS)]

    scores = [lax.dot_general(qm_ref[:, msl[h]], kvm_ref[:, msl[h]], _NT, preferred_element_type=F32) * mscale
              for h in range(MEM_HEADS)]

    vv = v_ref[...]
    mu = jnp.mean(vv, axis=-1, keepdims=True)
    var = jnp.mean(jnp.square(vv - mu), axis=-1, keepdims=True)
    vn = ((vv - mu) * lax.rsqrt(var + LN_EPS) * lng_ref[...] + lnb_ref[...]).astype(BF16)
    causal = (lax.broadcasted_iota(jnp.int32, (CHUNK, CHUNK), 0)
              >= lax.broadcasted_iota(jnp.int32, (CHUNK, CHUNK), 1))
    svs = []
    for gg in range(GM_GROUPS):
        ws = jnp.where(causal, wsp_ref[gg], 0.0).astype(BF16)
        svs.append(jnp.dot(ws, vn[:, gg * GM_DG:(gg + 1) * GM_DG], preferred_element_type=F32)
                   + bsp_ref[:, gg:gg + 1])

    probs = []
    for h in range(MEM_HEADS):
        m = jnp.max(scores[h], axis=-1, keepdims=True)
        e = jnp.exp(scores[h] - m)
        probs.append((e / jnp.sum(e, axis=-1, keepdims=True)).astype(BF16))

    sig = jax.nn.sigmoid(gates_ref[...])
    for h in range(N_HEADS):
        sl = slice(h * DH, (h + 1) * DH)
        comb = (sig[:, 3 * h:3 * h + 1] * oc_ref[:, sl]
                + sig[:, 3 * h + 1:3 * h + 2] * os_ref[:, sl]
                + sig[:, 3 * h + 2:3 * h + 3] * ow_ref[:, sl])
        y_ref[:, sl] = (comb * _silu(zn_ref[:, sl])).astype(y_ref.dtype)

    for h in range(MEM_HEADS):
        vm = kvm_ref[:, MEM_W + h * MEM_DH:MEM_W + (h + 1) * MEM_DH]
        o = jnp.dot(probs[h], vm, preferred_element_type=F32)
        y_ref[:, NSA_W + GM_W + h * MEM_DH:NSA_W + GM_W + (h + 1) * MEM_DH] = (
            o * _silu(zm_ref[:, msl[h]])).astype(y_ref.dtype)

    for gg in range(GM_GROUPS):
        sl = slice(gg * GM_DG, (gg + 1) * GM_DG)
        y_ref[:, NSA_W + gg * GM_DG:NSA_W + (gg + 1) * GM_DG] = (
            u_ref[:, sl] * svs[gg] * _silu(zg_ref[:, sl])).astype(y_ref.dtype)


def _mix(oc, osel, ow, gates, zuvz, zm, qm, kvm, lng, lnb, wsp, bsp_t):
    t = oc.shape[0]
    row = lambda c: (lambda i: (i, c))
    full2 = lambda i: (0, 0)
    return pl.pallas_call(
        _mix_kernel,
        grid=(t // QB,),
        in_specs=[pl.BlockSpec((QB, NSA_W), row(0)),
                  pl.BlockSpec((QB, NSA_W), row(0)),
                  pl.BlockSpec((QB, NSA_W), row(0)),
                  pl.BlockSpec((QB, LANES), row(0)),
                  pl.BlockSpec((QB, NSA_W), row(0)),
                  pl.BlockSpec((QB, GM_W), row(2)),
                  pl.BlockSpec((QB, GM_W), row(3)),
                  pl.BlockSpec((QB, GM_W), row(4)),
                  pl.BlockSpec((QB, MEM_W), row(0)),
                  pl.BlockSpec((QB, MEM_W), row(0)),
                  pl.BlockSpec((MEM_LEN, 2 * MEM_W), full2),
                  pl.BlockSpec((1, GM_W), full2),
                  pl.BlockSpec((1, GM_W), full2),
                  pl.BlockSpec((GM_GROUPS, CHUNK, CHUNK), lambda i: (0, 0, 0)),
                  pl.BlockSpec((CHUNK, GM_GROUPS), full2)],
        out_specs=pl.BlockSpec((QB, D_MODEL), row(0)),
        out_shape=jax.ShapeDtypeStruct((t, D_MODEL), BF16),
        compiler_params=_params(1),
        name="mix_heads",
    )(oc, osel, ow, gates, zuvz, zuvz, zuvz, zuvz, zm, qm, kvm, lng, lnb, wsp, bsp_t)


def _ln_kernel(z_ref, g_ref, b_ref, o_ref):
    z = z_ref[...]
    mu = jnp.mean(z, axis=-1, keepdims=True)
    var = jnp.mean(jnp.square(z - mu), axis=-1, keepdims=True)
    o_ref[...] = (z - mu) * lax.rsqrt(var + LN_EPS) * g_ref[...] + b_ref[...]


def _layernorm(z, g, b, tm=256):
    t, d = z.shape
    tm = min(tm, t)
    return pl.pallas_call(
        _ln_kernel,
        grid=(t // tm,),
        in_specs=[pl.BlockSpec((tm, d), lambda i: (i, 0)),
                  pl.BlockSpec((1, d), lambda i: (0, 0)),
                  pl.BlockSpec((1, d), lambda i: (0, 0))],
        out_specs=pl.BlockSpec((tm, d), lambda i: (i, 0)),
        out_shape=jax.ShapeDtypeStruct((t, d), F32),
        compiler_params=_params(1),
        name="residual_ln",
    )(z, g, b)


def _pool_matrix(nsbp, nch):
    j = np.arange(nsbp)[:, None]
    c = np.arange(nch)[None, :]
    ratio = SLC_BLOCK // CMP_STRIDE
    return ((c >= ratio * j - 1) & (c <= ratio * j + ratio - 1)).astype(np.float32)


def _tile_map(nsbp):
    j = np.arange(nsbp)[:, None]
    n = np.arange(LANES)[None, :]
    return (j // BLK_PER_KT == n).astype(np.float32)


def _block_pick():
    p = np.arange(OH_PERIOD)[:, None, None]
    j = np.arange(LANES)[None, :, None]
    c = np.arange(LANES)[None, None, :]
    return ((c < BLK_PER_KT) & (j == BLK_PER_KT * p + c)).astype(np.float32)


def _key_ext(n_keys, with_blocks):
    l = np.arange(n_keys)[:, None]
    c = np.arange(LANES)[None, :]
    kx = np.zeros((n_keys, LANES), np.float32)
    if with_blocks:
        kx += (c < BLK_PER_KT) & (l // SLC_BLOCK == c)
    kx += np.where((c >= EXT_LO0) & (c < EXT_LO0 + 3), l % 256, 0)
    kx += np.where((c >= EXT_HI0) & (c < EXT_HI0 + 3), l // 256, 0)
    return kx


def _ones_lane0(n_keys):
    vx = np.zeros((n_keys, LANES), np.float32)
    vx[:, 0] = 1.0
    return vx


def _window_band():
    d = np.arange(WINDOW // QB + 1)[:, None, None] * QB
    dist = d + np.arange(QB)[None, :, None] - np.arange(WSPAN)[None, None, :]
    return np.where((dist >= 0) & (dist < WINDOW), 0.0, NEG).astype(np.float32)


def _slope_lanes(cslopes):
    to_bf16 = lambda v: v.astype(BF16).astype(np.float32)
    s1 = to_bf16(cslopes)
    r1 = cslopes - s1
    s2 = to_bf16(r1)
    s3 = to_bf16(r1 - s2)
    pieces = np.stack([s1, s2, s3, 256.0 * s1, 256.0 * s2, 256.0 * s3], axis=-1)
    return np.pad(pieces, ((0, 0), (0, 0), (EXT_LO0, LANES - EXT_LO0 - 6))).astype(np.float32)


def kernel(x, mem, w_in, w_cmp_k1, w_cmp_k2, w_cmp_v1, w_cmp_v2, pe_cmp_k, pe_cmp_v,
           gm_ln_g, gm_ln_b, w_spatial, b_spatial, w_mem_kv, w_out, ln_g, ln_b):
    b, t, _ = x.shape
    assert b == 1 and t % KT == 0 and t >= WSPAN
    alpha = 2.0 ** 0.25
    nch = t // CMP_STRIDE
    nsb = t // SLC_BLOCK
    nsbp = -(-nsb // LANES) * LANES

    gate0 = NSA_W + 6 * KV_W
    w_t = w_in.T.astype(BF16)
    tail0 = gate0 + N_GATES
    x2 = x[0]
    xb = x2.astype(BF16)
    tm, tn = PROJ_TM, PROJ_TN
    proj = functools.partial(_matmul, xb, w_t, trans_b=True)
    qp = proj(BF16, tm, tn, "proj_q", 0, NSA_W, scale=QSCALE)
    kcvc = proj(F32, tm, tn, "proj_kvc", NSA_W, 2 * KV_W)
    kv = proj(BF16, tm, tn, "proj_kv", NSA_W + 2 * KV_W, 4 * KV_W)
    gates = proj(F32, tm, LANES, "proj_gates", gate0, LANES)
    zuvz = proj(F32, tm, tn, "proj_zuvz", tail0, NSA_W + 3 * GM_W)
    qm = proj(BF16, tm, tn, "proj_qm", tail0 + NSA_W + 3 * GM_W, MEM_W)
    zm = proj(F32, tm, tn, "proj_zm", tail0 + NSA_W + 3 * GM_W + MEM_W, MEM_W)
    kvm = _matmul(mem[0].astype(BF16), w_mem_kv.astype(BF16), BF16, MEM_LEN, tn, "mem_kv")

    pe = jnp.stack([pe_cmp_k, pe_cmp_v])
    w1 = jnp.stack([w_cmp_k1, w_cmp_v1]).astype(BF16)
    w2 = jnp.stack([w_cmp_k2, w_cmp_v2]).astype(BF16)
    kvc = _compress(kcvc, pe, w1, w2)

    slopes_np = np.exp2(-8.0 * np.arange(1, N_HEADS + 1, dtype=np.float32) / N_HEADS
                        ).astype(np.float32).reshape(N_KV, HG)
    cslopes_np = (slopes_np * np.float32(np.log2(np.e))).astype(np.float32)
    cslopes = jnp.asarray(cslopes_np)
    sconst = jnp.asarray(_slope_lanes(cslopes_np))
    pool = jnp.asarray(_pool_matrix(nsbp, nch), BF16)

    tmap = jnp.asarray(_tile_map(nsbp), BF16)
    o_c, mb, tf = _cmp_attention(cslopes, qp, kvc[0], kvc[1], pool, tmap)
    tflags = tf[:, :, 0, :t // KT]
    tile_list, tile_cnt = _tile_lists(tflags, t)
    o_s = _sel_attention(tile_list, tile_cnt, cslopes, qp, kv, mb, jnp.asarray(_block_pick(), BF16),
                         jnp.asarray(_key_ext(KT, True), BF16), jnp.asarray(_ones_lane0(KT), BF16), sconst)
    o_w = _win_attention(qp, kv, jnp.asarray(_window_band(), F32),
                         jnp.asarray(_key_ext(WSPAN, False), BF16), jnp.asarray(_ones_lane0(WSPAN), BF16),
                         sconst)

    y = _mix(o_c, o_s, o_w, gates, zuvz, zm, qm, kvm,
             gm_ln_g.reshape(1, GM_W), gm_ln_b.reshape(1, GM_W), w_spatial, b_spatial.T)
    z = _matmul(y, w_out.astype(BF16), F32, PROJ_TM // 2, PROJ_TN, "out_proj",
                residual=x2, res_scale=np.float32(alpha))
    out = _layernorm(z, ln_g.reshape(1, D_MODEL), ln_b.reshape(1, D_MODEL))
    return out[None]
```

```python
import functools

import numpy as np
import jax
import jax.numpy as jnp
from jax import lax
from jax.experimental import pallas as pl
from jax.experimental.pallas import tpu as pltpu

D_MODEL = 4096
DH = 128
N_HEADS = 16
HG = 4
N_KV = 4
NSA_W = N_HEADS * DH
KV_W = N_KV * DH
CMP_LEN = 32
CMP_STRIDE = 16
SLC_BLOCK = 64
TOP_N = 16
WINDOW = 512
QB = 128
CHUNK = 128
GM_W = 1024
GM_GROUPS = 4
GM_DG = GM_W // GM_GROUPS
MEM_W = 1024
MEM_LEN = 256
MEM_HEADS = 4
MEM_DH = MEM_W // MEM_HEADS
LN_EPS = 1e-5
N_GATES = N_HEADS * 3

LANES = 128
KT = 512
BLK_PER_KT = KT // SLC_BLOCK
OH_PERIOD = LANES // BLK_PER_KT
NEG = -2.0 ** 100
PROJ_TM = 1024
PROJ_TN = 1024
VMEM_LIMIT = 48 * 1024 * 1024

F32 = jnp.float32
BF16 = jnp.bfloat16
_NT = (((1,), (1,)), ((), ()))


def _params(n_axes):
    return pltpu.CompilerParams(dimension_semantics=("arbitrary",) * n_axes,
                                vmem_limit_bytes=VMEM_LIMIT)


def _mm_kernel(scale, trans_b, res_scale, a_ref, b_ref, *rest):
    o_ref = rest[-1]
    if trans_b:
        acc = lax.dot_general(a_ref[...], b_ref[...], _NT, preferred_element_type=F32)
    else:
        acc = jnp.dot(a_ref[...], b_ref[...], preferred_element_type=F32)
    if scale is not None:
        acc = acc * scale
    if res_scale is not None:
        acc = res_scale * rest[0][...] + acc
    o_ref[...] = acc.astype(o_ref.dtype)


def _matmul(a, b, out_dtype, tm, tn, name, col0=0, ncols=None, scale=None, trans_b=False,
            residual=None, res_scale=None):
    m, k = a.shape
    n_all = b.shape[0] if trans_b else b.shape[1]
    ncols = n_all - col0 if ncols is None else ncols
    tm = min(tm, m)
    assert ncols % tn == 0 and m % tm == 0
    if trans_b:
        assert col0 % 16 == 0
        b_spec = pl.BlockSpec((pl.Element(tn), pl.Element(k)),
                              lambda j, i: (pl.multiple_of(col0 + j * tn, 16), 0))
    else:
        assert col0 % tn == 0
        b_spec = pl.BlockSpec((k, tn), lambda j, i: (0, col0 // tn + j))
    in_specs = [pl.BlockSpec((tm, k), lambda j, i: (i, 0)), b_spec]
    operands = [a, b]
    if residual is not None:
        in_specs.append(pl.BlockSpec((tm, tn), lambda j, i: (i, j)))
        operands.append(residual)
    return pl.pallas_call(
        functools.partial(_mm_kernel, scale, trans_b, res_scale if residual is not None else None),
        grid=(ncols // tn, m // tm),
        in_specs=in_specs,
        out_specs=pl.BlockSpec((tm, tn), lambda j, i: (i, j)),
        out_shape=jax.ShapeDtypeStruct((m, ncols), out_dtype),
        compiler_params=_params(2),
        name=name,
    )(*operands)


def _gelu_tanh(x):
    c = np.float32(np.sqrt(2.0 / np.pi))
    return x * (0.5 * (1.0 + jnp.tanh(c * (x + 0.044715 * (x * x * x)))))


def _compress_kernel(x_ref, pe_ref, w1_ref, w2_ref, o_ref):
    nch = o_ref.shape[2]
    a = jnp.zeros((nch, DH), F32)
    b = jnp.zeros((nch, DH), F32)
    for r in range(CMP_STRIDE):
        xr = x_ref[pl.ds(r, nch, stride=CMP_STRIDE), :]
        lo = (xr + pe_ref[0, r:r + 1, :]).astype(BF16)
        hi = (xr + pe_ref[0, CMP_STRIDE + r:CMP_STRIDE + r + 1, :]).astype(BF16)
        a = a + jnp.dot(lo, w1_ref[0, r * DH:(r + 1) * DH, :], preferred_element_type=F32)
        b = b + jnp.dot(hi, w1_ref[0, (CMP_STRIDE + r) * DH:(CMP_STRIDE + r + 1) * DH, :],
                        preferred_element_type=F32)
    pre = a + pltpu.roll(b, nch - 1, axis=0)
    h = _gelu_tanh(pre).astype(BF16)
    o_ref[0, 0] = jnp.dot(h, w2_ref[0], preferred_element_type=F32).astype(o_ref.dtype)


def _compress(kcvc, pe, w1, w2):
    t = kcvc.shape[0]
    nch = t // CMP_STRIDE
    return pl.pallas_call(
        _compress_kernel,
        grid=(2, N_KV),
        in_specs=[pl.BlockSpec((t, DH), lambda s, j: (0, s * N_KV + j)),
                  pl.BlockSpec((1, CMP_LEN, DH), lambda s, j: (s, 0, 0)),
                  pl.BlockSpec((1, CMP_LEN * DH, DH), lambda s, j: (s, 0, 0)),
                  pl.BlockSpec((1, DH, DH), lambda s, j: (s, 0, 0))],
        out_specs=pl.BlockSpec((1, 1, nch, DH), lambda s, j: (s, j, 0, 0)),
        out_shape=jax.ShapeDtypeStruct((2, N_KV, nch, DH), BF16),
        compiler_params=_params(2),
        name="compress_kv",
    )(kcvc, pe, w1, w2)


def _stack_heads(q):
    return jnp.concatenate([q[:, h * DH:(h + 1) * DH] for h in range(HG)], axis=0)


def _store_heads(o_ref, o, blk=0, rows=QB):
    for h in range(HG):
        o_ref[blk * rows:(blk + 1) * rows, h * DH:(h + 1) * DH] = o[h * rows:(h + 1) * rows, :]


CMP_QBLOCKS = 4
CMP_COL_STEP = 256
FORCED_BLOCKS = 3
QSCALE = np.float32(DH ** -0.5 * np.log2(np.e))


def _cmp_scores(g, i0, w, cs_ref, q_ref, kc_ref, vc_ref, oc_ref):
    blocks = range(CMP_QBLOCKS)
    kc = kc_ref[0, 0:w, :]
    vc = vc_ref[0, 0:w, :]
    row = lax.broadcasted_iota(jnp.int32, (QB, 1), 0)
    ss, maskbs, cposfs, row_oks = [], [], [], []
    for blk in blocks:
        q0 = (i0 + blk) * QB
        qs = _stack_heads(q_ref[blk * QB:(blk + 1) * QB, :])
        ss.append(lax.dot_general(qs, kc, _NT, preferred_element_type=F32))
        cpos = lax.broadcasted_iota(jnp.int32, (1, w), 1) * CMP_STRIDE + (CMP_LEN - 1) - q0
        maskbs.append(jnp.where(cpos <= row, 0.0, NEG))
        cposfs.append(cpos.astype(F32))
        row_oks.append(jnp.where(q0 + row >= CMP_LEN - 1, 1.0, 0.0))

    imps = [jnp.zeros((QB, w), F32) for _ in blocks]
    for h in range(HG):
        pns = []
        for blk in blocks:
            t = ss[blk][h * QB:(h + 1) * QB] + cs_ref[g, h] * cposfs[blk] + maskbs[blk]
            m = jnp.max(t, axis=1, keepdims=True)
            p = jnp.exp2(t - m)
            l = jnp.sum(p, axis=1, keepdims=True)
            pn = p * (row_oks[blk] / jnp.maximum(l, 1e-30))
            imps[blk] = imps[blk] + pn
            pns.append(pn.astype(BF16))
        oc_ref[:, h * DH:(h + 1) * DH] = jnp.dot(jnp.concatenate(pns, axis=0), vc,
                                                 preferred_element_type=F32)
    return jnp.concatenate(imps, axis=0)


def _cmp_select(i0, w, r, imp, pool_ref, tmap_ref, mb_ref, tf_ref):
    nq = imp.shape[0]
    nsbp = mb_ref.shape[1]
    n_sel = min(TOP_N, r)
    pool = pool_ref[0:r, 0:w]
    p1 = imp.astype(BF16)
    r1 = imp - p1.astype(F32)
    p2 = r1.astype(BF16)
    p3 = (r1 - p2.astype(F32)).astype(BF16)
    imps = (lax.dot_general(pool, p1, _NT, preferred_element_type=F32)
            + lax.dot_general(pool, p2, _NT, preferred_element_type=F32)
            + lax.dot_general(pool, p3, _NT, preferred_element_type=F32))

    j = lax.broadcasted_iota(jnp.int32, (r, nq), 0)
    tq = i0 * QB + lax.broadcasted_iota(jnp.int32, (r, nq), 1)
    cur = lax.shift_right_arithmetic(tq, SLC_BLOCK.bit_length() - 1)
    forced = jnp.where(j == 0, 1, jnp.where(j == cur, 1, jnp.where(j == cur - 1, 1, 0)))
    sc = jnp.where(j > cur, -1.0, jnp.where(forced == 1, -2.0, imps))
    for _ in range(n_sel - FORCED_BLOCKS):
        mx = jnp.max(sc, axis=0, keepdims=True)
        idx = jnp.min(jnp.where(sc == mx, j, r), axis=0, keepdims=True)
        sc = jnp.where(j == idx, -2.0, sc)
    bias = jnp.where(j <= cur, jnp.where(sc == -2.0, 0.0, NEG), NEG)
    for blk in range(nq // QB):
        rows_out = slice(blk * QB, (blk + 1) * QB)
        bias_t = bias[:, rows_out].T
        mb_ref[rows_out, 0:r] = bias_t.astype(BF16)
        if r < nsbp:
            mb_ref[rows_out, r:nsbp] = jnp.full((QB, nsbp - r), NEG, BF16)
        anyq = jnp.max(jnp.where(bias_t == 0.0, 1.0, 0.0), axis=0, keepdims=True)
        hit = jnp.dot(jnp.broadcast_to(anyq, (8, r)).astype(BF16), tmap_ref[0:r, :],
                      preferred_element_type=F32)
        tf_ref[0, blk] = jnp.where(hit > 0.0, 1, 0).astype(jnp.int32)


def _cmp_kernel(cs_ref, q_ref, kc_ref, vc_ref, pool_ref, tmap_ref, oc_ref, mb_ref, tf_ref):
    g = pl.program_id(0)
    i0 = pl.program_id(1) * CMP_QBLOCKS
    nch = kc_ref.shape[1]
    nsbp = pool_ref.shape[0]
    step = min(CMP_COL_STEP, nch)
    need = (i0 + CMP_QBLOCKS - 1) * (QB // CMP_STRIDE) + (QB - CMP_LEN) // CMP_STRIDE
    variant = lax.shift_right_logical(need, step.bit_length() - 1)
    for v in range(nch // step):
        w = (v + 1) * step
        r = min(nsbp, -(-(w // (SLC_BLOCK // CMP_STRIDE)) // LANES) * LANES)

        @pl.when(variant == v)
        def _(w=w, r=r):
            imp = _cmp_scores(g, i0, w, cs_ref, q_ref, kc_ref, vc_ref, oc_ref)
            _cmp_select(i0, w, r, imp, pool_ref, tmap_ref, mb_ref, tf_ref)


def _cmp_attention(cslopes, qf, kc, vc, pool, tmap):
    t = qf.shape[0]
    nch = kc.shape[1]
    nsbp = pool.shape[0]
    rows = CMP_QBLOCKS * QB
    return pl.pallas_call(
        _cmp_kernel,
        grid=(N_KV, t // rows),
        in_specs=[pl.BlockSpec(memory_space=pltpu.SMEM),
                  pl.BlockSpec((rows, HG * DH), lambda g, i: (i, g)),
                  pl.BlockSpec((1, nch, DH), lambda g, i: (g, 0, 0)),
                  pl.BlockSpec((1, nch, DH), lambda g, i: (g, 0, 0)),
                  pl.BlockSpec((nsbp, nch), lambda g, i: (0, 0)),
                  pl.BlockSpec((nsbp, LANES), lambda g, i: (0, 0))],
        out_specs=[pl.BlockSpec((rows, HG * DH), lambda g, i: (i, g)),
                   pl.BlockSpec((rows, nsbp), lambda g, i: (i, g)),
                   pl.BlockSpec((1, CMP_QBLOCKS, 8, LANES), lambda g, i: (g, i, 0, 0))],
        out_shape=[jax.ShapeDtypeStruct((t, NSA_W), F32),
                   jax.ShapeDtypeStruct((t, N_KV * nsbp), BF16),
                   jax.ShapeDtypeStruct((N_KV, t // QB, 8, LANES), jnp.int32)],
        compiler_params=_params(2),
        name="cmp_attn_select",
    )(cslopes, qf, kc, vc, pool, tmap)


SEL_QB = 256
EXT_LO0 = BLK_PER_KT
EXT_HI0 = BLK_PER_KT + 3


def _aug_queries(qs, ext, sconst_ref, rows=QB):
    parts = []
    for h in range(HG):
        e = (ext + sconst_ref[0, h:h + 1, :]).astype(BF16)
        parts.append(jnp.concatenate([qs[h * rows:(h + 1) * rows], e], axis=1))
    return jnp.concatenate(parts, axis=0)


def _sel_kernel(lst_ref, cnt_ref, cs_ref, q_ref, ks_ref, vs_ref, mb_ref, pick_ref, kx_ref, vx_ref,
                sconst_ref, o_ref, m_scr, acc_scr, u_scr, p_scr, al_scr):
    g = pl.program_id(0)
    i = pl.program_id(1)
    q0 = i * SEL_QB
    nhalf = mb_ref.shape[1] // LANES
    n_prev = cnt_ref[g, i]

    qs = _stack_heads(q_ref[...])
    m_scr[...] = jnp.full(m_scr.shape, NEG, F32)
    acc_scr[...] = jnp.zeros(acc_scr.shape, F32)
    p_scr[1] = jnp.zeros(p_scr.shape[1:], BF16)
    al_scr[1] = jnp.ones(al_scr.shape[1:], F32)

    def score_stage(n, slot):
        tn = lst_ref[g, i, n]
        k0 = pl.multiple_of(tn * KT, KT)
        kaug = jnp.concatenate([ks_ref[pl.ds(k0, KT), :], kx_ref[...]], axis=1)
        half = lax.shift_right_logical(tn, OH_PERIOD.bit_length() - 1)
        mbh = mb_ref[:, 0:LANES]
        for hf in range(1, nhalf):
            mbh = jnp.where(half == hf, mb_ref[:, hf * LANES:(hf + 1) * LANES], mbh)
        ext = jnp.dot(mbh, pick_ref[tn & (OH_PERIOD - 1)], preferred_element_type=F32)
        u_scr[slot] = lax.dot_general(_aug_queries(qs, ext, sconst_ref, SEL_QB), kaug, _NT,
                                      preferred_element_type=F32)

    def softmax_stage(n, slot, causal):
        k0 = lst_ref[g, i, n] * KT
        dq = (q0 - k0).astype(F32)
        if causal:
            visible = (lax.broadcasted_iota(jnp.int32, (SEL_QB, KT), 0)
                       - lax.broadcasted_iota(jnp.int32, (SEL_QB, KT), 1) + (q0 - k0)) >= 0
        for h in range(HG):
            rows = pl.ds(h * SEL_QB, SEL_QB)
            uh = u_scr[slot, rows, :]
            if causal:
                uh = jnp.where(visible, uh, NEG)
            c = cs_ref[g, h] * dq
            m_prev = m_scr[rows, :]
            m_next = jnp.maximum(m_prev, jnp.max(uh, axis=1, keepdims=True) - c)
            p_scr[slot, rows, :] = jnp.exp2(uh - (m_next + c)).astype(BF16)
            al_scr[slot, rows, :] = jnp.exp2(m_prev - m_next)
            m_scr[rows, :] = m_next

    def pv_stage(n, slot):
        k0 = pl.multiple_of(lst_ref[g, i, jnp.maximum(n, 0)] * KT, KT)
        vaug = jnp.concatenate([vs_ref[pl.ds(k0, KT), :], vx_ref[...]], axis=1)
        pv = jnp.dot(p_scr[slot], vaug, preferred_element_type=F32)
        acc_scr[...] = al_scr[slot] * acc_scr[...] + pv

    def finish():
        acc = acc_scr[...]
        _store_heads(o_ref, acc[:, 0:DH] / acc[:, DH:DH + 1], rows=SEL_QB)

    score_stage(0, 0)

    def pair(j, carry):
        n = 2 * j
        score_stage(n + 1, 1)
        softmax_stage(n, 0, False)
        pv_stage(n - 1, 1)
        score_stage(n + 2, 0)
        softmax_stage(n + 1, 1, False)
        pv_stage(n, 0)
        return carry

    lax.fori_loop(0, lax.shift_right_logical(n_prev, 1), pair, 0)
    n_even = n_prev & ~1

    @pl.when(n_prev == n_even)
    def _():
        softmax_stage(n_even, 0, True)
        pv_stage(n_even - 1, 1)
        pv_stage(n_even, 0)
        finish()

    @pl.when(n_prev != n_even)
    def _():
        score_stage(n_even + 1, 1)
        pv_stage(n_even - 1, 1)
        softmax_stage(n_even, 0, False)
        softmax_stage(n_even + 1, 1, True)
        pv_stage(n_even, 0)
        pv_stage(n_even + 1, 1)
        finish()


def _sel_attention(tile_list, tile_cnt, cslopes, qf, projb, mb, pick, kx, vx, sconst):
    t = qf.shape[0]
    nsbp = mb.shape[1] // N_KV
    grid_spec = pltpu.PrefetchScalarGridSpec(
        num_scalar_prefetch=2,
        grid=(N_KV, t // SEL_QB),
        in_specs=[pl.BlockSpec(memory_space=pltpu.SMEM),
                  pl.BlockSpec((SEL_QB, HG * DH), lambda g, i, *_: (i, g)),
                  pl.BlockSpec((t, DH), lambda g, i, *_: (0, g)),
                  pl.BlockSpec((t, DH), lambda g, i, *_: (0, N_KV + g)),
                  pl.BlockSpec((SEL_QB, nsbp), lambda g, i, *_: (i, g)),
                  pl.BlockSpec((OH_PERIOD, LANES, LANES), lambda g, i, *_: (0, 0, 0)),
                  pl.BlockSpec((KT, LANES), lambda g, i, *_: (0, 0)),
                  pl.BlockSpec((KT, LANES), lambda g, i, *_: (0, 0)),
                  pl.BlockSpec((1, HG, LANES), lambda g, i, *_: (g, 0, 0))],
        out_specs=pl.BlockSpec((SEL_QB, HG * DH), lambda g, i, *_: (i, g)),
        scratch_shapes=[pltpu.VMEM((HG * SEL_QB, 1), F32),
                        pltpu.VMEM((HG * SEL_QB, 2 * LANES), F32),
                        pltpu.VMEM((2, HG * SEL_QB, KT), F32),
                        pltpu.VMEM((2, HG * SEL_QB, KT), BF16),
                        pltpu.VMEM((2, HG * SEL_QB, 1), F32)])
    return pl.pallas_call(
        _sel_kernel,
        grid_spec=grid_spec,
        out_shape=jax.ShapeDtypeStruct((t, NSA_W), F32),
        compiler_params=_params(2),
        name="sel_attn",
    )(tile_list, tile_cnt, cslopes, qf, projb, projb, mb, pick, kx, vx, sconst)


def _tile_lists(tflags, t):
    nt = t // KT
    nq = t // SEL_QB
    tflags = jnp.max(tflags.reshape(N_KV, nq, SEL_QB // QB, nt), axis=2)
    last = (jnp.arange(nq, dtype=jnp.int32) * SEL_QB + SEL_QB - 1) // KT
    ids = jnp.arange(nt, dtype=jnp.int32)
    active = (tflags != 0) & (ids[None, None, :] < last[None, :, None])
    cnt = jnp.sum(active, axis=-1).astype(jnp.int32)
    order = jnp.argsort(jnp.where(active, ids, nt + ids), axis=-1).astype(jnp.int32)
    lst = jnp.where(ids[None, None, :] == cnt[..., None], last[None, :, None], order)
    return lst, cnt


WSPAN = WINDOW + QB


WIN_QBLOCKS = 8


def _win_kernel(q_ref, kw_ref, vw_ref, band_ref, kx_ref, vx_ref, sconst_ref, o_ref):
    blocks = range(WIN_QBLOCKS)
    q0s = [(pl.program_id(1) * WIN_QBLOCKS + blk) * QB for blk in blocks]
    starts = [pl.multiple_of(jnp.maximum(q0 - WINDOW, 0), QB) for q0 in q0s]
    us = []
    for blk in blocks:
        qs = _stack_heads(q_ref[blk * QB:(blk + 1) * QB, :])
        kaug = jnp.concatenate([kw_ref[pl.ds(starts[blk], WSPAN), :], kx_ref[...]], axis=1)
        us.append(lax.dot_general(_aug_queries(qs, jnp.zeros((QB, LANES), F32), sconst_ref), kaug, _NT,
                                  preferred_element_type=F32))
    pss = []
    for blk in blocks:
        band = band_ref[lax.shift_right_logical(q0s[blk] - starts[blk], QB.bit_length() - 1)]
        ps = []
        for h in range(HG):
            uh = us[blk][h * QB:(h + 1) * QB] + band
            m = jnp.max(uh, axis=1, keepdims=True)
            ps.append(jnp.exp2(uh - m).astype(BF16))
        pss.append(jnp.concatenate(ps, axis=0))
    for blk in blocks:
        vaug = jnp.concatenate([vw_ref[pl.ds(starts[blk], WSPAN), :], vx_ref[...]], axis=1)
        pv = jnp.dot(pss[blk], vaug, preferred_element_type=F32)
        _store_heads(o_ref, pv[:, 0:DH] / pv[:, DH:DH + 1], blk)


def _win_attention(qf, projb, band, kxw, vxw, sconst):
    t = qf.shape[0]
    rows = WIN_QBLOCKS * QB
    return pl.pallas_call(
        _win_kernel,
        grid=(N_KV, t // rows),
        in_specs=[pl.BlockSpec((rows, HG * DH), lambda g, i: (i, g)),
                  pl.BlockSpec((t, DH), lambda g, i: (0, 2 * N_KV + g)),
                  pl.BlockSpec((t, DH), lambda g, i: (0, 3 * N_KV + g)),
                  pl.BlockSpec((WINDOW // QB + 1, QB, WSPAN), lambda g, i: (0, 0, 0)),
                  pl.BlockSpec((WSPAN, LANES), lambda g, i: (0, 0)),
                  pl.BlockSpec((WSPAN, LANES), lambda g, i: (0, 0)),
                  pl.BlockSpec((1, HG, LANES), lambda g, i: (g, 0, 0))],
        out_specs=pl.BlockSpec((rows, HG * DH), lambda g, i: (i, g)),
        out_shape=jax.ShapeDtypeStruct((t, NSA_W), F32),
        compiler_params=_params(2),
        name="win_attn",
    )(qf, projb, projb, band, kxw, vxw, sconst)


def _silu(z):
    return z * jax.nn.sigmoid(z)


def _mix_kernel(oc_ref, os_ref, ow_ref, gates_ref, zn_ref, u_ref, v_ref, zg_ref, zm_ref,
                qm_ref, kvm_ref, lng_ref, lnb_ref, wsp_ref, bsp_ref, y_ref):
    mscale = np.float32(MEM_DH ** -0.5)
    msl = [slice(h * MEM_DH, (h + 1) * MEM_DH) for h in range(MEM_HEADS)]

    scores = [lax.dot_general(qm_ref[:, msl[h]], kvm_ref[:, msl[h]], _NT, preferred_element_type=F32) * mscale
              for h in range(MEM_HEADS)]

    vv = v_ref[...]
    mu = jnp.mean(vv, axis=-1, keepdims=True)
    var = jnp.mean(jnp.square(vv - mu), axis=-1, keepdims=True)
    vn = ((vv - mu) * lax.rsqrt(var + LN_EPS) * lng_ref[...] + lnb_ref[...]).astype(BF16)
    causal = (lax.broadcasted_iota(jnp.int32, (CHUNK, CHUNK), 0)
              >= lax.broadcasted_iota(jnp.int32, (CHUNK, CHUNK), 1))
    svs = []
    for gg in range(GM_GROUPS):
        ws = jnp.where(causal, wsp_ref[gg], 0.0).astype(BF16)
        svs.append(jnp.dot(ws, vn[:, gg * GM_DG:(gg + 1) * GM_DG], preferred_element_type=F32)
                   + bsp_ref[:, gg:gg + 1])

    probs = []
    for h in range(MEM_HEADS):
        m = jnp.max(scores[h], axis=-1, keepdims=True)
        e = jnp.exp(scores[h] - m)
        probs.append((e / jnp.sum(e, axis=-1, keepdims=True)).astype(BF16))

    sig = jax.nn.sigmoid(gates_ref[...])
    for h in range(N_HEADS):
        sl = slice(h * DH, (h + 1) * DH)
        comb = (sig[:, 3 * h:3 * h + 1] * oc_ref[:, sl]
                + sig[:, 3 * h + 1:3 * h + 2] * os_ref[:, sl]
                + sig[:, 3 * h + 2:3 * h + 3] * ow_ref[:, sl])
        y_ref[:, sl] = (comb * _silu(zn_ref[:, sl])).astype(y_ref.dtype)

    for h in range(MEM_HEADS):
        vm = kvm_ref[:, MEM_W + h * MEM_DH:MEM_W + (h + 1) * MEM_DH]
        o = jnp.dot(probs[h], vm, preferred_element_type=F32)
        y_ref[:, NSA_W + GM_W + h * MEM_DH:NSA_W + GM_W + (h + 1) * MEM_DH] = (
            o * _silu(zm_ref[:, msl[h]])).astype(y_ref.dtype)

    for gg in range(GM_GROUPS):
        sl = slice(gg * GM_DG, (gg + 1) * GM_DG)
        y_ref[:, NSA_W + gg * GM_DG:NSA_W + (gg + 1) * GM_DG] = (
            u_ref[:, sl] * svs[gg] * _silu(zg_ref[:, sl])).astype(y_ref.dtype)


def _mix(oc, osel, ow, gates, zuvz, zm, qm, kvm, lng, lnb, wsp, bsp_t):
    t = oc.shape[0]
    row = lambda c: (lambda i: (i, c))
    full2 = lambda i: (0, 0)
    return pl.pallas_call(
        _mix_kernel,
        grid=(t // QB,),
        in_specs=[pl.BlockSpec((QB, NSA_W), row(0)),
                  pl.BlockSpec((QB, NSA_W), row(0)),
                  pl.BlockSpec((QB, NSA_W), row(0)),
                  pl.BlockSpec((QB, LANES), row(0)),
                  pl.BlockSpec((QB, NSA_W), row(0)),
                  pl.BlockSpec((QB, GM_W), row(2)),
                  pl.BlockSpec((QB, GM_W), row(3)),
                  pl.BlockSpec((QB, GM_W), row(4)),
                  pl.BlockSpec((QB, MEM_W), row(0)),
                  pl.BlockSpec((QB, MEM_W), row(0)),
                  pl.BlockSpec((MEM_LEN, 2 * MEM_W), full2),
                  pl.BlockSpec((1, GM_W), full2),
                  pl.BlockSpec((1, GM_W), full2),
                  pl.BlockSpec((GM_GROUPS, CHUNK, CHUNK), lambda i: (0, 0, 0)),
                  pl.BlockSpec((CHUNK, GM_GROUPS), full2)],
        out_specs=pl.BlockSpec((QB, D_MODEL), row(0)),
        out_shape=jax.ShapeDtypeStruct((t, D_MODEL), BF16),
        compiler_params=_params(1),
        name="mix_heads",
    )(oc, osel, ow, gates, zuvz, zuvz, zuvz, zuvz, zm, qm, kvm, lng, lnb, wsp, bsp_t)


def _ln_kernel(z_ref, g_ref, b_ref, o_ref):
    z = z_ref[...]
    mu = jnp.mean(z, axis=-1, keepdims=True)
    var = jnp.mean(jnp.square(z - mu), axis=-1, keepdims=True)
    o_ref[...] = (z - mu) * lax.rsqrt(var + LN_EPS) * g_ref[...] + b_ref[...]


def _layernorm(z, g, b, tm=256):
    t, d = z.shape
    tm = min(tm, t)
    return pl.pallas_call(
        _ln_kernel,
        grid=(t // tm,),
        in_specs=[pl.BlockSpec((tm, d), lambda i: (i, 0)),
                  pl.BlockSpec((1, d), lambda i: (0, 0)),
                  pl.BlockSpec((1, d), lambda i: (0, 0))],
        out_specs=pl.BlockSpec((tm, d), lambda i: (i, 0)),
        out_shape=jax.ShapeDtypeStruct((t, d), F32),
        compiler_params=_params(1),
        name="residual_ln",
    )(z, g, b)


def _pool_matrix(nsbp, nch):
    j = np.arange(nsbp)[:, None]
    c = np.arange(nch)[None, :]
    ratio = SLC_BLOCK // CMP_STRIDE
    return ((c >= ratio * j - 1) & (c <= ratio * j + ratio - 1)).astype(np.float32)


def _tile_map(nsbp):
    j = np.arange(nsbp)[:, None]
    n = np.arange(LANES)[None, :]
    return (j // BLK_PER_KT == n).astype(np.float32)


def _block_pick():
    p = np.arange(OH_PERIOD)[:, None, None]
    j = np.arange(LANES)[None, :, None]
    c = np.arange(LANES)[None, None, :]
    return ((c < BLK_PER_KT) & (j == BLK_PER_KT * p + c)).astype(np.float32)


def _key_ext(n_keys, with_blocks):
    l = np.arange(n_keys)[:, None]
    c = np.arange(LANES)[None, :]
    kx = np.zeros((n_keys, LANES), np.float32)
    if with_blocks:
        kx += (c < BLK_PER_KT) & (l // SLC_BLOCK == c)
    kx += np.where((c >= EXT_LO0) & (c < EXT_LO0 + 3), l % 256, 0)
    kx += np.where((c >= EXT_HI0) & (c < EXT_HI0 + 3), l // 256, 0)
    return kx


def _ones_lane0(n_keys):
    vx = np.zeros((n_keys, LANES), np.float32)
    vx[:, 0] = 1.0
    return vx


def _window_band():
    d = np.arange(WINDOW // QB + 1)[:, None, None] * QB
    dist = d + np.arange(QB)[None, :, None] - np.arange(WSPAN)[None, None, :]
    return np.where((dist >= 0) & (dist < WINDOW), 0.0, NEG).astype(np.float32)


def _slope_lanes(cslopes):
    to_bf16 = lambda v: v.astype(BF16).astype(np.float32)
    s1 = to_bf16(cslopes)
    r1 = cslopes - s1
    s2 = to_bf16(r1)
    s3 = to_bf16(r1 - s2)
    pieces = np.stack([s1, s2, s3, 256.0 * s1, 256.0 * s2, 256.0 * s3], axis=-1)
    return np.pad(pieces, ((0, 0), (0, 0), (EXT_LO0, LANES - EXT_LO0 - 6))).astype(np.float32)


def kernel(x, mem, w_in, w_cmp_k1, w_cmp_k2, w_cmp_v1, w_cmp_v2, pe_cmp_k, pe_cmp_v,
           gm_ln_g, gm_ln_b, w_spatial, b_spatial, w_mem_kv, w_out, ln_g, ln_b):
    b, t, _ = x.shape
    assert b == 1 and t % KT == 0 and t >= WSPAN
    alpha = 2.0 ** 0.25
    nch = t // CMP_STRIDE
    nsb = t // SLC_BLOCK
    nsbp = -(-nsb // LANES) * LANES

    gate0 = NSA_W + 6 * KV_W
    w_t = w_in.T.astype(BF16)
    tail0 = gate0 + N_GATES
    x2 = x[0]
    xb = x2.astype(BF16)
    tm, tn = PROJ_TM, PROJ_TN
    proj = functools.partial(_matmul, xb, w_t, trans_b=True)
    qp = proj(BF16, tm, tn, "proj_q", 0, NSA_W, scale=QSCALE)
    kcvc = proj(F32, tm, tn, "proj_kvc", NSA_W, 2 * KV_W)
    kv = proj(BF16, tm, tn, "proj_kv", NSA_W + 2 * KV_W, 4 * KV_W)
    gates = proj(F32, tm, LANES, "proj_gates", gate0, LANES)
    zuvz = proj(F32, tm, tn, "proj_zuvz", tail0, NSA_W + 3 * GM_W)
    qm = proj(BF16, tm, tn, "proj_qm", tail0 + NSA_W + 3 * GM_W, MEM_W)
    zm = proj(F32, tm, tn, "proj_zm", tail0 + NSA_W + 3 * GM_W + MEM_W, MEM_W)
    kvm = _matmul(mem[0].astype(BF16), w_mem_kv.astype(BF16), BF16, MEM_LEN, tn, "mem_kv")

    pe = jnp.stack([pe_cmp_k, pe_cmp_v])
    w1 = jnp.stack([w_cmp_k1, w_cmp_v1]).astype(BF16)
    w2 = jnp.stack([w_cmp_k2, w_cmp_v2]).astype(BF16)
    kvc = _compress(kcvc, pe, w1, w2)

    slopes_np = np.exp2(-8.0 * np.arange(1, N_HEADS + 1, dtype=np.float32) / N_HEADS
                        ).astype(np.float32).reshape(N_KV, HG)
    cslopes_np = (slopes_np * np.float32(np.log2(np.e))).astype(np.float32)
    cslopes = jnp.asarray(cslopes_np)
    sconst = jnp.asarray(_slope_lanes(cslopes_np))
    pool = jnp.asarray(_pool_matrix(nsbp, nch), BF16)

    tmap = jnp.asarray(_tile_map(nsbp), BF16)
    o_c, mb, tf = _cmp_attention(cslopes, qp, kvc[0], kvc[1], pool, tmap)
    tflags = tf[:, :, 0, :t // KT]
    tile_list, tile_cnt = _tile_lists(tflags, t)
    o_s = _sel_attention(tile_list, tile_cnt, cslopes, qp, kv, mb, jnp.asarray(_block_pick(), BF16),
                         jnp.asarray(_key_ext(KT, True), BF16), jnp.asarray(_ones_lane0(KT), BF16), sconst)
    o_w = _win_attention(qp, kv, jnp.asarray(_window_band(), F32),
                         jnp.asarray(_key_ext(WSPAN, False), BF16), jnp.asarray(_ones_lane0(WSPAN), BF16),
                         sconst)

    y = _mix(o_c, o_s, o_w, gates, zuvz, zm, qm, kvm,
             gm_ln_g.reshape(1, GM_W), gm_ln_b.reshape(1, GM_W), w_spatial, b_spatial.T)
    z = _matmul(y, w_out.astype(BF16), F32, PROJ_TM // 2, PROJ_TN, "out_proj",
                residual=x2, res_scale=np.float32(alpha))
    out = _layernorm(z, ln_g.reshape(1, D_MODEL), ln_b.reshape(1, D_MODEL))
    return out[None]
```

```python
import functools

import numpy as np
import jax
import jax.numpy as jnp
from jax import lax
from jax.experimental import pallas as pl
from jax.experimental.pallas import tpu as pltpu

D_MODEL = 4096
DH = 128
N_HEADS = 16
HG = 4
N_KV = 4
NSA_W = N_HEADS * DH
KV_W = N_KV * DH
CMP_LEN = 32
CMP_STRIDE = 16
SLC_BLOCK = 64
TOP_N = 16
WINDOW = 512
QB = 128
CHUNK = 128
GM_W = 1024
GM_GROUPS = 4
GM_DG = GM_W // GM_GROUPS
MEM_W = 1024
MEM_LEN = 256
MEM_HEADS = 4
MEM_DH = MEM_W // MEM_HEADS
LN_EPS = 1e-5
N_GATES = N_HEADS * 3

LANES = 128
KT = 512
BLK_PER_KT = KT // SLC_BLOCK
OH_PERIOD = LANES // BLK_PER_KT
NEG = -2.0 ** 100
PROJ_TM = 1024
PROJ_TN = 1024
VMEM_LIMIT = 48 * 1024 * 1024

F32 = jnp.float32
BF16 = jnp.bfloat16
_NT = (((1,), (1,)), ((), ()))


def _params(n_axes):
    return pltpu.CompilerParams(dimension_semantics=("arbitrary",) * n_axes,
                                vmem_limit_bytes=VMEM_LIMIT)


def _mm_kernel(scale, trans_b, res_scale, a_ref, b_ref, *rest):
    o_ref = rest[-1]
    if trans_b:
        acc = lax.dot_general(a_ref[...], b_ref[...], _NT, preferred_element_type=F32)
    else:
        acc = jnp.dot(a_ref[...], b_ref[...], preferred_element_type=F32)
    if scale is not None:
        acc = acc * scale
    if res_scale is not None:
        acc = res_scale * rest[0][...] + acc
    o_ref[...] = acc.astype(o_ref.dtype)


def _matmul(a, b, out_dtype, tm, tn, name, col0=0, ncols=None, scale=None, trans_b=False,
            residual=None, res_scale=None):
    m, k = a.shape
    n_all = b.shape[0] if trans_b else b.shape[1]
    ncols = n_all - col0 if ncols is None else ncols
    tm = min(tm, m)
    assert ncols % tn == 0 and m % tm == 0
    if trans_b:
        assert col0 % 16 == 0
        b_spec = pl.BlockSpec((pl.Element(tn), pl.Element(k)),
                              lambda j, i: (pl.multiple_of(col0 + j * tn, 16), 0))
    else:
        assert col0 % tn == 0
        b_spec = pl.BlockSpec((k, tn), lambda j, i: (0, col0 // tn + j))
    in_specs = [pl.BlockSpec((tm, k), lambda j, i: (i, 0)), b_spec]
    operands = [a, b]
    if residual is not None:
        in_specs.append(pl.BlockSpec((tm, tn), lambda j, i: (i, j)))
        operands.append(residual)
    return pl.pallas_call(
        functools.partial(_mm_kernel, scale, trans_b, res_scale if residual is not None else None),
        grid=(ncols // tn, m // tm),
        in_specs=in_specs,
        out_specs=pl.BlockSpec((tm, tn), lambda j, i: (i, j)),
        out_shape=jax.ShapeDtypeStruct((m, ncols), out_dtype),
        compiler_params=_params(2),
        name=name,
    )(*operands)


def _gates_cast_kernel(x_ref, w_ref, g_ref, xb_ref):
    xb = x_ref[...].astype(BF16)
    xb_ref[...] = xb
    g_ref[...] = lax.dot_general(xb, w_ref[...], _NT, preferred_element_type=F32)


def _gates_and_cast(x2, w_t, row0, tm=512):
    t, d = x2.shape
    tm = min(tm, t)
    return pl.pallas_call(
        _gates_cast_kernel,
        grid=(t // tm,),
        in_specs=[pl.BlockSpec((tm, d), lambda i: (i, 0)),
                  pl.BlockSpec((pl.Element(LANES), pl.Element(d)), lambda i: (row0, 0))],
        out_specs=[pl.BlockSpec((tm, LANES), lambda i: (i, 0)),
                   pl.BlockSpec((tm, d), lambda i: (i, 0))],
        out_shape=[jax.ShapeDtypeStruct((t, LANES), F32), jax.ShapeDtypeStruct((t, d), BF16)],
        compiler_params=_params(1),
        name="proj_gates",
    )(x2, w_t)


def _gelu_tanh(x):
    c = np.float32(np.sqrt(2.0 / np.pi))
    return x * (0.5 * (1.0 + jnp.tanh(c * (x + 0.044715 * (x * x * x)))))


def _compress_kernel(x_ref, pe_ref, w1_ref, w2_ref, o_ref):
    nch = o_ref.shape[2]
    a = jnp.zeros((nch, DH), F32)
    b = jnp.zeros((nch, DH), F32)
    for r in range(CMP_STRIDE):
        xr = x_ref[pl.ds(r, nch, stride=CMP_STRIDE), :]
        lo = (xr + pe_ref[0, r:r + 1, :]).astype(BF16)
        hi = (xr + pe_ref[0, CMP_STRIDE + r:CMP_STRIDE + r + 1, :]).astype(BF16)
        a = a + jnp.dot(lo, w1_ref[0, r * DH:(r + 1) * DH, :], preferred_element_type=F32)
        b = b + jnp.dot(hi, w1_ref[0, (CMP_STRIDE + r) * DH:(CMP_STRIDE + r + 1) * DH, :],
                        preferred_element_type=F32)
    pre = a + pltpu.roll(b, nch - 1, axis=0)
    h = _gelu_tanh(pre).astype(BF16)
    o_ref[0, 0] = jnp.dot(h, w2_ref[0], preferred_element_type=F32).astype(o_ref.dtype)


def _compress(kcvc, pe, w1, w2):
    t = kcvc.shape[0]
    nch = t // CMP_STRIDE
    return pl.pallas_call(
        _compress_kernel,
        grid=(2, N_KV),
        in_specs=[pl.BlockSpec((t, DH), lambda s, j: (0, s * N_KV + j)),
                  pl.BlockSpec((1, CMP_LEN, DH), lambda s, j: (s, 0, 0)),
                  pl.BlockSpec((1, CMP_LEN * DH, DH), lambda s, j: (s, 0, 0)),
                  pl.BlockSpec((1, DH, DH), lambda s, j: (s, 0, 0))],
        out_specs=pl.BlockSpec((1, 1, nch, DH), lambda s, j: (s, j, 0, 0)),
        out_shape=jax.ShapeDtypeStruct((2, N_KV, nch, DH), BF16),
        compiler_params=_params(2),
        name="compress_kv",
    )(kcvc, pe, w1, w2)


def _stack_heads(q):
    return jnp.concatenate([q[:, h * DH:(h + 1) * DH] for h in range(HG)], axis=0)


def _store_heads(o_ref, o, blk=0, rows=QB):
    for h in range(HG):
        o_ref[blk * rows:(blk + 1) * rows, h * DH:(h + 1) * DH] = o[h * rows:(h + 1) * rows, :]


CMP_QBLOCKS = 4
CMP_COL_STEP = 256
FORCED_BLOCKS = 3
QSCALE = np.float32(DH ** -0.5 * np.log2(np.e))


def _cmp_scores(g, i0, w, cs_ref, q_ref, kc_ref, vc_ref, oc_ref):
    blocks = range(CMP_QBLOCKS)
    kc = kc_ref[0, 0:w, :]
    vc = vc_ref[0, 0:w, :]
    row = lax.broadcasted_iota(jnp.int32, (QB, 1), 0)
    ss, maskbs, cposfs, row_oks = [], [], [], []
    for blk in blocks:
        q0 = (i0 + blk) * QB
        qs = _stack_heads(q_ref[blk * QB:(blk + 1) * QB, :])
        ss.append(lax.dot_general(qs, kc, _NT, preferred_element_type=F32))
        cpos = lax.broadcasted_iota(jnp.int32, (1, w), 1) * CMP_STRIDE + (CMP_LEN - 1) - q0
        maskbs.append(jnp.where(cpos <= row, 0.0, NEG))
        cposfs.append(cpos.astype(F32))
        row_oks.append(jnp.where(q0 + row >= CMP_LEN - 1, 1.0, 0.0))

    imps = [jnp.zeros((QB, w), F32) for _ in blocks]
    for h in range(HG):
        pns = []
        for blk in blocks:
            t = ss[blk][h * QB:(h + 1) * QB] + cs_ref[g, h] * cposfs[blk] + maskbs[blk]
            m = jnp.max(t, axis=1, keepdims=True)
            p = jnp.exp2(t - m)
            l = jnp.sum(p, axis=1, keepdims=True)
            pn = p * (row_oks[blk] / jnp.maximum(l, 1e-30))
            imps[blk] = imps[blk] + pn
            pns.append(pn.astype(BF16))
        oc_ref[:, h * DH:(h + 1) * DH] = jnp.dot(jnp.concatenate(pns, axis=0), vc,
                                                 preferred_element_type=F32)
    return jnp.concatenate(imps, axis=0)


def _cmp_select(i0, w, r, imp, pool_ref, tmap_ref, mb_ref, tf_ref):
    nq = imp.shape[0]
    nsbp = mb_ref.shape[1]
    n_sel = min(TOP_N, r)
    pool = pool_ref[0:r, 0:w]
    p1 = imp.astype(BF16)
    r1 = imp - p1.astype(F32)
    p2 = r1.astype(BF16)
    p3 = (r1 - p2.astype(F32)).astype(BF16)
    imps = (lax.dot_general(pool, p1, _NT, preferred_element_type=F32)
            + lax.dot_general(pool, p2, _NT, preferred_element_type=F32)
            + lax.dot_general(pool, p3, _NT, preferred_element_type=F32))

    j = lax.broadcasted_iota(jnp.int32, (r, nq), 0)
    tq = i0 * QB + lax.broadcasted_iota(jnp.int32, (r, nq), 1)
    cur = lax.shift_right_arithmetic(tq, SLC_BLOCK.bit_length() - 1)
    forced = jnp.where(j == 0, 1, jnp.where(j == cur, 1, jnp.where(j == cur - 1, 1, 0)))
    sc = jnp.where(j > cur, -1.0, jnp.where(forced == 1, -2.0, imps))
    for _ in range(n_sel - FORCED_BLOCKS):
        mx = jnp.max(sc, axis=0, keepdims=True)
        idx = jnp.min(jnp.where(sc == mx, j, r), axis=0, keepdims=True)
        sc = jnp.where(j == idx, -2.0, sc)
    bias = jnp.where(j <= cur, jnp.where(sc == -2.0, 0.0, NEG), NEG)
    for blk in range(nq // QB):
        rows_out = slice(blk * QB, (blk + 1) * QB)
        bias_t = bias[:, rows_out].T
        mb_ref[rows_out, 0:r] = bias_t.astype(BF16)
        if r < nsbp:
            mb_ref[rows_out, r:nsbp] = jnp.full((QB, nsbp - r), NEG, BF16)
        anyq = jnp.max(jnp.where(bias_t == 0.0, 1.0, 0.0), axis=0, keepdims=True)
        hit = jnp.dot(jnp.broadcast_to(anyq, (8, r)).astype(BF16), tmap_ref[0:r, :],
                      preferred_element_type=F32)
        tf_ref[0, blk] = jnp.where(hit > 0.0, 1, 0).astype(jnp.int32)


def _cmp_kernel(cs_ref, q_ref, kc_ref, vc_ref, pool_ref, tmap_ref, oc_ref, mb_ref, tf_ref):
    g = pl.program_id(0)
    i0 = pl.program_id(1) * CMP_QBLOCKS
    nch = kc_ref.shape[1]
    nsbp = pool_ref.shape[0]
    step = min(CMP_COL_STEP, nch)
    need = (i0 + CMP_QBLOCKS - 1) * (QB // CMP_STRIDE) + (QB - CMP_LEN) // CMP_STRIDE
    variant = lax.shift_right_logical(need, step.bit_length() - 1)
    for v in range(nch // step):
        w = (v + 1) * step
        r = min(nsbp, -(-(w // (SLC_BLOCK // CMP_STRIDE)) // LANES) * LANES)

        @pl.when(variant == v)
        def _(w=w, r=r):
            imp = _cmp_scores(g, i0, w, cs_ref, q_ref, kc_ref, vc_ref, oc_ref)
            _cmp_select(i0, w, r, imp, pool_ref, tmap_ref, mb_ref, tf_ref)


def _cmp_attention(cslopes, qf, kc, vc, pool, tmap):
    t = qf.shape[0]
    nch = kc.shape[1]
    nsbp = pool.shape[0]
    rows = CMP_QBLOCKS * QB
    return pl.pallas_call(
        _cmp_kernel,
        grid=(N_KV, t // rows),
        in_specs=[pl.BlockSpec(memory_space=pltpu.SMEM),
                  pl.BlockSpec((rows, HG * DH), lambda g, i: (i, g)),
                  pl.BlockSpec((1, nch, DH), lambda g, i: (g, 0, 0)),
                  pl.BlockSpec((1, nch, DH), lambda g, i: (g, 0, 0)),
                  pl.BlockSpec((nsbp, nch), lambda g, i: (0, 0)),
                  pl.BlockSpec((nsbp, LANES), lambda g, i: (0, 0))],
        out_specs=[pl.BlockSpec((rows, HG * DH), lambda g, i: (i, g)),
                   pl.BlockSpec((rows, nsbp), lambda g, i: (i, g)),
                   pl.BlockSpec((1, CMP_QBLOCKS, 8, LANES), lambda g, i: (g, i, 0, 0))],
        out_shape=[jax.ShapeDtypeStruct((t, NSA_W), F32),
                   jax.ShapeDtypeStruct((t, N_KV * nsbp), BF16),
                   jax.ShapeDtypeStruct((N_KV, t // QB, 8, LANES), jnp.int32)],
        compiler_params=_params(2),
        name="cmp_attn_select",
    )(cslopes, qf, kc, vc, pool, tmap)


SEL_QB = 256
EXT_LO0 = BLK_PER_KT
EXT_HI0 = BLK_PER_KT + 3


def _aug_queries(qs, ext, sconst_ref, rows=QB):
    parts = []
    for h in range(HG):
        e = (ext + sconst_ref[0, h:h + 1, :]).astype(BF16)
        parts.append(jnp.concatenate([qs[h * rows:(h + 1) * rows], e], axis=1))
    return jnp.concatenate(parts, axis=0)


def _sel_kernel(lst_ref, cnt_ref, cs_ref, q_ref, ks_ref, vs_ref, mb_ref, pick_ref, kx_ref, vx_ref,
                sconst_ref, o_ref, m_scr, acc_scr, u_scr, p_scr, al_scr):
    g = pl.program_id(0)
    i = pl.program_id(1)
    q0 = i * SEL_QB
    nhalf = mb_ref.shape[1] // LANES
    n_prev = cnt_ref[g, i]

    qs = _stack_heads(q_ref[...])
    m_scr[...] = jnp.full(m_scr.shape, NEG, F32)
    acc_scr[...] = jnp.zeros(acc_scr.shape, F32)
    p_scr[1] = jnp.zeros(p_scr.shape[1:], BF16)
    al_scr[1] = jnp.ones(al_scr.shape[1:], F32)

    def score_stage(n, slot):
        tn = lst_ref[g, i, n]
        k0 = pl.multiple_of(tn * KT, KT)
        kaug = jnp.concatenate([ks_ref[pl.ds(k0, KT), :], kx_ref[...]], axis=1)
        half = lax.shift_right_logical(tn, OH_PERIOD.bit_length() - 1)
        mbh = mb_ref[:, 0:LANES]
        for hf in range(1, nhalf):
            mbh = jnp.where(half == hf, mb_ref[:, hf * LANES:(hf + 1) * LANES], mbh)
        ext = jnp.dot(mbh, pick_ref[tn & (OH_PERIOD - 1)], preferred_element_type=F32)
        u_scr[slot] = lax.dot_general(_aug_queries(qs, ext, sconst_ref, SEL_QB), kaug, _NT,
                                      preferred_element_type=F32)

    def softmax_stage(n, slot, causal):
        k0 = lst_ref[g, i, n] * KT
        dq = (q0 - k0).astype(F32)
        if causal:
            visible = (lax.broadcasted_iota(jnp.int32, (SEL_QB, KT), 0)
                       - lax.broadcasted_iota(jnp.int32, (SEL_QB, KT), 1) + (q0 - k0)) >= 0
        for h in range(HG):
            rows = pl.ds(h * SEL_QB, SEL_QB)
            uh = u_scr[slot, rows, :]
            if causal:
                uh = jnp.where(visible, uh, NEG)
            c = cs_ref[g, h] * dq
            m_prev = m_scr[rows, :]
            m_next = jnp.maximum(m_prev, jnp.max(uh, axis=1, keepdims=True) - c)
            p_scr[slot, rows, :] = jnp.exp2(uh - (m_next + c)).astype(BF16)
            al_scr[slot, rows, :] = jnp.exp2(m_prev - m_next)
            m_scr[rows, :] = m_next

    def pv_stage(n, slot):
        k0 = pl.multiple_of(lst_ref[g, i, jnp.maximum(n, 0)] * KT, KT)
        vaug = jnp.concatenate([vs_ref[pl.ds(k0, KT), :], vx_ref[...]], axis=1)
        pv = jnp.dot(p_scr[slot], vaug, preferred_element_type=F32)
        acc_scr[...] = al_scr[slot] * acc_scr[...] + pv

    def finish():
        acc = acc_scr[...]
        _store_heads(o_ref, acc[:, 0:DH] / acc[:, DH:DH + 1], rows=SEL_QB)

    score_stage(0, 0)

    def pair(j, carry):
        n = 2 * j
        score_stage(n + 1, 1)
        softmax_stage(n, 0, False)
        pv_stage(n - 1, 1)
        score_stage(n + 2, 0)
        softmax_stage(n + 1, 1, False)
        pv_stage(n, 0)
        return carry

    lax.fori_loop(0, lax.shift_right_logical(n_prev, 1), pair, 0)
    n_even = n_prev & ~1

    @pl.when(n_prev == n_even)
    def _():
        softmax_stage(n_even, 0, True)
        pv_stage(n_even - 1, 1)
        pv_stage(n_even, 0)
        finish()

    @pl.when(n_prev != n_even)
    def _():
        score_stage(n_even + 1, 1)
        pv_stage(n_even - 1, 1)
        softmax_stage(n_even, 0, False)
        softmax_stage(n_even + 1, 1, True)
        pv_stage(n_even, 0)
        pv_stage(n_even + 1, 1)
        finish()


def _sel_attention(tile_list, tile_cnt, cslopes, qf, projb, mb, pick, kx, vx, sconst):
    t = qf.shape[0]
    nsbp = mb.shape[1] // N_KV
    grid_spec = pltpu.PrefetchScalarGridSpec(
        num_scalar_prefetch=2,
        grid=(N_KV, t // SEL_QB),
        in_specs=[pl.BlockSpec(memory_space=pltpu.SMEM),
                  pl.BlockSpec((SEL_QB, HG * DH), lambda g, i, *_: (i, g)),
                  pl.BlockSpec((t, DH), lambda g, i, *_: (0, g)),
                  pl.BlockSpec((t, DH), lambda g, i, *_: (0, N_KV + g)),
                  pl.BlockSpec((SEL_QB, nsbp), lambda g, i, *_: (i, g)),
                  pl.BlockSpec((OH_PERIOD, LANES, LANES), lambda g, i, *_: (0, 0, 0)),
                  pl.BlockSpec((KT, LANES), lambda g, i, *_: (0, 0)),
                  pl.BlockSpec((KT, LANES), lambda g, i, *_: (0, 0)),
                  pl.BlockSpec((1, HG, LANES), lambda g, i, *_: (g, 0, 0))],
        out_specs=pl.BlockSpec((SEL_QB, HG * DH), lambda g, i, *_: (i, g)),
        scratch_shapes=[pltpu.VMEM((HG * SEL_QB, 1), F32),
                        pltpu.VMEM((HG * SEL_QB, 2 * LANES), F32),
                        pltpu.VMEM((2, HG * SEL_QB, KT), F32),
                        pltpu.VMEM((2, HG * SEL_QB, KT), BF16),
                        pltpu.VMEM((2, HG * SEL_QB, 1), F32)])
    return pl.pallas_call(
        _sel_kernel,
        grid_spec=grid_spec,
        out_shape=jax.ShapeDtypeStruct((t, NSA_W), F32),
        compiler_params=_params(2),
        name="sel_attn",
    )(tile_list, tile_cnt, cslopes, qf, projb, projb, mb, pick, kx, vx, sconst)


def _tile_lists(tflags, t):
    nt = t // KT
    nq = t // SEL_QB
    tflags = jnp.max(tflags.reshape(N_KV, nq, SEL_QB // QB, nt), axis=2)
    last = (jnp.arange(nq, dtype=jnp.int32) * SEL_QB + SEL_QB - 1) // KT
    ids = jnp.arange(nt, dtype=jnp.int32)
    active = (tflags != 0) & (ids[None, None, :] < last[None, :, None])
    cnt = jnp.sum(active, axis=-1).astype(jnp.int32)
    order = jnp.argsort(jnp.where(active, ids, nt + ids), axis=-1).astype(jnp.int32)
    lst = jnp.where(ids[None, None, :] == cnt[..., None], last[None, :, None], order)
    return lst, cnt


WSPAN = WINDOW + QB


WIN_QBLOCKS = 8


def _win_kernel(q_ref, kw_ref, vw_ref, band_ref, kx_ref, vx_ref, sconst_ref, o_ref):
    blocks = range(WIN_QBLOCKS)
    q0s = [(pl.program_id(1) * WIN_QBLOCKS + blk) * QB for blk in blocks]
    starts = [pl.multiple_of(jnp.maximum(q0 - WINDOW, 0), QB) for q0 in q0s]
    us = []
    for blk in blocks:
        qs = _stack_heads(q_ref[blk * QB:(blk + 1) * QB, :])
        kaug = jnp.concatenate([kw_ref[pl.ds(starts[blk], WSPAN), :], kx_ref[...]], axis=1)
        us.append(lax.dot_general(_aug_queries(qs, jnp.zeros((QB, LANES), F32), sconst_ref), kaug, _NT,
                                  preferred_element_type=F32))
    pss = []
    for blk in blocks:
        band = band_ref[lax.shift_right_logical(q0s[blk] - starts[blk], QB.bit_length() - 1)]
        ps = []
        for h in range(HG):
            uh = us[blk][h * QB:(h + 1) * QB] + band
            m = jnp.max(uh, axis=1, keepdims=True)
            ps.append(jnp.exp2(uh - m).astype(BF16))
        pss.append(jnp.concatenate(ps, axis=0))
    for blk in blocks:
        vaug = jnp.concatenate([vw_ref[pl.ds(starts[blk], WSPAN), :], vx_ref[...]], axis=1)
        pv = jnp.dot(pss[blk], vaug, preferred_element_type=F32)
        _store_heads(o_ref, pv[:, 0:DH] / pv[:, DH:DH + 1], blk)


def _win_attention(qf, projb, band, kxw, vxw, sconst):
    t = qf.shape[0]
    rows = WIN_QBLOCKS * QB
    return pl.pallas_call(
        _win_kernel,
        grid=(N_KV, t // rows),
        in_specs=[pl.BlockSpec((rows, HG * DH), lambda g, i: (i, g)),
                  pl.BlockSpec((t, DH), lambda g, i: (0, 2 * N_KV + g)),
                  pl.BlockSpec((t, DH), lambda g, i: (0, 3 * N_KV + g)),
                  pl.BlockSpec((WINDOW // QB + 1, QB, WSPAN), lambda g, i: (0, 0, 0)),
                  pl.BlockSpec((WSPAN, LANES), lambda g, i: (0, 0)),
                  pl.BlockSpec((WSPAN, LANES), lambda g, i: (0, 0)),
                  pl.BlockSpec((1, HG, LANES), lambda g, i: (g, 0, 0))],
        out_specs=pl.BlockSpec((rows, HG * DH), lambda g, i: (i, g)),
        out_shape=jax.ShapeDtypeStruct((t, NSA_W), F32),
        compiler_params=_params(2),
        name="win_attn",
    )(qf, projb, projb, band, kxw, vxw, sconst)


def _silu(z):
    return z * jax.nn.sigmoid(z)


def _mix_kernel(oc_ref, os_ref, ow_ref, gates_ref, zn_ref, u_ref, v_ref, zg_ref, zm_ref,
                qm_ref, kvm_ref, lng_ref, lnb_ref, wsp_ref, bsp_ref, y_ref):
    mscale = np.float32(MEM_DH ** -0.5)
    msl = [slice(h * MEM_DH, (h + 1) * MEM_DH) for h in range(MEM_HEADS)]

    scores = [lax.dot_general(qm_ref[:, msl[h]], kvm_ref[:, msl[h]], _NT, preferred_element_type=F32) * mscale
              for h in range(MEM_HEADS)]

    vv = v_ref[...]
    mu = jnp.mean(vv, axis=-1, keepdims=True)
    var = jnp.mean(jnp.square(vv - mu), axis=-1, keepdims=True)
    vn = ((vv - mu) * lax.rsqrt(var + LN_EPS) * lng_ref[...] + lnb_ref[...]).astype(BF16)
    causal = (lax.broadcasted_iota(jnp.int32, (CHUNK, CHUNK), 0)
              >= lax.broadcasted_iota(jnp.int32, (CHUNK, CHUNK), 1))
    svs = []
    for gg in range(GM_GROUPS):
        ws = jnp.where(causal, wsp_ref[gg], 0.0).astype(BF16)
        svs.append(jnp.dot(ws, vn[:, gg * GM_DG:(gg + 1) * GM_DG], preferred_element_type=F32)
                   + bsp_ref[:, gg:gg + 1])

    probs = []
    for h in range(MEM_HEADS):
        m = jnp.max(scores[h], axis=-1, keepdims=True)
        e = jnp.exp(scores[h] - m)
        probs.append((e / jnp.sum(e, axis=-1, keepdims=True)).astype(BF16))

    sig = jax.nn.sigmoid(gates_ref[...])
    for h in range(N_HEADS):
        sl = slice(h * DH, (h + 1) * DH)
        comb = (sig[:, 3 * h:3 * h + 1] * oc_ref[:, sl]
                + sig[:, 3 * h + 1:3 * h + 2] * os_ref[:, sl]
                + sig[:, 3 * h + 2:3 * h + 3] * ow_ref[:, sl])
        y_ref[:, sl] = (comb * _silu(zn_ref[:, sl])).astype(y_ref.dtype)

    for h in range(MEM_HEADS):
        vm = kvm_ref[:, MEM_W + h * MEM_DH:MEM_W + (h + 1) * MEM_DH]
        o = jnp.dot(probs[h], vm, preferred_element_type=F32)
        y_ref[:, NSA_W + GM_W + h * MEM_DH:NSA_W + GM_W + (h + 1) * MEM_DH] = (
            o * _silu(zm_ref[:, msl[h]])).astype(y_ref.dtype)

    for gg in range(GM_GROUPS):
        sl = slice(gg * GM_DG, (gg + 1) * GM_DG)
        y_ref[:, NSA_W + gg * GM_DG:NSA_W + (gg + 1) * GM_DG] = (
            u_ref[:, sl] * svs[gg] * _silu(zg_ref[:, sl])).astype(y_ref.dtype)


def _mix(oc, osel, ow, gates, zuvz, zm, qm, kvm, lng, lnb, wsp, bsp_t):
    t = oc.shape[0]
    row = lambda c: (lambda i: (i, c))
    full2 = lambda i: (0, 0)
    return pl.pallas_call(
        _mix_kernel,
        grid=(t // QB,),
        in_specs=[pl.BlockSpec((QB, NSA_W), row(0)),
                  pl.BlockSpec((QB, NSA_W), row(0)),
                  pl.BlockSpec((QB, NSA_W), row(0)),
                  pl.BlockSpec((QB, LANES), row(0)),
                  pl.BlockSpec((QB, NSA_W), row(0)),
                  pl.BlockSpec((QB, GM_W), row(2)),
                  pl.BlockSpec((QB, GM_W), row(3)),
                  pl.BlockSpec((QB, GM_W), row(4)),
                  pl.BlockSpec((QB, MEM_W), row(0)),
                  pl.BlockSpec((QB, MEM_W), row(0)),
                  pl.BlockSpec((MEM_LEN, 2 * MEM_W), full2),
                  pl.BlockSpec((1, GM_W), full2),
                  pl.BlockSpec((1, GM_W), full2),
                  pl.BlockSpec((GM_GROUPS, CHUNK, CHUNK), lambda i: (0, 0, 0)),
                  pl.BlockSpec((CHUNK, GM_GROUPS), full2)],
        out_specs=pl.BlockSpec((QB, D_MODEL), row(0)),
        out_shape=jax.ShapeDtypeStruct((t, D_MODEL), BF16),
        compiler_params=_params(1),
        name="mix_heads",
    )(oc, osel, ow, gates, zuvz, zuvz, zuvz, zuvz, zm, qm, kvm, lng, lnb, wsp, bsp_t)


def _ln_kernel(z_ref, g_ref, b_ref, o_ref):
    z = z_ref[...]
    mu = jnp.mean(z, axis=-1, keepdims=True)
    var = jnp.mean(jnp.square(z - mu), axis=-1, keepdims=True)
    o_ref[...] = (z - mu) * lax.rsqrt(var + LN_EPS) * g_ref[...] + b_ref[...]


def _layernorm(z, g, b, tm=256):
    t, d = z.shape
    tm = min(tm, t)
    return pl.pallas_call(
        _ln_kernel,
        grid=(t // tm,),
        in_specs=[pl.BlockSpec((tm, d), lambda i: (i, 0)),
                  pl.BlockSpec((1, d), lambda i: (0, 0)),
                  pl.BlockSpec((1, d), lambda i: (0, 0))],
        out_specs=pl.BlockSpec((tm, d), lambda i: (i, 0)),
        out_shape=jax.ShapeDtypeStruct((t, d), F32),
        compiler_params=_params(1),
        name="residual_ln",
    )(z, g, b)


def _pool_matrix(nsbp, nch):
    j = np.arange(nsbp)[:, None]
    c = np.arange(nch)[None, :]
    ratio = SLC_BLOCK // CMP_STRIDE
    return ((c >= ratio * j - 1) & (c <= ratio * j + ratio - 1)).astype(np.float32)


def _tile_map(nsbp):
    j = np.arange(nsbp)[:, None]
    n = np.arange(LANES)[None, :]
    return (j // BLK_PER_KT == n).astype(np.float32)


def _block_pick():
    p = np.arange(OH_PERIOD)[:, None, None]
    j = np.arange(LANES)[None, :, None]
    c = np.arange(LANES)[None, None, :]
    return ((c < BLK_PER_KT) & (j == BLK_PER_KT * p + c)).astype(np.float32)


def _key_ext(n_keys, with_blocks):
    l = np.arange(n_keys)[:, None]
    c = np.arange(LANES)[None, :]
    kx = np.zeros((n_keys, LANES), np.float32)
    if with_blocks:
        kx += (c < BLK_PER_KT) & (l // SLC_BLOCK == c)
    kx += np.where((c >= EXT_LO0) & (c < EXT_LO0 + 3), l % 256, 0)
    kx += np.where((c >= EXT_HI0) & (c < EXT_HI0 + 3), l // 256, 0)
    return kx


def _ones_lane0(n_keys):
    vx = np.zeros((n_keys, LANES), np.float32)
    vx[:, 0] = 1.0
    return vx


def _window_band():
    d = np.arange(WINDOW // QB + 1)[:, None, None] * QB
    dist = d + np.arange(QB)[None, :, None] - np.arange(WSPAN)[None, None, :]
    return np.where((dist >= 0) & (dist < WINDOW), 0.0, NEG).astype(np.float32)


def _slope_lanes(cslopes):
    to_bf16 = lambda v: v.astype(BF16).astype(np.float32)
    s1 = to_bf16(cslopes)
    r1 = cslopes - s1
    s2 = to_bf16(r1)
    s3 = to_bf16(r1 - s2)
    pieces = np.stack([s1, s2, s3, 256.0 * s1, 256.0 * s2, 256.0 * s3], axis=-1)
    return np.pad(pieces, ((0, 0), (0, 0), (EXT_LO0, LANES - EXT_LO0 - 6))).astype(np.float32)


def kernel(x, mem, w_in, w_cmp_k1, w_cmp_k2, w_cmp_v1, w_cmp_v2, pe_cmp_k, pe_cmp_v,
           gm_ln_g, gm_ln_b, w_spatial, b_spatial, w_mem_kv, w_out, ln_g, ln_b):
    b, t, _ = x.shape
    assert b == 1 and t % KT == 0 and t >= WSPAN
    alpha = 2.0 ** 0.25
    nch = t // CMP_STRIDE
    nsb = t // SLC_BLOCK
    nsbp = -(-nsb // LANES) * LANES

    gate0 = NSA_W + 6 * KV_W
    w_t = w_in.T.astype(BF16)
    tail0 = gate0 + N_GATES
    x2 = x[0]
    gates, xb = _gates_and_cast(x2, w_t, gate0)
    tm, tn = PROJ_TM, PROJ_TN
    proj = functools.partial(_matmul, xb, w_t, trans_b=True)
    qp = proj(BF16, tm, tn, "proj_q", 0, NSA_W, scale=QSCALE)
    kcvc = proj(F32, tm, tn, "proj_kvc", NSA_W, 2 * KV_W)
    kv = proj(BF16, tm, tn, "proj_kv", NSA_W + 2 * KV_W, 4 * KV_W)
    zuvz = proj(F32, tm, tn, "proj_zuvz", tail0, NSA_W + 3 * GM_W)
    qm = proj(BF16, tm, tn, "proj_qm", tail0 + NSA_W + 3 * GM_W, MEM_W)
    zm = proj(F32, tm, tn, "proj_zm", tail0 + NSA_W + 3 * GM_W + MEM_W, MEM_W)
    kvm = _matmul(mem[0].astype(BF16), w_mem_kv.astype(BF16), BF16, MEM_LEN, tn, "mem_kv")

    pe = jnp.stack([pe_cmp_k, pe_cmp_v])
    w1 = jnp.stack([w_cmp_k1, w_cmp_v1]).astype(BF16)
    w2 = jnp.stack([w_cmp_k2, w_cmp_v2]).astype(BF16)
    kvc = _compress(kcvc, pe, w1, w2)

    slopes_np = np.exp2(-8.0 * np.arange(1, N_HEADS + 1, dtype=np.float32) / N_HEADS
                        ).astype(np.float32).reshape(N_KV, HG)
    cslopes_np = (slopes_np * np.float32(np.log2(np.e))).astype(np.float32)
    cslopes = jnp.asarray(cslopes_np)
    sconst = jnp.asarray(_slope_lanes(cslopes_np))
    pool = jnp.asarray(_pool_matrix(nsbp, nch), BF16)

    tmap = jnp.asarray(_tile_map(nsbp), BF16)
    o_c, mb, tf = _cmp_attention(cslopes, qp, kvc[0], kvc[1], pool, tmap)
    tflags = tf[:, :, 0, :t // KT]
    tile_list, tile_cnt = _tile_lists(tflags, t)
    o_s = _sel_attention(tile_list, tile_cnt, cslopes, qp, kv, mb, jnp.asarray(_block_pick(), BF16),
                         jnp.asarray(_key_ext(KT, True), BF16), jnp.asarray(_ones_lane0(KT), BF16), sconst)
    o_w = _win_attention(qp, kv, jnp.asarray(_window_band(), F32),
                         jnp.asarray(_key_ext(WSPAN, False), BF16), jnp.asarray(_ones_lane0(WSPAN), BF16),
                         sconst)

    y = _mix(o_c, o_s, o_w, gates, zuvz, zm, qm, kvm,
             gm_ln_g.reshape(1, GM_W), gm_ln_b.reshape(1, GM_W), w_spatial, b_spatial.T)
    z = _matmul(y, w_out.astype(BF16), F32, PROJ_TM // 2, PROJ_TN, "out_proj",
                residual=x2, res_scale=np.float32(alpha))
    out = _layernorm(z, ln_g.reshape(1, D_MODEL), ln_b.reshape(1, D_MODEL))
    return out[None]
```
